```python
import math
import jax
import jax.numpy as jnp
from jax import lax
import numpy as np

D_MODEL = 1024
BATCH = 4
SEQ = 4096
DEPTH = 4
DEC_BATCH = 128
DEC_SEQ = 4
PAST_LEN = 2048
PAGE_SIZE = 128

N_MIXERS = 4
D_PLE = 256
ALPHA_DN = (2 * DEPTH) ** 0.25
BETA_DN = (8 * DEPTH) ** -0.25
LN_EPS = 1e-5

CF_WIDTH = D_MODEL
CF_KERNEL = 31
MB_INNER = 2 * D_MODEL
MB_HEAD_DIM = 64
MB_HEADS = MB_INNER // MB_HEAD_DIM
MB_GROUPS = 8
MB_STATE = 128
MB_CONV = 4
MB_CONV_DIM = MB_INNER + 2 * MB_GROUPS * MB_STATE
MB_CHUNK = 128
AT_HEADS = 16
AT_HEAD_DIM = D_MODEL // AT_HEADS
AT_KV_HEADS = 4
IDX_HEADS = 8
IDX_DIM = 64
TOPK_MAX = 256
Q_BLOCK = 128
GM_WIDTH = 2 * D_MODEL
GM_GROUPS = 4
GM_CHUNK = 128

kernel_name = 'hybrid_conv_ssd_dsa_gmlp_decoder_step'


def layer_norm(x, g, b):
    xf = x.astype(jnp.float32)
    mu = jnp.mean(xf, axis=-1, keepdims=True)
    var = jnp.mean(jnp.square(xf - mu), axis=-1, keepdims=True)
    return ((xf - mu) * lax.rsqrt(var + LN_EPS)).astype(x.dtype) * g + b


def rms_norm(x, g):
    xf = x.astype(jnp.float32)
    return (xf * lax.rsqrt(jnp.mean(jnp.square(xf), axis=-1, keepdims=True) + LN_EPS)).astype(x.dtype) * g


def causal_dwconv(xpad, w, b):
    y = lax.conv_general_dilated(xpad, w[:, None, :], window_strides=(1,), padding='VALID',
                                 dimension_numbers=('NWC', 'WIO', 'NWC'),
                                 feature_group_count=w.shape[1])
    return y + b


def conformer_mixer(x, hist, w_in, b_in, w_dw, b_dw, ln_g, ln_b, w_out, b_out):
    a, gl, z = jnp.split(x @ w_in + b_in, 3, axis=-1)
    u = a * jax.nn.sigmoid(gl)
    upad = jnp.concatenate([hist, u], axis=1)
    c = jax.nn.silu(layer_norm(causal_dwconv(upad, w_dw, b_dw), ln_g, ln_b))
    out = (c * jax.nn.silu(z)) @ w_out + b_out
    return out, (upad[:, -(CF_KERNEL - 1):],)


def ssd_chunked(xh, dt, A, Bm, Cm, S0, chunk):
    Bt, L = xh.shape[:2]
    nc = L // chunk

    def to_chunks(t):
        return jnp.moveaxis(t.reshape((Bt, nc, chunk) + t.shape[2:]), 1, 0)

    tri = jnp.tril(jnp.ones((chunk, chunk), bool))[None, :, :, None, None]

    def step(S, inp):
        xc, dtc, Bc, Cc = inp
        a = dtc.astype(jnp.float32) * A
        cum = jnp.cumsum(a, axis=1)
        seg = cum[:, :, None] - cum[:, None, :]
        decay = jnp.exp(jnp.where(tri, seg, -jnp.inf)).astype(xc.dtype)
        cb = jnp.einsum('btgn,bsgn->btsg', Cc, Bc)
        y_in = jnp.einsum('btsg,btsgr,bsgr,bsgrp->btgrp', cb, decay, dtc, xc)
        y_st = jnp.einsum('btgn,bgrpn,btgr->btgrp', Cc, S, jnp.exp(cum).astype(xc.dtype))
        w_end = jnp.exp(cum[:, -1:] - cum).astype(xc.dtype) * dtc
        S_new = (jnp.exp(cum[:, -1]).astype(S.dtype)[..., None, None] * S
                 + jnp.einsum('bsgr,bsgn,bsgrp->bgrpn', w_end, Bc, xc).astype(S.dtype))
        return S_new, y_in + y_st

    S_fin, ys = lax.scan(step, S0, (to_chunks(xh), to_chunks(dt), to_chunks(Bm), to_chunks(Cm)))
    return jnp.moveaxis(ys, 0, 1).reshape(xh.shape), S_fin


def mamba2_mixer(x, conv_hist, ssm_state, w_in, w_conv, b_conv, dt_bias, a_log, d_skip, norm_g, w_out):
    Bt, L, _ = x.shape
    R = MB_HEADS // MB_GROUPS
    z, xbc, dt = jnp.split(x @ w_in, [MB_INNER, MB_INNER + MB_CONV_DIM], axis=-1)
    xpad = jnp.concatenate([conv_hist, xbc], axis=1)
    xbc = jax.nn.silu(causal_dwconv(xpad, w_conv, b_conv))
    xs, Bm, Cm = jnp.split(xbc, [MB_INNER, MB_INNER + MB_GROUPS * MB_STATE], axis=-1)
    xs = xs.reshape(Bt, L, MB_GROUPS, R, MB_HEAD_DIM)
    Bm = Bm.reshape(Bt, L, MB_GROUPS, MB_STATE)
    Cm = Cm.reshape(Bt, L, MB_GROUPS, MB_STATE)
    dt = jax.nn.softplus(dt + dt_bias).reshape(Bt, L, MB_GROUPS, R)
    A = -jnp.exp(a_log.astype(jnp.float32)).reshape(MB_GROUPS, R)
    S0 = ssm_state.reshape(Bt, MB_GROUPS, R, MB_HEAD_DIM, MB_STATE)
    y, S = ssd_chunked(xs, dt, A, Bm, Cm, S0, min(MB_CHUNK, L))
    y = (y + d_skip.reshape(MB_GROUPS, R)[:, :, None] * xs).reshape(Bt, L, MB_INNER)
    y = rms_norm(y * jax.nn.silu(z), norm_g)
    return y @ w_out, (xpad[:, -(MB_CONV - 1):], S.reshape(Bt, MB_HEADS, MB_HEAD_DIM, MB_STATE))


def alibi_slopes():
    s = np.float32(2.0) ** (-8.0 * np.arange(1, AT_HEADS + 1, dtype=np.float32) / AT_HEADS)
    return jnp.asarray(s, jnp.float32).reshape(AT_KV_HEADS, AT_HEADS // AT_KV_HEADS)


def dsa_project(x, w_in, ki_g, ki_b):
    Bt, L, _ = x.shape
    R = AT_HEADS // AT_KV_HEADS
    sizes = [AT_HEADS * AT_HEAD_DIM, AT_KV_HEADS * AT_HEAD_DIM, AT_KV_HEADS * AT_HEAD_DIM,
             IDX_HEADS * IDX_DIM, IDX_DIM, IDX_HEADS, AT_HEADS * AT_HEAD_DIM]
    q, k, v, qi, ki, wi, z = jnp.split(x @ w_in, np.cumsum(sizes)[:-1].tolist(), axis=-1)
    q = q.reshape(Bt, L, AT_KV_HEADS, R, AT_HEAD_DIM)
    k = k.reshape(Bt, L, AT_KV_HEADS, AT_HEAD_DIM)
    v = v.reshape(Bt, L, AT_KV_HEADS, AT_HEAD_DIM)
    qi = qi.reshape(Bt, L, IDX_HEADS, IDX_DIM)
    ki = layer_norm(ki, ki_g, ki_b)
    return q, k, v, qi, ki, wi, z


def dsa_block(q, qi, wi, q_pos, k_all, v_all, ki_all, topk):
    S = k_all.shape[1]
    causal = jnp.arange(S, dtype=jnp.int32)[None, :] <= q_pos[:, None]
    sc = jnp.einsum('btid,bsd->btis', qi, ki_all).astype(jnp.float32) * (IDX_DIM ** -0.5)
    idx = jnp.einsum('btis,bti->bts', jax.nn.relu(sc), wi.astype(jnp.float32) * (IDX_HEADS ** -0.5))
    idx = jnp.where(causal[None], idx, -jnp.inf)
    _, sel = lax.top_k(idx, topk)
    bidx = jnp.arange(k_all.shape[0])[:, None, None]
    k_sel = k_all[bidx, sel]
    v_sel = v_all[bidx, sel]
    dist = (q_pos[None, :, None] - sel).astype(jnp.float32)
    valid = dist >= 0.0
    logits = jnp.einsum('btgrd,btkgd->btgrk', q, k_sel).astype(jnp.float32) * (AT_HEAD_DIM ** -0.5)
    logits = logits - alibi_slopes()[None, None, :, :, None] * dist[:, :, None, None, :]
    logits = jnp.where(valid[:, :, None, None, :], logits, -jnp.inf)
    p = jax.nn.softmax(logits, axis=-1).astype(v_sel.dtype)
    return jnp.einsum('btgrk,btkgd->btgrd', p, v_sel)


def dsa_prompt(x, w_in, ki_g, ki_b, w_out):
    Bt, L, _ = x.shape
    q, k, v, qi, ki, wi, z = dsa_project(x, w_in, ki_g, ki_b)
    qb = min(Q_BLOCK, L)
    nb = L // qb
    topk = min(TOPK_MAX, L // 4)

    def blocks(t):
        return jnp.moveaxis(t.reshape((Bt, nb, qb) + t.shape[2:]), 1, 0)

    pos = jnp.arange(L, dtype=jnp.int32).reshape(nb, qb)
    o = lax.map(lambda a: dsa_block(a[0], a[1], a[2], a[3], k, v, ki, topk),
                (blocks(q), blocks(qi), blocks(wi), pos))
    o = jnp.moveaxis(o, 0, 1).reshape(Bt, L, AT_HEADS * AT_HEAD_DIM)
    return (o * jax.nn.silu(z)) @ w_out, (k, v, ki)


def dsa_sample(x, cache_k, cache_v, cache_kidx, page_table, w_in, ki_g, ki_b, w_out):
    Bt, T, _ = x.shape
    q, k, v, qi, ki, wi, z = dsa_project(x, w_in, ki_g, ki_b)
    past = page_table.shape[1] * cache_k.shape[1]

    def gather(c):
        return c[page_table].reshape((Bt, past) + c.shape[2:])

    k_all = jnp.concatenate([gather(cache_k), k], axis=1)
    v_all = jnp.concatenate([gather(cache_v), v], axis=1)
    ki_all = jnp.concatenate([gather(cache_kidx), ki], axis=1)
    q_pos = past + jnp.arange(T, dtype=jnp.int32)
    topk = min(TOPK_MAX, (past + T) // 4)
    o = dsa_block(q, qi, wi, q_pos, k_all, v_all, ki_all, topk).reshape(Bt, T, AT_HEADS * AT_HEAD_DIM)
    return (o * jax.nn.silu(z)) @ w_out, (k, v, ki)


def gmlp_mixer(x, w_in, b_in, ln_g, ln_b, w_s, b_s, w_out):
    Bt, L, _ = x.shape
    u, v, z = jnp.split(x @ w_in + b_in, 3, axis=-1)
    u = jax.nn.gelu(u)
    v = layer_norm(jax.nn.gelu(v), ln_g, ln_b)
    Q = min(GM_CHUNK, L)
    nc = L // Q
    W = jnp.tril(w_s[:, :Q, :Q])
    vg = v.reshape(Bt, nc, Q, GM_GROUPS, GM_WIDTH // GM_GROUPS)
    s = jnp.einsum('gts,bcsgd->bctgd', W, vg) + b_s[:, :Q].T[None, None, :, :, None]
    s = s.reshape(Bt, L, GM_WIDTH)
    return (u * s * jax.nn.silu(z)) @ w_out, (v,)


def run_trunk(x, p, mixers, post_ln_g, post_ln_b, ple_w, ple_gate_w):
    states = []
    for i in range(DEPTH):
        out, st = mixers[i % N_MIXERS](x)
        states.append(st)
        h = layer_norm(ALPHA_DN * x + out, post_ln_g[i], post_ln_b[i])
        x = h + jax.nn.sigmoid(h @ ple_gate_w[i]) * (p[i] @ ple_w[i])
    return x, states


def setup_inputs(seed: int = 0) -> dict:
    key = jax.random.key(seed)
    ks = list(jax.random.split(key, 64))
    f32 = jnp.float32

    def nrm(shape, scale=1.0):
        return scale * jax.random.normal(ks.pop(), shape, f32)

    def gain(shape):
        return 1.0 + nrm(shape, 0.05)

    D = D_MODEL
    n_pages = PAST_LEN // PAGE_SIZE
    n_pool = (DEC_BATCH * n_pages * 5) // 4
    at_in = (2 * AT_HEADS * AT_HEAD_DIM + 2 * AT_KV_HEADS * AT_HEAD_DIM
             + IDX_HEADS * IDX_DIM + IDX_DIM + IDX_HEADS)
    out = {}
    out['x_prompt'] = nrm((BATCH, SEQ, D))
    out['x_sample'] = nrm((DEC_BATCH, DEC_SEQ, D))
    out['p_prompt'] = nrm((DEPTH, BATCH, SEQ, D_PLE))
    out['p_sample'] = nrm((DEPTH, DEC_BATCH, DEC_SEQ, D_PLE))
    out['state_cf_conv'] = nrm((DEC_BATCH, CF_KERNEL - 1, CF_WIDTH), 0.5)
    out['state_mb_conv'] = nrm((DEC_BATCH, MB_CONV - 1, MB_CONV_DIM))
    out['state_mb_ssm'] = nrm((DEC_BATCH, MB_HEADS, MB_HEAD_DIM, MB_STATE), 0.1)
    out['cache_k'] = nrm((n_pool, PAGE_SIZE, AT_KV_HEADS, AT_HEAD_DIM))
    out['cache_v'] = nrm((n_pool, PAGE_SIZE, AT_KV_HEADS, AT_HEAD_DIM))
    out['cache_kidx'] = nrm((n_pool, PAGE_SIZE, IDX_DIM))
    out['page_table'] = jax.random.permutation(ks.pop(), n_pool)[: DEC_BATCH * n_pages].reshape(
        DEC_BATCH, n_pages).astype(jnp.int32)
    out['post_ln_g'] = gain((DEPTH, D))
    out['post_ln_b'] = nrm((DEPTH, D), 0.02)
    out['ple_w'] = nrm((DEPTH, D_PLE, D), D_PLE ** -0.5)
    out['ple_gate_w'] = nrm((DEPTH, D, D), D ** -0.5)
    out['cf_w_in'] = nrm((D, 3 * CF_WIDTH), D ** -0.5)
    out['cf_b_in'] = nrm((3 * CF_WIDTH,), 0.02)
    out['cf_w_dw'] = nrm((CF_KERNEL, CF_WIDTH), CF_KERNEL ** -0.5)
    out['cf_b_dw'] = nrm((CF_WIDTH,), 0.02)
    out['cf_ln_g'] = gain((CF_WIDTH,))
    out['cf_ln_b'] = nrm((CF_WIDTH,), 0.02)
    out['cf_w_out'] = nrm((CF_WIDTH, D), BETA_DN * CF_WIDTH ** -0.5)
    out['cf_b_out'] = nrm((D,), 0.02)
    out['mb_w_in'] = nrm((D, MB_INNER + MB_CONV_DIM + MB_HEADS), D ** -0.5)
    out['mb_w_conv'] = nrm((MB_CONV, MB_CONV_DIM), MB_CONV ** -0.5)
    out['mb_b_conv'] = nrm((MB_CONV_DIM,), 0.02)
    dt0 = jnp.exp(jax.random.uniform(ks.pop(), (MB_HEADS,), f32, math.log(1e-3), math.log(1e-1)))
    out['mb_dt_bias'] = dt0 + jnp.log(-jnp.expm1(-dt0))
    out['mb_a_log'] = jnp.log(jax.random.uniform(ks.pop(), (MB_HEADS,), f32, 1.0, 16.0))
    out['mb_d_skip'] = gain((MB_HEADS,))
    out['mb_norm_g'] = gain((MB_INNER,))
    out['mb_w_out'] = nrm((MB_INNER, D), BETA_DN * MB_INNER ** -0.5)
    out['at_w_in'] = nrm((D, at_in), D ** -0.5)
    out['at_ki_ln_g'] = gain((IDX_DIM,))
    out['at_ki_ln_b'] = nrm((IDX_DIM,), 0.02)
    out['at_w_out'] = nrm((AT_HEADS * AT_HEAD_DIM, D), BETA_DN * (AT_HEADS * AT_HEAD_DIM) ** -0.5)
    out['gm_w_in'] = nrm((D, 3 * GM_WIDTH), D ** -0.5)
    out['gm_b_in'] = nrm((3 * GM_WIDTH,), 0.02)
    out['gm_ln_g'] = gain((GM_WIDTH,))
    out['gm_ln_b'] = nrm((GM_WIDTH,), 0.02)
    out['gm_w_s'] = nrm((GM_GROUPS, GM_CHUNK, GM_CHUNK), GM_CHUNK ** -0.5)
    out['gm_b_s'] = gain((GM_GROUPS, GM_CHUNK))
    out['gm_w_out'] = nrm((GM_WIDTH, D), BETA_DN * GM_WIDTH ** -0.5)
    return out


def reference(x_prompt, x_sample, p_prompt, p_sample, state_cf_conv, state_mb_conv, state_mb_ssm,
              cache_k, cache_v, cache_kidx, page_table, post_ln_g, post_ln_b, ple_w, ple_gate_w,
              cf_w_in, cf_b_in, cf_w_dw, cf_b_dw, cf_ln_g, cf_ln_b, cf_w_out, cf_b_out,
              mb_w_in, mb_w_conv, mb_b_conv, mb_dt_bias, mb_a_log, mb_d_skip, mb_norm_g, mb_w_out,
              at_w_in, at_ki_ln_g, at_ki_ln_b, at_w_out,
              gm_w_in, gm_b_in, gm_ln_g, gm_ln_b, gm_w_s, gm_b_s, gm_w_out):
    cf_p = (cf_w_in, cf_b_in, cf_w_dw, cf_b_dw, cf_ln_g, cf_ln_b, cf_w_out, cf_b_out)
    mb_p = (mb_w_in, mb_w_conv, mb_b_conv, mb_dt_bias, mb_a_log, mb_d_skip, mb_norm_g, mb_w_out)
    at_p = (at_w_in, at_ki_ln_g, at_ki_ln_b, at_w_out)
    gm_p = (gm_w_in, gm_b_in, gm_ln_g, gm_ln_b, gm_w_s, gm_b_s, gm_w_out)

    Bp = x_prompt.shape[0]
    dtp = x_prompt.dtype
    cf_h0 = jnp.zeros((Bp, CF_KERNEL - 1, CF_WIDTH), dtp)
    mb_c0 = jnp.zeros((Bp, MB_CONV - 1, MB_CONV_DIM), dtp)
    mb_s0 = jnp.zeros((Bp, MB_HEADS, MB_HEAD_DIM, MB_STATE), dtp)
    mix_prompt = (
        lambda x: conformer_mixer(x, cf_h0, *cf_p),
        lambda x: mamba2_mixer(x, mb_c0, mb_s0, *mb_p),
        lambda x: dsa_prompt(x, *at_p),
        lambda x: gmlp_mixer(x, *gm_p),
    )
    y_prompt, st_p = run_trunk(x_prompt, p_prompt, mix_prompt, post_ln_g, post_ln_b, ple_w, ple_gate_w)

    mix_sample = (
        lambda x: conformer_mixer(x, state_cf_conv, *cf_p),
        lambda x: mamba2_mixer(x, state_mb_conv, state_mb_ssm, *mb_p),
        lambda x: dsa_sample(x, cache_k, cache_v, cache_kidx, page_table, *at_p),
        lambda x: gmlp_mixer(x, *gm_p),
    )
    y_sample, st_s = run_trunk(x_sample, p_sample, mix_sample, post_ln_g, post_ln_b, ple_w, ple_gate_w)

    (cf_conv_p,), (mb_conv_p, mb_ssm_p), (k_p, v_p, kidx_p), _prompt_chunk_v = st_p
    (cf_conv_s,), (mb_conv_s, mb_ssm_s), (k_s, v_s, kidx_s), (gm_v_s,) = st_s
    return (y_prompt, y_sample, cf_conv_p, cf_conv_s, mb_conv_p, mb_conv_s, mb_ssm_p, mb_ssm_s,
            k_p, v_p, kidx_p, k_s, v_s, kidx_s, gm_v_s)
```

```python
import functools

import numpy as np
import jax
import jax.numpy as jnp
from jax import lax
from jax.experimental import pallas as pl
from jax.experimental.pallas import tpu as pltpu

F32 = jnp.float32
BF16 = jnp.bfloat16

D_MODEL = 1024
D_PLE = 256
DEPTH = 4
ALPHA_DN = (2 * DEPTH) ** 0.25
LN_EPS = 1e-5

CF_KERNEL = 31
CF_HIST = CF_KERNEL - 1
MB_INNER = 2048
MB_HEAD_DIM = 64
MB_HEADS = 32
MB_GROUPS = 8
MB_STATE = 128
MB_CONV = 4
MB_CONV_DIM = 4096
MB_CHUNK = 128
AT_HEADS = 16
AT_HEAD_DIM = 64
AT_KV_HEADS = 4
AT_KV_DIM = AT_KV_HEADS * AT_HEAD_DIM
IDX_HEADS = 8
IDX_DIM = 64
TOPK_MAX = 256
Q_BLOCK = 128
GM_WIDTH = 2048
GM_GROUPS = 4
GM_CHUNK = 128

LANES = 128
KEY_BLOCK = 512
NEG_BIG = -1e30
INT_MIN = -(2 ** 31)
VMEM_LIMIT = 56 * 1024 * 1024

ALIBI_SLOPES = [float(s) for s in
                (np.float32(2.0) ** (-8.0 * np.arange(1, AT_HEADS + 1, dtype=np.float32) / AT_HEADS))]


def _bdot(a, b):
    return jnp.dot(a.astype(BF16), b.astype(BF16), preferred_element_type=F32)


def _bdot_nt(a, b):
    return lax.dot_general(a.astype(BF16), b.astype(BF16), (((1,), (1,)), ((), ())),
                           preferred_element_type=F32)


def _bdot_tn(a, b):
    return lax.dot_general(a.astype(BF16), b.astype(BF16), (((0,), (0,)), ((), ())),
                           preferred_element_type=F32)


def _sigmoid(x):
    return 1.0 / (1.0 + jnp.exp(-x))


def _silu(x):
    return x * _sigmoid(x)


def _gelu_tanh(x):
    return x * (0.5 * (1.0 + jnp.tanh(np.sqrt(2.0 / np.pi).astype(np.float32) * (x + 0.044715 * (x * x * x)))))


def _softplus(x):
    return jnp.maximum(x, 0.0) + jnp.log1p(jnp.exp(-jnp.abs(x)))


def _layer_norm(x, g, b):
    mu = jnp.mean(x, axis=-1, keepdims=True)
    xc = x - mu
    var = jnp.mean(xc * xc, axis=-1, keepdims=True)
    return xc * lax.rsqrt(var + LN_EPS) * g + b


def _const_spec(shape):
    nd = len(shape)
    return pl.BlockSpec(shape, lambda *_: (0,) * nd)


def _params(n_axes):
    return pltpu.CompilerParams(dimension_semantics=("arbitrary",) * n_axes,
                                vmem_limit_bytes=VMEM_LIMIT)


def _tail_kernel(act_ref, x_ref, p_ref, wo_ref, bo_ref, g_ref, b_ref, wg_ref, wp_ref, o_ref):
    out = _bdot(act_ref[...], wo_ref[...]) + bo_ref[...]
    h = _layer_norm(ALPHA_DN * x_ref[...] + out, g_ref[...], b_ref[...])
    gate = _sigmoid(_bdot(h, wg_ref[...]))
    o_ref[...] = h + gate * _bdot(p_ref[...], wp_ref[...])


def _tail(act, x, p, wo, bo, g, b, wg, wp, tm):
    n, k = act.shape
    row = lambda c: pl.BlockSpec((tm, c), lambda i: (i, 0))
    return pl.pallas_call(
        _tail_kernel, grid=(n // tm,),
        in_specs=[row(k), row(D_MODEL), row(D_PLE), _const_spec(wo.shape), _const_spec(bo.shape),
                  _const_spec(g.shape), _const_spec(b.shape), _const_spec(wg.shape), _const_spec(wp.shape)],
        out_specs=row(D_MODEL), out_shape=jax.ShapeDtypeStruct((n, D_MODEL), F32),
        compiler_params=_params(1), name="layer_tail")(act, x, p, wo, bo, g, b, wg, wp)


def _cf_project(x2d, win_ref, bin_ref):
    xb = x2d.astype(BF16)
    d = D_MODEL
    a = jnp.dot(xb, win_ref[:, 0:d], preferred_element_type=F32) + bin_ref[:, 0:d]
    gl = jnp.dot(xb, win_ref[:, d:2 * d], preferred_element_type=F32) + bin_ref[:, d:2 * d]
    z = jnp.dot(xb, win_ref[:, 2 * d:3 * d], preferred_element_type=F32) + bin_ref[:, 2 * d:3 * d]
    return a * _sigmoid(gl), _silu(z)


def _cf_finish(cbuf, zbuf, g_ref, b_ref, store, n_rows, chunk):
    def body(r, carry):
        rs = pl.ds(pl.multiple_of(r * chunk, chunk), chunk)
        c = _silu(_layer_norm(cbuf[rs, :], g_ref[...], b_ref[...]))
        store(r, rs, (c * zbuf[rs, :]).astype(BF16))
        return carry
    lax.fori_loop(0, n_rows // chunk, body, 0)


def _cf_prompt_kernel(x_ref, win_ref, bin_ref, wdw_ref, bdw_ref, g_ref, b_ref, act_ref, st_ref,
                      ubuf, zbuf, cbuf, *, tm, tiles_per_seq):
    head = 32
    i = pl.program_id(0)
    first = (i % tiles_per_seq) == 0

    @pl.when(first)
    def _():
        ubuf[0:head, :] = jnp.zeros((head, D_MODEL), F32)

    @pl.when(jnp.logical_not(first))
    def _():
        ubuf[0:head, :] = ubuf[tm:tm + head, :]

    u, zs = _cf_project(x_ref[...], win_ref, bin_ref)
    ubuf[head:head + tm, :] = u
    zbuf[...] = zs
    off = head - CF_HIST
    rows, cols = 64, 256
    for rc in range(tm // rows):
        for cc in range(D_MODEL // cols):
            cs = slice(cc * cols, (cc + 1) * cols)
            acc = jnp.broadcast_to(bdw_ref[:, cs], (rows, cols))
            for j in range(CF_KERNEL):
                r0 = rc * rows + off + j
                acc = acc + wdw_ref[j:j + 1, cs] * ubuf[r0:r0 + rows, cs]
            cbuf[rc * rows:(rc + 1) * rows, cs] = acc
    def store(r, rs, val):
        act_ref[rs, :] = val
    _cf_finish(cbuf, zbuf, g_ref, b_ref, store, tm, 64)

    @pl.when((i % tiles_per_seq) == tiles_per_seq - 1)
    def _():
        st_ref[0] = ubuf[head + tm - CF_HIST:head + tm, :]


def _cf_prompt(x, w, batch, seq, tm):
    n = x.shape[0]
    tps = seq // tm
    row = lambda c: pl.BlockSpec((tm, c), lambda i: (i, 0))
    kern = functools.partial(_cf_prompt_kernel, tm=tm, tiles_per_seq=tps)
    return pl.pallas_call(
        kern, grid=(n // tm,),
        in_specs=[row(D_MODEL)] + [_const_spec(a.shape) for a in w],
        out_specs=[row(D_MODEL), pl.BlockSpec((1, CF_HIST, D_MODEL), lambda i: (i // tps, 0, 0))],
        out_shape=[jax.ShapeDtypeStruct((n, D_MODEL), BF16),
                   jax.ShapeDtypeStruct((batch, CF_HIST, D_MODEL), F32)],
        scratch_shapes=[pltpu.VMEM((tm + 32, D_MODEL), F32), pltpu.VMEM((tm, D_MODEL), F32),
                        pltpu.VMEM((tm, D_MODEL), F32)],
        compiler_params=_params(1), name="cf_prompt")(x, *w)


def _cf_sample_kernel(x_ref, hist_ref, win_ref, bin_ref, wdw_ref, bdw_ref, g_ref, b_ref, act_ref, st_ref,
                      ubuf, zbuf, cbuf, *, steps, bb):
    n = steps * bb
    u, zs = _cf_project(x_ref[...].reshape(n, D_MODEL), win_ref, bin_ref)
    zbuf[...] = zs
    ubuf[0:CF_HIST] = hist_ref[...]
    for t in range(steps):
        ubuf[CF_HIST + t] = u[t * bb:(t + 1) * bb, :]
    cols = 512
    for t in range(steps):
        for cc in range(D_MODEL // cols):
            cs = slice(cc * cols, (cc + 1) * cols)
            acc = jnp.broadcast_to(bdw_ref[:, cs], (bb, cols))
            for j in range(CF_KERNEL):
                acc = acc + wdw_ref[j:j + 1, cs] * ubuf[t + j, :, cs]
            cbuf[t * bb:(t + 1) * bb, cs] = acc
    def store(r, rs, val):
        act_ref[r] = val
    _cf_finish(cbuf, zbuf, g_ref, b_ref, store, n, bb)
    st_ref[...] = ubuf[steps:steps + CF_HIST]


def _cf_sample(x_tm, hist_tm, w, bb):
    steps, batch, _ = x_tm.shape
    kern = functools.partial(_cf_sample_kernel, steps=steps, bb=bb)
    blk = lambda t, c: pl.BlockSpec((t, bb, c), lambda i: (0, i, 0))
    return pl.pallas_call(
        kern, grid=(batch // bb,),
        in_specs=[blk(steps, D_MODEL), blk(CF_HIST, D_MODEL)] + [_const_spec(a.shape) for a in w],
        out_specs=[blk(steps, D_MODEL), blk(CF_HIST, D_MODEL)],
        out_shape=[jax.ShapeDtypeStruct((steps, batch, D_MODEL), BF16),
                   jax.ShapeDtypeStruct((CF_HIST, batch, D_MODEL), F32)],
        scratch_shapes=[pltpu.VMEM((CF_HIST + steps, bb, D_MODEL), F32),
                        pltpu.VMEM((steps * bb, D_MODEL), F32), pltpu.VMEM((steps * bb, D_MODEL), F32)],
        compiler_params=_params(1), name="cf_sample")(x_tm, hist_tm, *w)


def _mb_project(x2d, wz_ref, wxbc_ref, wdt_ref, dtb_ref, store_z, store_dt, store_xbc):
    xb = x2d.astype(BF16)
    store_z(jnp.dot(xb, wz_ref[...], preferred_element_type=F32))
    store_dt(_softplus(jnp.dot(xb, wdt_ref[...], preferred_element_type=F32) + dtb_ref[...]))
    cols = 1024
    for cc in range(MB_CONV_DIM // cols):
        cs = slice(cc * cols, (cc + 1) * cols)
        store_xbc(cs, jnp.dot(xb, wxbc_ref[:, cs], preferred_element_type=F32))


def _mb_in_prompt_kernel(x_ref, wz_ref, wxbc_ref, wdt_ref, dtb_ref, cw_ref, cb_ref,
                         z_ref, xbc_ref, dt_ref, st_ref, xbuf, *, tm, tiles_per_seq):
    head = 8
    i = pl.program_id(0)
    first = (i % tiles_per_seq) == 0

    @pl.when(first)
    def _():
        xbuf[0:head, :] = jnp.zeros((head, MB_CONV_DIM), F32)

    @pl.when(jnp.logical_not(first))
    def _():
        xbuf[0:head, :] = xbuf[tm:tm + head, :]

    def store_xbc(cs, val):
        xbuf[head:head + tm, cs] = val
    def store_z(val):
        z_ref[...] = val

    def store_dt(val):
        dt_ref[...] = val
    _mb_project(x_ref[...], wz_ref, wxbc_ref, wdt_ref, dtb_ref, store_z, store_dt, store_xbc)
    off = head - (MB_CONV - 1)
    rows, cols = 32, 512
    for rc in range(tm // rows):
        for cc in range(MB_CONV_DIM // cols):
            cs = slice(cc * cols, (cc + 1) * cols)
            acc = jnp.broadcast_to(cb_ref[:, cs], (rows, cols))
            for j in range(MB_CONV):
                r0 = rc * rows + off + j
                acc = acc + cw_ref[j:j + 1, cs] * xbuf[r0:r0 + rows, cs]
            xbc_ref[rc * rows:(rc + 1) * rows, cs] = _silu(acc)

    @pl.when((i % tiles_per_seq) == tiles_per_seq - 1)
    def _():
        st_ref[0] = xbuf[head + tm - (MB_CONV - 1):head + tm, :]


def _mb_in_prompt(x, w, batch, seq, tm):
    n = x.shape[0]
    tps = seq // tm
    row = lambda c: pl.BlockSpec((tm, c), lambda i: (i, 0))
    kern = functools.partial(_mb_in_prompt_kernel, tm=tm, tiles_per_seq=tps)
    return pl.pallas_call(
        kern, grid=(n // tm,),
        in_specs=[row(D_MODEL)] + [_const_spec(a.shape) for a in w],
        out_specs=[row(MB_INNER), row(MB_CONV_DIM), row(LANES),
                   pl.BlockSpec((1, MB_CONV - 1, MB_CONV_DIM), lambda i: (i // tps, 0, 0))],
        out_shape=[jax.ShapeDtypeStruct((n, MB_INNER), F32), jax.ShapeDtypeStruct((n, MB_CONV_DIM), F32),
                   jax.ShapeDtypeStruct((n, LANES), F32),
                   jax.ShapeDtypeStruct((batch, MB_CONV - 1, MB_CONV_DIM), F32)],
        scratch_shapes=[pltpu.VMEM((tm + 8, MB_CONV_DIM), F32)],
        compiler_params=_params(1), name="mb_in_prompt")(x, *w)


def _mb_in_sample_kernel(x_ref, hist_ref, wz_ref, wxbc_ref, wdt_ref, dtb_ref, cw_ref, cb_ref,
                         z_ref, xbc_ref, dt_ref, st_ref, xbuf, *, steps, bb):
    n = steps * bb
    hist = MB_CONV - 1
    xbuf[0:hist] = hist_ref[...]

    def store_xbc(cs, val):
        for t in range(steps):
            xbuf[hist + t, :, cs] = val[t * bb:(t + 1) * bb, :]
    def store_z(val):
        for t in range(steps):
            z_ref[t] = val[t * bb:(t + 1) * bb, :]

    def store_dt(val):
        for t in range(steps):
            dt_ref[t] = val[t * bb:(t + 1) * bb, :]
    _mb_project(x_ref[...].reshape(n, D_MODEL), wz_ref, wxbc_ref, wdt_ref, dtb_ref,
                store_z, store_dt, store_xbc)
    cols = 512
    for t in range(steps):
        for cc in range(MB_CONV_DIM // cols):
            cs = slice(cc * cols, (cc + 1) * cols)
            acc = jnp.broadcast_to(cb_ref[:, cs], (bb, cols))
            for j in range(MB_CONV):
                acc = acc + cw_ref[j:j + 1, cs] * xbuf[t + j, :, cs]
            xbc_ref[t, :, cs] = _silu(acc)
    st_ref[...] = xbuf[steps:steps + hist]


def _mb_in_sample(x_tm, hist_tm, w, bb):
    steps, batch, _ = x_tm.shape
    hist = MB_CONV - 1
    kern = functools.partial(_mb_in_sample_kernel, steps=steps, bb=bb)
    blk = lambda t, c: pl.BlockSpec((t, bb, c), lambda i: (0, i, 0))
    return pl.pallas_call(
        kern, grid=(batch // bb,),
        in_specs=[blk(steps, D_MODEL), blk(hist, MB_CONV_DIM)] + [_const_spec(a.shape) for a in w],
        out_specs=[blk(steps, MB_INNER), blk(steps, MB_CONV_DIM), blk(steps, LANES), blk(hist, MB_CONV_DIM)],
        out_shape=[jax.ShapeDtypeStruct((steps, batch, MB_INNER), F32),
                   jax.ShapeDtypeStruct((steps, batch, MB_CONV_DIM), F32),
                   jax.ShapeDtypeStruct((steps, batch, LANES), F32),
                   jax.ShapeDtypeStruct((hist, batch, MB_CONV_DIM), F32)],
        scratch_shapes=[pltpu.VMEM((hist + steps, bb, MB_CONV_DIM), F32)],
        compiler_params=_params(1), name="mb_in_sample")(x_tm, hist_tm, *w)


def _ssd_kernel(*refs, q, valid, n_chunks, has_s0):
    if has_s0:
        (xbc_x, xbc_b, xbc_c, dt_ref, z_ref, alog_ref, dskip_ref, ng_ref, s0_ref,
         act_ref, sfin_ref, s_ref, ybuf) = refs
    else:
        (xbc_x, xbc_b, xbc_c, dt_ref, z_ref, alog_ref, dskip_ref, ng_ref,
         act_ref, sfin_ref, s_ref, ybuf) = refs
    c = pl.program_id(1)

    @pl.when(c == 0)
    def _():
        if has_s0:
            s_ref[...] = s0_ref[0]
        else:
            s_ref[...] = jnp.zeros(s_ref.shape, F32)

    dt = dt_ref[...]
    if valid < q:
        dt = jnp.where(lax.broadcasted_iota(jnp.int32, (q, LANES), 0) < valid, dt, 0.0)
    a = dt * (-jnp.exp(alog_ref[...]))
    rid = lax.broadcasted_iota(jnp.int32, (q, q), 0)
    cid = lax.broadcasted_iota(jnp.int32, (q, q), 1)
    tri = rid >= cid
    lower = jnp.where(tri, 1.0, 0.0)
    upper = jnp.where(rid <= cid, 1.0, 0.0)
    eye = jnp.where(rid == cid, 1.0, 0.0)
    hi = lax.Precision.HIGHEST
    cum = jnp.dot(lower, a, precision=hi, preferred_element_type=F32)
    tn = (((0,), (0,)), ((), ()))
    cum_t = lax.dot_general(a, upper, tn, precision=hi, preferred_element_type=F32)
    dt_t = lax.dot_general(dt, eye, tn, precision=hi, preferred_element_type=F32)
    r_per_g = MB_HEADS // MB_GROUPS
    for g in range(MB_GROUPS):
        gs = slice(g * MB_STATE, (g + 1) * MB_STATE)
        cg = xbc_c[:, gs].astype(BF16)
        bg = xbc_b[:, gs].astype(BF16)
        cb = _bdot_nt(cg, bg)
        for r in range(r_per_g):
            h = g * r_per_g + r
            hs = slice(h * MB_HEAD_DIM, (h + 1) * MB_HEAD_DIM)
            ch = cum[:, h:h + 1]
            decay = jnp.exp(jnp.where(tri, ch - cum_t[h:h + 1, :], -jnp.inf))
            m = cb * decay * dt_t[h:h + 1, :]
            xh = xbc_x[:, hs]
            sh = s_ref[h]
            y = _bdot(m, xh) + _bdot_nt(cg, sh) * jnp.exp(ch) + dskip_ref[:, h:h + 1] * xh
            ybuf[:, hs] = y
            cl = cum[q - 1:q, h:h + 1]
            w_end = jnp.exp(cl - ch) * dt[:, h:h + 1]
            s_ref[h] = jnp.exp(cl) * sh + _bdot_tn(xh * w_end, bg)
    y = ybuf[...] * _silu(z_ref[...])
    y = y * lax.rsqrt(jnp.mean(y * y, axis=-1, keepdims=True) + LN_EPS) * ng_ref[...]
    act_ref[...] = y.astype(act_ref.dtype)

    @pl.when(c == n_chunks - 1)
    def _():
        sfin_ref[0] = s_ref[...]


def _ssd(xbc, dt, z, alog, dskip, ng, s0, batch, q, valid, n_chunks, act_dtype):
    n = xbc.shape[0]
    has_s0 = s0 is not None
    rowblk = lambda cols, cb: pl.BlockSpec((q, cols), lambda b, c: (b * n_chunks + c, cb))
    st_spec = pl.BlockSpec((1, MB_HEADS, MB_HEAD_DIM, MB_STATE), lambda b, c: (b, 0, 0, 0))
    in_specs = [rowblk(MB_INNER, 0), rowblk(MB_GROUPS * MB_STATE, 2), rowblk(MB_GROUPS * MB_STATE, 3),
                rowblk(LANES, 0), rowblk(MB_INNER, 0),
                _const_spec(alog.shape), _const_spec(dskip.shape), _const_spec(ng.shape)]
    args = [xbc, xbc, xbc, dt, z, alog, dskip, ng]
    if has_s0:
        in_specs.append(st_spec)
        args.append(s0)
    kern = functools.partial(_ssd_kernel, q=q, valid=valid, n_chunks=n_chunks, has_s0=has_s0)
    return pl.pallas_call(
        kern, grid=(batch, n_chunks), in_specs=in_specs,
        out_specs=[rowblk(MB_INNER, 0), st_spec],
        out_shape=[jax.ShapeDtypeStruct((n, MB_INNER), act_dtype),
                   jax.ShapeDtypeStruct((batch, MB_HEADS, MB_HEAD_DIM, MB_STATE), F32)],
        scratch_shapes=[pltpu.VMEM((MB_HEADS, MB_HEAD_DIM, MB_STATE), F32), pltpu.VMEM((q, MB_INNER), F32)],
        compiler_params=_params(2), name="ssd")(*args)


def _at_in_kernel(x_ref, wq_ref, wk_ref, wv_ref, wqi_ref, wki_ref, wwi_ref, wz_ref, kg_ref, kb_ref,
                  q_ref, k_ref, v_ref, qi_ref, ki_ref, wi_ref, z_ref):
    xb = x_ref[...].astype(BF16)
    dot = lambda w: jnp.dot(xb, w[...], preferred_element_type=F32)
    q_ref[...] = dot(wq_ref) * (AT_HEAD_DIM ** -0.5)
    k_ref[...] = dot(wk_ref)
    v_ref[...] = dot(wv_ref)
    qi_ref[...] = dot(wqi_ref) * (IDX_DIM ** -0.5)
    ki_ref[...] = _layer_norm(dot(wki_ref), kg_ref[...], kb_ref[...])
    wi_ref[...] = dot(wwi_ref) * (IDX_HEADS ** -0.5)
    z_ref[...] = _silu(dot(wz_ref))


def _at_in(x, w, tm):
    n = x.shape[0]
    row = lambda c: pl.BlockSpec((tm, c), lambda i: (i, 0))
    widths = [D_MODEL, AT_KV_DIM, AT_KV_DIM, IDX_HEADS * IDX_DIM, IDX_DIM, LANES, D_MODEL]
    return pl.pallas_call(
        _at_in_kernel, grid=(n // tm,),
        in_specs=[row(D_MODEL)] + [_const_spec(a.shape) for a in w],
        out_specs=[row(c) for c in widths],
        out_shape=[jax.ShapeDtypeStruct((n, c), F32) for c in widths],
        compiler_params=_params(1), name="at_in")(x, *w)


def _gather_kernel(pt_ref, k_ref, v_ref, ki_ref, ko_ref, vo_ref, kio_ref):
    ko_ref[...] = k_ref[...]
    vo_ref[...] = v_ref[...]
    kio_ref[...] = ki_ref[...]


def _gather_pages(page_table, cache_k, cache_v, cache_ki):
    batch, n_pages = page_table.shape
    page = cache_k.shape[1]
    src = lambda c: pl.BlockSpec((1, page, c), lambda b, p, pt: (pt[b * n_pages + p], 0, 0))
    dst = lambda c: pl.BlockSpec((1, page, c), lambda b, p, pt: (b, p, 0))
    shp = lambda c: jax.ShapeDtypeStruct((batch, n_pages * page, c), F32)
    return pl.pallas_call(
        _gather_kernel,
        grid_spec=pltpu.PrefetchScalarGridSpec(
            num_scalar_prefetch=1, grid=(batch, n_pages),
            in_specs=[src(AT_KV_DIM), src(AT_KV_DIM), src(IDX_DIM)],
            out_specs=[dst(AT_KV_DIM), dst(AT_KV_DIM), dst(IDX_DIM)]),
        out_shape=[shp(AT_KV_DIM), shp(AT_KV_DIM), shp(IDX_DIM)],
        compiler_params=_params(2), name="gather_pages")(page_table.reshape(-1), cache_k, cache_v, cache_ki)


def _dsa_kernel(q_ref, qi_ref, wi_ref, z_ref, k_ref, v_ref, ki_ref, act_ref,
                keys_ref, acc_ref, m_ref, *, t, pos0_static, topk):
    kb_sz = KEY_BLOCK
    if pos0_static is None:
        pos0 = pl.program_id(1) * t
    else:
        pos0 = pos0_static
    n_kb = (pos0 + t - 1) // kb_sz + 1
    rowpos = pos0 + lax.broadcasted_iota(jnp.int32, (t, 1), 0)
    lane = lax.broadcasted_iota(jnp.int32, (1, kb_sz), 1)

    def score_block(kb, carry):
        ks = pl.ds(pl.multiple_of(kb * kb_sz, kb_sz), kb_sz)
        kib = ki_ref[ks, :].astype(BF16)
        acc = jnp.zeros((t, kb_sz), F32)
        for i in range(IDX_HEADS):
            sc = _bdot_nt(qi_ref[:, i * IDX_DIM:(i + 1) * IDX_DIM], kib)
            acc = acc + jnp.maximum(sc, 0.0) * wi_ref[:, i:i + 1]
        acc = jnp.where(kb * kb_sz + lane <= rowpos, acc, -jnp.inf)
        bits = pltpu.bitcast(acc, jnp.int32)
        keys_ref[kb] = jnp.where(bits < 0, bits ^ jnp.int32(0x7FFFFFFF), bits)
        return carry
    lax.fori_loop(0, n_kb, score_block, 0)

    def count(pred):
        def body(kb, cnt):
            hit = jnp.where(pred(keys_ref[kb]), 1.0, 0.0)
            for s in range(kb_sz // LANES):
                cnt = cnt + hit[:, s * LANES:(s + 1) * LANES]
            return cnt
        cnt = lax.fori_loop(0, n_kb, body, jnp.zeros((t, LANES), F32))
        return jnp.sum(cnt, axis=1, keepdims=True)

    thr = jnp.full((t, 1), INT_MIN, jnp.int32)
    for bit in range(31, -1, -1):
        cand = thr + jnp.int32(INT_MIN if bit == 31 else (1 << bit))
        thr = jnp.where(count(lambda key: key >= cand) >= float(topk), cand, thr)
    need_eq = float(topk) - count(lambda key: key > thr)

    acc_ref[...] = jnp.zeros(acc_ref.shape, F32)
    m_ref[...] = jnp.full(m_ref.shape, NEG_BIG, F32)
    incl = jnp.where(lax.broadcasted_iota(jnp.int32, (kb_sz, kb_sz), 0)
                     <= lax.broadcasted_iota(jnp.int32, (kb_sz, kb_sz), 1), 1.0, 0.0).astype(BF16)
    ones_cols = jnp.ones((kb_sz, LANES - AT_HEAD_DIM), BF16)
    r_per_g = AT_HEADS // AT_KV_HEADS

    def attend_block(kb, run_eq):
        ks = pl.ds(pl.multiple_of(kb * kb_sz, kb_sz), kb_sz)
        key = keys_ref[kb]
        eq = key == thr
        pre = jnp.dot(jnp.where(eq, 1.0, 0.0).astype(BF16), incl, preferred_element_type=F32) + run_eq
        col = kb * kb_sz + lane
        sel = ((key > thr) | (eq & (pre <= need_eq))) & (col <= rowpos)
        dist = (rowpos - col).astype(F32)
        mask_bias = jnp.where(sel, 0.0, NEG_BIG)
        kblk = k_ref[ks, :].astype(BF16)
        vblk = v_ref[ks, :].astype(BF16)
        for g in range(AT_KV_HEADS):
            gsl = slice(g * AT_HEAD_DIM, (g + 1) * AT_HEAD_DIM)
            kg = kblk[:, gsl]
            vg = jnp.concatenate([vblk[:, gsl], ones_cols], axis=1)
            for r in range(r_per_g):
                h = g * r_per_g + r
                qh = q_ref[:, h * AT_HEAD_DIM:(h + 1) * AT_HEAD_DIM]
                s = _bdot_nt(qh, kg) + (mask_bias - ALIBI_SLOPES[h] * dist)
                m_old = m_ref[h]
                m_new = jnp.maximum(m_old, jnp.max(s, axis=1, keepdims=True))
                p = jnp.exp(s - m_new)
                acc_ref[h] = jnp.exp(m_old - m_new) * acc_ref[h] + _bdot(p, vg)
                m_ref[h] = m_new
        return pre[:, kb_sz - 1:kb_sz]
    lax.fori_loop(0, n_kb, attend_block, jnp.zeros((t, 1), F32))

    for h in range(AT_HEADS):
        hs = slice(h * AT_HEAD_DIM, (h + 1) * AT_HEAD_DIM)
        a = acc_ref[h]
        o = a[:, 0:AT_HEAD_DIM] / a[:, AT_HEAD_DIM:AT_HEAD_DIM + 1]
        act_ref[:, hs] = (o * z_ref[:, hs]).astype(act_ref.dtype)


def _dsa(q, qi, wi, z, k_all, v_all, ki_all, batch, n_qb, t, s_len, pos0_static, topk, act_dtype):
    n = q.shape[0]
    qrow = lambda c: pl.BlockSpec((t, c), lambda b, j: (b * n_qb + j, 0))
    krow = lambda c: pl.BlockSpec((s_len, c), lambda b, j: (b, 0))
    kern = functools.partial(_dsa_kernel, t=t, pos0_static=pos0_static, topk=topk)
    return pl.pallas_call(
        kern, grid=(batch, n_qb),
        in_specs=[qrow(D_MODEL), qrow(IDX_HEADS * IDX_DIM), qrow(LANES), qrow(D_MODEL),
                  krow(AT_KV_DIM), krow(AT_KV_DIM), krow(IDX_DIM)],
        out_specs=qrow(D_MODEL), out_shape=jax.ShapeDtypeStruct((n, D_MODEL), act_dtype),
        scratch_shapes=[pltpu.VMEM((s_len // KEY_BLOCK, t, KEY_BLOCK), jnp.int32),
                        pltpu.VMEM((AT_HEADS, t, LANES), F32),
                        pltpu.VMEM((AT_HEADS, t, 1), F32)],
        compiler_params=_params(2), name="dsa")(q, qi, wi, z, k_all, v_all, ki_all)


def _gm_kernel(*refs, tm, chunk, emit_v):
    if emit_v:
        (x_ref, win_ref, bin_ref, g_ref, b_ref, wmix_ref, bmix_ref, act_ref, v_ref, ubuf, vbuf, zbuf) = refs
    else:
        (x_ref, win_ref, bin_ref, g_ref, b_ref, wmix_ref, bmix_ref, act_ref, ubuf, vbuf, zbuf) = refs
    xb = x_ref[...].astype(BF16)
    w = GM_WIDTH
    cols = 512
    for cc in range(w // cols):
        cs = slice(cc * cols, (cc + 1) * cols)
        proj = lambda off: (jnp.dot(xb, win_ref[:, off + cc * cols:off + (cc + 1) * cols],
                                    preferred_element_type=F32) + bin_ref[:, off + cc * cols:off + (cc + 1) * cols])
        ubuf[:, cs] = _gelu_tanh(proj(0))
        vbuf[:, cs] = _gelu_tanh(proj(w))
        zbuf[:, cs] = _silu(proj(2 * w))

    rows = 32

    def norm_rows(r, carry):
        rs = pl.ds(pl.multiple_of(r * rows, rows), rows)
        v = _layer_norm(vbuf[rs, :], g_ref[...], b_ref[...])
        vbuf[rs, :] = v
        if emit_v:
            v_ref[rs, :] = v
        return carry
    lax.fori_loop(0, tm // rows, norm_rows, 0)

    gw = w // GM_GROUPS
    for c in range(tm // chunk):
        rs = slice(c * chunk, (c + 1) * chunk)
        for g in range(GM_GROUPS):
            cs = slice(g * gw, (g + 1) * gw)
            s = jnp.dot(wmix_ref[g], vbuf[rs, cs].astype(BF16), preferred_element_type=F32) + bmix_ref[:, g:g + 1]
            act_ref[rs, cs] = (ubuf[rs, cs] * s * zbuf[rs, cs]).astype(act_ref.dtype)


def _gm(x, w, tm, chunk, emit_v):
    n = x.shape[0]
    row = lambda c: pl.BlockSpec((tm, c), lambda i: (i, 0))
    out_specs = [row(GM_WIDTH)]
    out_shape = [jax.ShapeDtypeStruct((n, GM_WIDTH), BF16)]
    if emit_v:
        out_specs.append(row(GM_WIDTH))
        out_shape.append(jax.ShapeDtypeStruct((n, GM_WIDTH), F32))
    kern = functools.partial(_gm_kernel, tm=tm, chunk=chunk, emit_v=emit_v)
    return pl.pallas_call(
        kern, grid=(n // tm,),
        in_specs=[row(D_MODEL)] + [_const_spec(a.shape) for a in w],
        out_specs=out_specs, out_shape=out_shape,
        scratch_shapes=[pltpu.VMEM((tm, GM_WIDTH), F32)] * 3,
        compiler_params=_params(1), name="gmlp")(x, *w)


def _pad_lanes(a, width=LANES):
    return jnp.pad(a, [(0, 0)] * (a.ndim - 1) + [(0, width - a.shape[-1])])


def kernel(x_prompt, x_sample, p_prompt, p_sample, state_cf_conv, state_mb_conv, state_mb_ssm,
           cache_k, cache_v, cache_kidx, page_table, post_ln_g, post_ln_b, ple_w, ple_gate_w,
           cf_w_in, cf_b_in, cf_w_dw, cf_b_dw, cf_ln_g, cf_ln_b, cf_w_out, cf_b_out,
           mb_w_in, mb_w_conv, mb_b_conv, mb_dt_bias, mb_a_log, mb_d_skip, mb_norm_g, mb_w_out,
           at_w_in, at_ki_ln_g, at_ki_ln_b, at_w_out,
           gm_w_in, gm_b_in, gm_ln_g, gm_ln_b, gm_w_s, gm_b_s, gm_w_out):
    bp, lp, d = x_prompt.shape
    bs, ls, _ = x_sample.shape
    row2 = lambda v: v.reshape(1, -1)
    bf = lambda v: v.astype(BF16)

    ple_wb, ple_gate_wb = bf(ple_w), bf(ple_gate_w)
    cf_w = (bf(cf_w_in), row2(cf_b_in), cf_w_dw, row2(cf_b_dw), row2(cf_ln_g), row2(cf_ln_b))
    mb_in_w = (bf(mb_w_in[:, :MB_INNER]), bf(mb_w_in[:, MB_INNER:MB_INNER + MB_CONV_DIM]),
               bf(_pad_lanes(mb_w_in[:, MB_INNER + MB_CONV_DIM:])), _pad_lanes(row2(mb_dt_bias)),
               mb_w_conv, row2(mb_b_conv))
    ssd_w = (_pad_lanes(row2(mb_a_log)), _pad_lanes(row2(mb_d_skip)), row2(mb_norm_g))
    sizes = [AT_HEADS * AT_HEAD_DIM, AT_KV_DIM, AT_KV_DIM, IDX_HEADS * IDX_DIM, IDX_DIM, IDX_HEADS,
             AT_HEADS * AT_HEAD_DIM]
    offs = np.concatenate([[0], np.cumsum(sizes)]).tolist()
    at_parts = [at_w_in[:, offs[i]:offs[i + 1]] for i in range(len(sizes))]
    at_parts[5] = _pad_lanes(at_parts[5])
    at_w = tuple(bf(a) for a in at_parts) + (row2(at_ki_ln_g), row2(at_ki_ln_b))
    gm_common = (bf(gm_w_in), row2(gm_b_in), row2(gm_ln_g), row2(gm_ln_b))
    zeros_d = jnp.zeros((1, d), F32)
    out_w = [(bf(cf_w_out), row2(cf_b_out)), (bf(mb_w_out), zeros_d), (bf(at_w_out), zeros_d),
             (bf(gm_w_out), zeros_d)]

    def tail(i, act, x, p, tm):
        wo, bo = out_w[i]
        return _tail(act, x, p, wo, bo, row2(post_ln_g[i]), row2(post_ln_b[i]), ple_gate_wb[i], ple_wb[i], tm)

    tm = 512
    n_p = bp * lp
    x = x_prompt.reshape(n_p, d)
    pp = p_prompt.reshape(DEPTH, n_p, D_PLE)

    act, cf_conv_p = _cf_prompt(x, cf_w, bp, lp, tm)
    x = tail(0, act, x, pp[0], tm)

    z, xbc, dt, mb_conv_p = _mb_in_prompt(x, mb_in_w, bp, lp, tm)
    act, mb_ssm_p = _ssd(xbc, dt, z, *ssd_w, None, bp, MB_CHUNK, MB_CHUNK, lp // MB_CHUNK, BF16)
    x = tail(1, act, x, pp[1], tm)

    q, k, v, qi, ki, wi, zs = _at_in(x, at_w, tm)
    act = _dsa(q, qi, wi, zs, k, v, ki, bp, lp // Q_BLOCK, Q_BLOCK, lp, None, min(TOPK_MAX, lp // 4), BF16)
    x = tail(2, act, x, pp[2], tm)
    k_p = k.reshape(bp, lp, AT_KV_HEADS, AT_HEAD_DIM)
    v_p = v.reshape(bp, lp, AT_KV_HEADS, AT_HEAD_DIM)
    kidx_p = ki.reshape(bp, lp, IDX_DIM)

    gm_w_p = gm_common + (bf(jnp.tril(gm_w_s)), gm_b_s.T)
    (act,) = _gm(x, gm_w_p, tm, GM_CHUNK, False)
    y_prompt = tail(3, act, x, pp[3], tm).reshape(bp, lp, d)

    n_s = bs * ls
    bb = 32
    to_tm = lambda a: jnp.swapaxes(a, 0, 1)
    x_tm = to_tm(x_sample)
    x = x_tm.reshape(n_s, d)
    ps = jnp.swapaxes(p_sample, 1, 2).reshape(DEPTH, n_s, D_PLE)
    tm_s = n_s

    act, cf_st = _cf_sample(x_tm, to_tm(state_cf_conv), cf_w, bb)
    cf_conv_s = to_tm(cf_st)
    x = tail(0, act.reshape(n_s, d), x, ps[0], tm_s)

    z, xbc, dt, mb_st = _mb_in_sample(x.reshape(ls, bs, d), to_tm(state_mb_conv), mb_in_w, bb)
    mb_conv_s = to_tm(mb_st)
    qs = 8

    def to_bm(a):
        a = jnp.pad(to_tm(a), ((0, 0), (0, qs - ls), (0, 0)))
        return a.reshape(bs * qs, a.shape[-1])

    def from_bm(a):
        a = a.reshape(bs, qs, a.shape[-1])[:, :ls]
        return to_tm(a).reshape(n_s, a.shape[-1])

    act, mb_ssm_s = _ssd(to_bm(xbc), to_bm(dt), to_bm(z), *ssd_w, state_mb_ssm, bs, qs, ls, 1, F32)
    x = tail(1, from_bm(act), x, ps[1], tm_s)

    q, k, v, qi, ki, wi, zs = _at_in(x, at_w, tm_s)
    k_s = to_tm(k.reshape(ls, bs, AT_KV_HEADS, AT_HEAD_DIM))
    v_s = to_tm(v.reshape(ls, bs, AT_KV_HEADS, AT_HEAD_DIM))
    kidx_s = to_tm(ki.reshape(ls, bs, IDX_DIM))
    n_pool, page = cache_k.shape[:2]
    kg, vg, kig = _gather_pages(page_table, cache_k.reshape(n_pool, page, AT_KV_DIM),
                                cache_v.reshape(n_pool, page, AT_KV_DIM), cache_kidx)
    past = kg.shape[1]
    s_len = -(-(past + qs) // KEY_BLOCK) * KEY_BLOCK

    def with_new(old, new):
        new = to_tm(new.reshape(ls, bs, -1))
        a = jnp.concatenate([old, new], axis=1)
        a = jnp.pad(a, ((0, 0), (0, s_len - a.shape[1]), (0, 0)))
        return a.reshape(bs * s_len, a.shape[-1])

    to_bm2 = lambda a: to_bm(a.reshape(ls, bs, a.shape[-1]))
    act = _dsa(to_bm2(q), to_bm2(qi), to_bm2(wi), to_bm2(zs), with_new(kg, k), with_new(vg, v), with_new(kig, ki),
               bs, 1, qs, s_len, past, min(TOPK_MAX, (past + ls) // 4), F32)
    x = tail(2, from_bm(act), x, ps[2], tm_s)

    mix = jnp.stack([jnp.kron(jnp.tril(gm_w_s[g, :ls, :ls]), jnp.eye(bs, dtype=F32)) for g in range(GM_GROUPS)])
    gm_w_smp = gm_common + (bf(mix), jnp.repeat(gm_b_s[:, :ls].T, bs, axis=0))
    act, gm_v = _gm(x, gm_w_smp, tm_s, n_s, True)
    y_s = tail(3, act, x, ps[3], tm_s)
    y_sample = to_tm(y_s.reshape(ls, bs, d))
    gm_v_s = to_tm(gm_v.reshape(ls, bs, GM_WIDTH))

    return (y_prompt, y_sample, cf_conv_p, cf_conv_s, mb_conv_p, mb_conv_s, mb_ssm_p, mb_ssm_s,
            k_p, v_p, kidx_p, k_s, v_s, kidx_s, gm_v_s)
```

```python
import functools

import numpy as np
import jax
import jax.numpy as jnp
from jax import lax
from jax.experimental import pallas as pl
from jax.experimental.pallas import tpu as pltpu

F32 = jnp.float32
BF16 = jnp.bfloat16

D_MODEL = 1024
D_PLE = 256
DEPTH = 4
ALPHA_DN = (2 * DEPTH) ** 0.25
LN_EPS = 1e-5

CF_KERNEL = 31
CF_HIST = CF_KERNEL - 1
MB_INNER = 2048
MB_HEAD_DIM = 64
MB_HEADS = 32
MB_GROUPS = 8
MB_STATE = 128
MB_CONV = 4
MB_CONV_DIM = 4096
MB_CHUNK = 128
AT_HEADS = 16
AT_HEAD_DIM = 64
AT_KV_HEADS = 4
AT_KV_DIM = AT_KV_HEADS * AT_HEAD_DIM
IDX_HEADS = 8
IDX_DIM = 64
TOPK_MAX = 256
Q_BLOCK = 128
GM_WIDTH = 2048
GM_GROUPS = 4
GM_CHUNK = 128

LANES = 128
KEY_BLOCK = 512
NEG_BIG = -1e30
INT_MIN = -(2 ** 31)
VMEM_LIMIT = 56 * 1024 * 1024

ALIBI_SLOPES = [float(s) for s in
                (np.float32(2.0) ** (-8.0 * np.arange(1, AT_HEADS + 1, dtype=np.float32) / AT_HEADS))]


def _bdot(a, b):
    return jnp.dot(a.astype(BF16), b.astype(BF16), preferred_element_type=F32)


def _bdot_nt(a, b):
    return lax.dot_general(a.astype(BF16), b.astype(BF16), (((1,), (1,)), ((), ())),
                           preferred_element_type=F32)


def _bdot_tn(a, b):
    return lax.dot_general(a.astype(BF16), b.astype(BF16), (((0,), (0,)), ((), ())),
                           preferred_element_type=F32)


def _sigmoid(x):
    return 1.0 / (1.0 + jnp.exp(-x))


def _silu(x):
    return x * _sigmoid(x)


def _gelu_tanh(x):
    return x * (0.5 * (1.0 + jnp.tanh(np.sqrt(2.0 / np.pi).astype(np.float32) * (x + 0.044715 * (x * x * x)))))


def _softplus(x):
    return jnp.maximum(x, 0.0) + jnp.log1p(jnp.exp(-jnp.abs(x)))


def _layer_norm(x, g, b):
    mu = jnp.mean(x, axis=-1, keepdims=True)
    xc = x - mu
    var = jnp.mean(xc * xc, axis=-1, keepdims=True)
    return xc * lax.rsqrt(var + LN_EPS) * g + b


def _const_spec(shape):
    nd = len(shape)
    return pl.BlockSpec(shape, lambda *_: (0,) * nd)


def _params(n_axes):
    return pltpu.CompilerParams(dimension_semantics=("arbitrary",) * n_axes,
                                vmem_limit_bytes=VMEM_LIMIT)


def _tail_kernel(act_ref, x_ref, p_ref, wo_ref, bo_ref, g_ref, b_ref, wg_ref, wp_ref, o_ref):
    out = _bdot(act_ref[...], wo_ref[...]) + bo_ref[...]
    h = _layer_norm(ALPHA_DN * x_ref[...] + out, g_ref[...], b_ref[...])
    gate = _sigmoid(_bdot(h, wg_ref[...]))
    o_ref[...] = h + gate * _bdot(p_ref[...], wp_ref[...])


def _tail(act, x, p, wo, bo, g, b, wg, wp, tm):
    n, k = act.shape
    row = lambda c: pl.BlockSpec((tm, c), lambda i: (i, 0))
    return pl.pallas_call(
        _tail_kernel, grid=(n // tm,),
        in_specs=[row(k), row(D_MODEL), row(D_PLE), _const_spec(wo.shape), _const_spec(bo.shape),
                  _const_spec(g.shape), _const_spec(b.shape), _const_spec(wg.shape), _const_spec(wp.shape)],
        out_specs=row(D_MODEL), out_shape=jax.ShapeDtypeStruct((n, D_MODEL), F32),
        compiler_params=_params(1), name="layer_tail")(act, x, p, wo, bo, g, b, wg, wp)


def _cf_project(x2d, win_ref, bin_ref):
    xb = x2d.astype(BF16)
    d = D_MODEL
    a = jnp.dot(xb, win_ref[:, 0:d], preferred_element_type=F32) + bin_ref[:, 0:d]
    gl = jnp.dot(xb, win_ref[:, d:2 * d], preferred_element_type=F32) + bin_ref[:, d:2 * d]
    z = jnp.dot(xb, win_ref[:, 2 * d:3 * d], preferred_element_type=F32) + bin_ref[:, 2 * d:3 * d]
    return a * _sigmoid(gl), _silu(z)


def _cf_finish(cbuf, zbuf, g_ref, b_ref, store, n_rows, chunk):
    def body(r, carry):
        rs = pl.ds(pl.multiple_of(r * chunk, chunk), chunk)
        c = _silu(_layer_norm(cbuf[rs, :], g_ref[...], b_ref[...]))
        store(r, rs, (c * zbuf[rs, :]).astype(BF16))
        return carry
    lax.fori_loop(0, n_rows // chunk, body, 0)


def _cf_prompt_kernel(x_ref, win_ref, bin_ref, wdw_ref, bdw_ref, g_ref, b_ref, act_ref, st_ref,
                      ubuf, zbuf, cbuf, *, tm, tiles_per_seq):
    head = 32
    i = pl.program_id(0)
    first = (i % tiles_per_seq) == 0

    @pl.when(first)
    def _():
        ubuf[0:head, :] = jnp.zeros((head, D_MODEL), F32)

    @pl.when(jnp.logical_not(first))
    def _():
        ubuf[0:head, :] = ubuf[tm:tm + head, :]

    u, zs = _cf_project(x_ref[...], win_ref, bin_ref)
    ubuf[head:head + tm, :] = u
    zbuf[...] = zs
    off = head - CF_HIST
    rows, cols = 64, 256
    for rc in range(tm // rows):
        for cc in range(D_MODEL // cols):
            cs = slice(cc * cols, (cc + 1) * cols)
            acc = jnp.broadcast_to(bdw_ref[:, cs], (rows, cols))
            for j in range(CF_KERNEL):
                r0 = rc * rows + off + j
                acc = acc + wdw_ref[j:j + 1, cs] * ubuf[r0:r0 + rows, cs]
            cbuf[rc * rows:(rc + 1) * rows, cs] = acc
    def store(r, rs, val):
        act_ref[rs, :] = val
    _cf_finish(cbuf, zbuf, g_ref, b_ref, store, tm, 64)

    @pl.when((i % tiles_per_seq) == tiles_per_seq - 1)
    def _():
        st_ref[0] = ubuf[head + tm - CF_HIST:head + tm, :]


def _cf_prompt(x, w, batch, seq, tm):
    n = x.shape[0]
    tps = seq // tm
    row = lambda c: pl.BlockSpec((tm, c), lambda i: (i, 0))
    kern = functools.partial(_cf_prompt_kernel, tm=tm, tiles_per_seq=tps)
    return pl.pallas_call(
        kern, grid=(n // tm,),
        in_specs=[row(D_MODEL)] + [_const_spec(a.shape) for a in w],
        out_specs=[row(D_MODEL), pl.BlockSpec((1, CF_HIST, D_MODEL), lambda i: (i // tps, 0, 0))],
        out_shape=[jax.ShapeDtypeStruct((n, D_MODEL), BF16),
                   jax.ShapeDtypeStruct((batch, CF_HIST, D_MODEL), F32)],
        scratch_shapes=[pltpu.VMEM((tm + 32, D_MODEL), F32), pltpu.VMEM((tm, D_MODEL), F32),
                        pltpu.VMEM((tm, D_MODEL), F32)],
        compiler_params=_params(1), name="cf_prompt")(x, *w)


def _cf_sample_kernel(x_ref, hist_ref, win_ref, bin_ref, wdw_ref, bdw_ref, g_ref, b_ref, act_ref, st_ref,
                      ubuf, zbuf, cbuf, *, steps, bb):
    n = steps * bb
    u, zs = _cf_project(x_ref[...].reshape(n, D_MODEL), win_ref, bin_ref)
    zbuf[...] = zs
    ubuf[0:CF_HIST] = hist_ref[...]
    for t in range(steps):
        ubuf[CF_HIST + t] = u[t * bb:(t + 1) * bb, :]
    cols = 512
    for t in range(steps):
        for cc in range(D_MODEL // cols):
            cs = slice(cc * cols, (cc + 1) * cols)
            acc = jnp.broadcast_to(bdw_ref[:, cs], (bb, cols))
            for j in range(CF_KERNEL):
                acc = acc + wdw_ref[j:j + 1, cs] * ubuf[t + j, :, cs]
            cbuf[t * bb:(t + 1) * bb, cs] = acc
    def store(r, rs, val):
        act_ref[r] = val
    _cf_finish(cbuf, zbuf, g_ref, b_ref, store, n, bb)
    st_ref[...] = ubuf[steps:steps + CF_HIST]


def _cf_sample(x_tm, hist_tm, w, bb):
    steps, batch, _ = x_tm.shape
    kern = functools.partial(_cf_sample_kernel, steps=steps, bb=bb)
    blk = lambda t, c: pl.BlockSpec((t, bb, c), lambda i: (0, i, 0))
    return pl.pallas_call(
        kern, grid=(batch // bb,),
        in_specs=[blk(steps, D_MODEL), blk(CF_HIST, D_MODEL)] + [_const_spec(a.shape) for a in w],
        out_specs=[blk(steps, D_MODEL), blk(CF_HIST, D_MODEL)],
        out_shape=[jax.ShapeDtypeStruct((steps, batch, D_MODEL), BF16),
                   jax.ShapeDtypeStruct((CF_HIST, batch, D_MODEL), F32)],
        scratch_shapes=[pltpu.VMEM((CF_HIST + steps, bb, D_MODEL), F32),
                        pltpu.VMEM((steps * bb, D_MODEL), F32), pltpu.VMEM((steps * bb, D_MODEL), F32)],
        compiler_params=_params(1), name="cf_sample")(x_tm, hist_tm, *w)


def _mb_project(x2d, wz_ref, wxbc_ref, wdt_ref, dtb_ref, store_z, store_dt, store_xbc):
    xb = x2d.astype(BF16)
    store_z(jnp.dot(xb, wz_ref[...], preferred_element_type=F32))
    store_dt(_softplus(jnp.dot(xb, wdt_ref[...], preferred_element_type=F32) + dtb_ref[...]))
    cols = 1024
    for cc in range(MB_CONV_DIM // cols):
        cs = slice(cc * cols, (cc + 1) * cols)
        store_xbc(cs, jnp.dot(xb, wxbc_ref[:, cs], preferred_element_type=F32))


def _mb_in_prompt_kernel(x_ref, wz_ref, wxbc_ref, wdt_ref, dtb_ref, cw_ref, cb_ref,
                         z_ref, xbc_ref, dt_ref, st_ref, xbuf, *, tm, tiles_per_seq):
    head = 8
    i = pl.program_id(0)
    first = (i % tiles_per_seq) == 0

    @pl.when(first)
    def _():
        xbuf[0:head, :] = jnp.zeros((head, MB_CONV_DIM), F32)

    @pl.when(jnp.logical_not(first))
    def _():
        xbuf[0:head, :] = xbuf[tm:tm + head, :]

    def store_xbc(cs, val):
        xbuf[head:head + tm, cs] = val
    def store_z(val):
        z_ref[...] = val

    def store_dt(val):
        dt_ref[...] = val
    _mb_project(x_ref[...], wz_ref, wxbc_ref, wdt_ref, dtb_ref, store_z, store_dt, store_xbc)
    off = head - (MB_CONV - 1)
    rows, cols = 32, 512
    for rc in range(tm // rows):
        for cc in range(MB_CONV_DIM // cols):
            cs = slice(cc * cols, (cc + 1) * cols)
            acc = jnp.broadcast_to(cb_ref[:, cs], (rows, cols))
            for j in range(MB_CONV):
                r0 = rc * rows + off + j
                acc = acc + cw_ref[j:j + 1, cs] * xbuf[r0:r0 + rows, cs]
            xbc_ref[rc * rows:(rc + 1) * rows, cs] = _silu(acc)

    @pl.when((i % tiles_per_seq) == tiles_per_seq - 1)
    def _():
        st_ref[0] = xbuf[head + tm - (MB_CONV - 1):head + tm, :]


def _mb_in_prompt(x, w, batch, seq, tm):
    n = x.shape[0]
    tps = seq // tm
    row = lambda c: pl.BlockSpec((tm, c), lambda i: (i, 0))
    kern = functools.partial(_mb_in_prompt_kernel, tm=tm, tiles_per_seq=tps)
    return pl.pallas_call(
        kern, grid=(n // tm,),
        in_specs=[row(D_MODEL)] + [_const_spec(a.shape) for a in w],
        out_specs=[row(MB_INNER), row(MB_CONV_DIM), row(LANES),
                   pl.BlockSpec((1, MB_CONV - 1, MB_CONV_DIM), lambda i: (i // tps, 0, 0))],
        out_shape=[jax.ShapeDtypeStruct((n, MB_INNER), F32), jax.ShapeDtypeStruct((n, MB_CONV_DIM), F32),
                   jax.ShapeDtypeStruct((n, LANES), F32),
                   jax.ShapeDtypeStruct((batch, MB_CONV - 1, MB_CONV_DIM), F32)],
        scratch_shapes=[pltpu.VMEM((tm + 8, MB_CONV_DIM), F32)],
        compiler_params=_params(1), name="mb_in_prompt")(x, *w)


def _mb_in_sample_kernel(x_ref, hist_ref, wz_ref, wxbc_ref, wdt_ref, dtb_ref, cw_ref, cb_ref,
                         z_ref, xbc_ref, dt_ref, st_ref, xbuf, *, steps, bb):
    n = steps * bb
    hist = MB_CONV - 1
    xbuf[0:hist] = hist_ref[...]

    def store_xbc(cs, val):
        for t in range(steps):
            xbuf[hist + t, :, cs] = val[t * bb:(t + 1) * bb, :]
    def store_z(val):
        for t in range(steps):
            z_ref[t] = val[t * bb:(t + 1) * bb, :]

    def store_dt(val):
        for t in range(steps):
            dt_ref[t] = val[t * bb:(t + 1) * bb, :]
    _mb_project(x_ref[...].reshape(n, D_MODEL), wz_ref, wxbc_ref, wdt_ref, dtb_ref,
                store_z, store_dt, store_xbc)
    cols = 512
    for t in range(steps):
        for cc in range(MB_CONV_DIM // cols):
            cs = slice(cc * cols, (cc + 1) * cols)
            acc = jnp.broadcast_to(cb_ref[:, cs], (bb, cols))
            for j in range(MB_CONV):
                acc = acc + cw_ref[j:j + 1, cs] * xbuf[t + j, :, cs]
            xbc_ref[t, :, cs] = _silu(acc)
    st_ref[...] = xbuf[steps:steps + hist]


def _mb_in_sample(x_tm, hist_tm, w, bb):
    steps, batch, _ = x_tm.shape
    hist = MB_CONV - 1
    kern = functools.partial(_mb_in_sample_kernel, steps=steps, bb=bb)
    blk = lambda t, c: pl.BlockSpec((t, bb, c), lambda i: (0, i, 0))
    return pl.pallas_call(
        kern, grid=(batch // bb,),
        in_specs=[blk(steps, D_MODEL), blk(hist, MB_CONV_DIM)] + [_const_spec(a.shape) for a in w],
        out_specs=[blk(steps, MB_INNER), blk(steps, MB_CONV_DIM), blk(steps, LANES), blk(hist, MB_CONV_DIM)],
        out_shape=[jax.ShapeDtypeStruct((steps, batch, MB_INNER), F32),
                   jax.ShapeDtypeStruct((steps, batch, MB_CONV_DIM), F32),
                   jax.ShapeDtypeStruct((steps, batch, LANES), F32),
                   jax.ShapeDtypeStruct((hist, batch, MB_CONV_DIM), F32)],
        scratch_shapes=[pltpu.VMEM((hist + steps, bb, MB_CONV_DIM), F32)],
        compiler_params=_params(1), name="mb_in_sample")(x_tm, hist_tm, *w)


def _ssd_kernel(*refs, q, valid, n_chunks, has_s0):
    if has_s0:
        (xbc_x, xbc_b, xbc_c, dt_ref, z_ref, alog_ref, dskip_ref, ng_ref, s0_ref,
         act_ref, sfin_ref, s_ref, ybuf) = refs
    else:
        (xbc_x, xbc_b, xbc_c, dt_ref, z_ref, alog_ref, dskip_ref, ng_ref,
         act_ref, sfin_ref, s_ref, ybuf) = refs
    c = pl.program_id(1)

    @pl.when(c == 0)
    def _():
        if has_s0:
            s_ref[...] = s0_ref[0]
        else:
            s_ref[...] = jnp.zeros(s_ref.shape, F32)

    dt = dt_ref[...]
    if valid < q:
        dt = jnp.where(lax.broadcasted_iota(jnp.int32, (q, LANES), 0) < valid, dt, 0.0)
    a = dt * (-jnp.exp(alog_ref[...]))
    rid = lax.broadcasted_iota(jnp.int32, (q, q), 0)
    cid = lax.broadcasted_iota(jnp.int32, (q, q), 1)
    tri = rid >= cid
    lower = jnp.where(tri, 1.0, 0.0)
    upper = jnp.where(rid <= cid, 1.0, 0.0)
    eye = jnp.where(rid == cid, 1.0, 0.0)
    hi = lax.Precision.HIGHEST
    cum = jnp.dot(lower, a, precision=hi, preferred_element_type=F32)
    tn = (((0,), (0,)), ((), ()))
    cum_t = lax.dot_general(a, upper, tn, precision=hi, preferred_element_type=F32)
    dt_t = lax.dot_general(dt, eye, tn, precision=hi, preferred_element_type=F32)
    r_per_g = MB_HEADS // MB_GROUPS
    for g in range(MB_GROUPS):
        gs = slice(g * MB_STATE, (g + 1) * MB_STATE)
        cg = xbc_c[:, gs].astype(BF16)
        bg = xbc_b[:, gs].astype(BF16)
        cb = _bdot_nt(cg, bg)
        for r in range(r_per_g):
            h = g * r_per_g + r
            hs = slice(h * MB_HEAD_DIM, (h + 1) * MB_HEAD_DIM)
            ch = cum[:, h:h + 1]
            decay = jnp.exp(jnp.where(tri, ch - cum_t[h:h + 1, :], -jnp.inf))
            m = cb * decay * dt_t[h:h + 1, :]
            xh = xbc_x[:, hs]
            sh = s_ref[h]
            y = _bdot(m, xh) + _bdot_nt(cg, sh) * jnp.exp(ch) + dskip_ref[:, h:h + 1] * xh
            ybuf[:, hs] = y
            cl = cum[q - 1:q, h:h + 1]
            w_end = jnp.exp(cl - ch) * dt[:, h:h + 1]
            s_ref[h] = jnp.exp(cl) * sh + _bdot_tn(xh * w_end, bg)
    y = ybuf[...] * _silu(z_ref[...])
    y = y * lax.rsqrt(jnp.mean(y * y, axis=-1, keepdims=True) + LN_EPS) * ng_ref[...]
    act_ref[...] = y.astype(act_ref.dtype)

    @pl.when(c == n_chunks - 1)
    def _():
        sfin_ref[0] = s_ref[...]


def _ssd(xbc, dt, z, alog, dskip, ng, s0, batch, q, valid, n_chunks, act_dtype):
    n = xbc.shape[0]
    has_s0 = s0 is not None
    rowblk = lambda cols, cb: pl.BlockSpec((q, cols), lambda b, c: (b * n_chunks + c, cb))
    st_spec = pl.BlockSpec((1, MB_HEADS, MB_HEAD_DIM, MB_STATE), lambda b, c: (b, 0, 0, 0))
    in_specs = [rowblk(MB_INNER, 0), rowblk(MB_GROUPS * MB_STATE, 2), rowblk(MB_GROUPS * MB_STATE, 3),
                rowblk(LANES, 0), rowblk(MB_INNER, 0),
                _const_spec(alog.shape), _const_spec(dskip.shape), _const_spec(ng.shape)]
    args = [xbc, xbc, xbc, dt, z, alog, dskip, ng]
    if has_s0:
        in_specs.append(st_spec)
        args.append(s0)
    kern = functools.partial(_ssd_kernel, q=q, valid=valid, n_chunks=n_chunks, has_s0=has_s0)
    return pl.pallas_call(
        kern, grid=(batch, n_chunks), in_specs=in_specs,
        out_specs=[rowblk(MB_INNER, 0), st_spec],
        out_shape=[jax.ShapeDtypeStruct((n, MB_INNER), act_dtype),
                   jax.ShapeDtypeStruct((batch, MB_HEADS, MB_HEAD_DIM, MB_STATE), F32)],
        scratch_shapes=[pltpu.VMEM((MB_HEADS, MB_HEAD_DIM, MB_STATE), F32), pltpu.VMEM((q, MB_INNER), F32)],
        compiler_params=_params(2), name="ssd")(*args)


def _at_in_kernel(x_ref, wq_ref, wk_ref, wv_ref, wqi_ref, wki_ref, wwi_ref, wz_ref, kg_ref, kb_ref,
                  q_ref, k_ref, v_ref, qi_ref, ki_ref, wi_ref, z_ref):
    xb = x_ref[...].astype(BF16)
    dot = lambda w: jnp.dot(xb, w[...], preferred_element_type=F32)
    q_ref[...] = dot(wq_ref) * (AT_HEAD_DIM ** -0.5)
    k_ref[...] = dot(wk_ref)
    v_ref[...] = dot(wv_ref)
    qi_ref[...] = dot(wqi_ref) * (IDX_DIM ** -0.5)
    ki_ref[...] = _layer_norm(dot(wki_ref), kg_ref[...], kb_ref[...])
    wi_ref[...] = dot(wwi_ref) * (IDX_HEADS ** -0.5)
    z_ref[...] = _silu(dot(wz_ref))


def _at_in(x, w, tm):
    n = x.shape[0]
    row = lambda c: pl.BlockSpec((tm, c), lambda i: (i, 0))
    widths = [D_MODEL, AT_KV_DIM, AT_KV_DIM, IDX_HEADS * IDX_DIM, IDX_DIM, LANES, D_MODEL]
    return pl.pallas_call(
        _at_in_kernel, grid=(n // tm,),
        in_specs=[row(D_MODEL)] + [_const_spec(a.shape) for a in w],
        out_specs=[row(c) for c in widths],
        out_shape=[jax.ShapeDtypeStruct((n, c), F32) for c in widths],
        compiler_params=_params(1), name="at_in")(x, *w)


ALIBI_COLS = 6


def _split3_bf16(x):
    hi = x.astype(jnp.bfloat16).astype(np.float32)
    mid = (x - hi).astype(jnp.bfloat16).astype(np.float32)
    lo = (x - hi - mid).astype(jnp.bfloat16).astype(np.float32)
    return hi, mid, lo


def _alibi_query_columns():
    s = np.asarray(ALIBI_SLOPES, np.float32) * np.float32(np.log2(np.e))
    parts = _split3_bf16(s)
    out = np.zeros((AT_HEADS, LANES), np.float32)
    for i, part in enumerate(parts):
        out[:, AT_HEAD_DIM + i] = 16.0 * part
        out[:, AT_HEAD_DIM + 3 + i] = part
    return out.reshape(1, AT_HEADS * LANES)


def _slot_weights(w, n_slots):
    d = w.shape[0]
    w = w.reshape(d, n_slots, AT_HEAD_DIM)
    return jnp.pad(w, ((0, 0), (0, 0), (0, LANES - AT_HEAD_DIM))).reshape(d, n_slots * LANES)


def _at_in_prompt_kernel(x_ref, wqs_ref, wk_ref, wv_ref, wks_ref, wvs_ref, wqi_ref, wki_ref, wwi_ref, wz_ref,
                         kg_ref, kb_ref, qcol_ref,
                         k_ref, v_ref, qi_ref, ki_ref, wi_ref, z_ref, qaug_ref, kaug_ref, vaug_ref,
                         *, tm, tiles_per_seq):
    xb = x_ref[...].astype(BF16)
    dot = lambda w: jnp.dot(xb, w[...], preferred_element_type=F32)
    k_ref[...] = dot(wk_ref)
    v_ref[...] = dot(wv_ref)
    qi_ref[...] = dot(wqi_ref) * (IDX_DIM ** -0.5)
    ki_ref[...] = _layer_norm(dot(wki_ref), kg_ref[...], kb_ref[...])
    wi_ref[...] = dot(wwi_ref) * (IDX_HEADS ** -0.5)
    z_ref[...] = _silu(dot(wz_ref))
    qaug_ref[...] = (dot(wqs_ref) * (AT_HEAD_DIM ** -0.5 * float(np.log2(np.e))) + qcol_ref[...]).astype(BF16)
    pos = (pl.program_id(0) % tiles_per_seq) * tm + lax.broadcasted_iota(jnp.int32, (tm, 1), 0)
    a = (pos >> 4).astype(F32)
    c = (pos & 15).astype(F32)
    col = lax.broadcasted_iota(jnp.int32, (1, AT_KV_HEADS * LANES), 1) & (LANES - 1)
    in_a = (col >= AT_HEAD_DIM) & (col < AT_HEAD_DIM + 3)
    in_c = (col >= AT_HEAD_DIM + 3) & (col < AT_HEAD_DIM + ALIBI_COLS)
    kaug_ref[...] = (dot(wks_ref) + jnp.where(in_a, a, jnp.where(in_c, c, 0.0))).astype(BF16)
    vaug_ref[...] = (dot(wvs_ref) + jnp.where(col == AT_HEAD_DIM, 1.0, 0.0)).astype(BF16)


def _at_in_prompt(x, w, seq, tm):
    n = x.shape[0]
    row = lambda c: pl.BlockSpec((tm, c), lambda i: (i, 0))
    outs = [(AT_KV_DIM, F32), (AT_KV_DIM, F32), (IDX_HEADS * IDX_DIM, F32), (IDX_DIM, F32), (LANES, F32),
            (D_MODEL, F32), (AT_HEADS * LANES, BF16), (AT_KV_HEADS * LANES, BF16), (AT_KV_HEADS * LANES, BF16)]
    kern = functools.partial(_at_in_prompt_kernel, tm=tm, tiles_per_seq=seq // tm)
    return pl.pallas_call(
        kern, grid=(n // tm,),
        in_specs=[row(D_MODEL)] + [_const_spec(a.shape) for a in w],
        out_specs=[row(c) for c, _ in outs],
        out_shape=[jax.ShapeDtypeStruct((n, c), dt) for c, dt in outs],
        compiler_params=_params(1), name="at_in_prompt")(x, *w)


def _page_dmas(pt_ref, seq0, n_seq, n_pages, src_hbm, dst_of, sem):
    out = []
    for g in range(n_seq):
        for p in range(n_pages):
            out.append(pltpu.make_async_copy(src_hbm.at[pt_ref[(seq0 + g) * n_pages + p]], dst_of(g, p), sem))
    return out


def _prefetch_step(dmas):
    i = pl.program_id(0)
    slot = i % 2

    @pl.when(i == 0)
    def _():
        for c in dmas(0, 0):
            c.start()

    @pl.when(i + 1 < pl.num_programs(0))
    def _():
        for c in dmas(i + 1, 1 - slot):
            c.start()
    for c in dmas(i, slot):
        c.wait()
    return slot


def _dsa_select_sample_kernel(pt_ref, qi_ref, wi_ref, kin_ref, kidx_hbm, bias_ref, kibuf, sem, idxbuf, keybuf,
                              *, g_seq, n_pages, page, steps, topk):
    past = n_pages * page
    s_pad = past + LANES
    n_tiles = s_pad // LANES
    rows = g_seq * 8
    slot = _prefetch_step(lambda step, sl: _page_dmas(
        pt_ref, step * g_seq, g_seq, n_pages, kidx_hbm,
        lambda g, p: kibuf.at[sl, g, pl.ds(p * page, page)], sem.at[sl]))

    idxbuf[...] = jnp.full((rows, s_pad), -jnp.inf, F32)
    tt = lax.broadcasted_iota(jnp.int32, (steps, 8), 0)
    jj = lax.broadcasted_iota(jnp.int32, (steps, 8), 1)
    for g in range(g_seq):
        qs = qi_ref[g].astype(BF16)
        w = wi_ref[g]
        sc = _bdot_nt(qs, kibuf[slot, g])
        val = jnp.maximum(sc, 0.0) * jnp.concatenate([w] * (past // LANES), axis=1)
        idxbuf[g * 8:g * 8 + steps, 0:past] = jnp.sum(val.reshape(steps, IDX_HEADS, past), axis=1)
        scn = _bdot_nt(qs, kin_ref[g])
        valn = jnp.maximum(scn, 0.0) * w[:, 0:8]
        idn = jnp.sum(valn.reshape(steps, IDX_HEADS, 8), axis=1)
        idxbuf[g * 8:g * 8 + steps, past:past + 8] = jnp.where(jj <= tt, idn, -jnp.inf)

    bits = pltpu.bitcast(idxbuf[...], jnp.int32)
    keybuf[...] = jnp.where(bits < 0, bits ^ jnp.int32(0x7FFFFFFF), bits)

    def tile(c):
        return keybuf[:, c * LANES:(c + 1) * LANES]

    def bit_pass(b, carry):
        thr, cnt_gt = carry
        cand = thr + jnp.left_shift(jnp.int32(1), 31 - b)
        part = jnp.zeros((rows, LANES), F32)
        for c in range(n_tiles):
            part = part + jnp.where(tile(c) >= cand, 1.0, 0.0)
        total = jnp.sum(part, axis=1, keepdims=True)
        ok = total >= float(topk)
        return jnp.where(ok, cand, thr), jnp.where(ok, cnt_gt, total)
    thr, cnt_gt = lax.fori_loop(0, 32, bit_pass, (jnp.full((rows, 1), INT_MIN, jnp.int32),
                                                  jnp.zeros((rows, 1), F32)))
    need_eq = float(topk) - cnt_gt

    incl = jnp.where(lax.broadcasted_iota(jnp.int32, (LANES, LANES), 0)
                     <= lax.broadcasted_iota(jnp.int32, (LANES, LANES), 1), 1.0, 0.0).astype(BF16)
    eq_rows = jnp.concatenate([jnp.where(tile(c) == thr, 1.0, 0.0).astype(BF16) for c in range(n_tiles)], axis=0)
    pre = jnp.dot(eq_rows, incl, preferred_element_type=F32)
    run = jnp.zeros((rows, 1), F32)
    for c in range(n_tiles):
        key = tile(c)
        pc = pre[c * rows:(c + 1) * rows] + run
        sel = (key > thr) | ((key == thr) & (pc <= need_eq))
        idxbuf[:, c * LANES:(c + 1) * LANES] = jnp.where(sel, 0.0, NEG_BIG)
        run = pc[:, LANES - 1:LANES]
    for g in range(g_seq):
        bias_ref[g] = idxbuf[g * 8:(g + 1) * 8, :]


def _dsa_select_sample(page_table, qi_st, wi_col, ki_new, cache_ki, topk, g_seq):
    bs, n_pages = page_table.shape
    page = cache_ki.shape[1]
    past = n_pages * page
    s_pad = past + LANES
    rows_q = qi_st.shape[1]
    blk = lambda r, c: pl.BlockSpec((g_seq, r, c), lambda i, pt: (i, 0, 0))
    kern = functools.partial(_dsa_select_sample_kernel, g_seq=g_seq, n_pages=n_pages, page=page,
                             steps=rows_q // IDX_HEADS, topk=topk)
    return pl.pallas_call(
        kern,
        grid_spec=pltpu.PrefetchScalarGridSpec(
            num_scalar_prefetch=1, grid=(bs // g_seq,),
            in_specs=[blk(rows_q, IDX_DIM), blk(rows_q, LANES), blk(8, IDX_DIM), pl.BlockSpec(memory_space=pl.ANY)],
            out_specs=blk(8, s_pad),
            scratch_shapes=[pltpu.VMEM((2, g_seq, past, IDX_DIM), F32), pltpu.SemaphoreType.DMA((2,)),
                            pltpu.VMEM((g_seq * 8, s_pad), F32), pltpu.VMEM((g_seq * 8, s_pad), jnp.int32)]),
        out_shape=jax.ShapeDtypeStruct((bs, 8, s_pad), F32),
        compiler_params=_params(1), name="dsa_select_sample")(page_table.reshape(-1), qi_st, wi_col, ki_new, cache_ki)


def _dsa_attend_sample_kernel(pt_ref, qbd_ref, zbd_ref, bias_ref, knew_ref, vnew_ref, meta_ref, k_hbm, v_hbm,
                              out_ref, kbuf, vbuf, sem, *, n_pages, page, steps):
    past = n_pages * page

    def dmas(step, sl):
        return (_page_dmas(pt_ref, step, 1, n_pages, k_hbm, lambda g, p: kbuf.at[sl, pl.ds(p * page, page)],
                           sem.at[0, sl])
                + _page_dmas(pt_ref, step, 1, n_pages, v_hbm, lambda g, p: vbuf.at[sl, pl.ds(p * page, page)],
                             sem.at[1, sl]))
    slot = _prefetch_step(dmas)

    q = qbd_ref[0].astype(BF16)
    slope = meta_ref[:, 0:1]
    tq = meta_ref[:, 1:2]
    bias = jnp.concatenate([jnp.broadcast_to(bias_ref[0, t:t + 1, :], (AT_HEADS, past + LANES))
                            for t in range(steps)], axis=0)
    pos_old = lax.broadcasted_iota(jnp.int32, (1, past), 1).astype(F32)
    l_old = _bdot_nt(q, kbuf[slot]) - slope * ((float(past) + tq) - pos_old) + bias[:, 0:past]
    pos_new = lax.broadcasted_iota(jnp.int32, (1, 8), 1).astype(F32)
    l_new = _bdot_nt(q, knew_ref[0]) - slope * (tq - pos_new) + bias[:, past:past + 8]
    m = jnp.maximum(jnp.max(l_old, axis=1, keepdims=True), jnp.max(l_new, axis=1, keepdims=True))
    p_old = jnp.exp(l_old - m)
    p_new = jnp.exp(l_new - m)
    den = jnp.sum(p_old, axis=1, keepdims=True) + jnp.sum(p_new, axis=1, keepdims=True)
    o = _bdot(p_old, vbuf[slot]) + _bdot(p_new, vnew_ref[0])
    out_ref[0] = o / den * zbd_ref[0]


def _dsa_attend_sample(page_table, qbd, zbd, bias, k_new, v_new, meta, cache_k, cache_v):
    bs, n_pages = page_table.shape
    page = cache_k.shape[1]
    past = n_pages * page
    rows = qbd.shape[1]
    seq = lambda r, c: pl.BlockSpec((1, r, c), lambda i, pt: (i, 0, 0))
    hbm = pl.BlockSpec(memory_space=pl.ANY)
    kern = functools.partial(_dsa_attend_sample_kernel, n_pages=n_pages, page=page, steps=rows // AT_HEADS)
    return pl.pallas_call(
        kern,
        grid_spec=pltpu.PrefetchScalarGridSpec(
            num_scalar_prefetch=1, grid=(bs,),
            in_specs=[seq(rows, AT_KV_DIM), seq(rows, AT_KV_DIM), seq(8, past + LANES), seq(8, AT_KV_DIM),
                      seq(8, AT_KV_DIM), pl.BlockSpec(meta.shape, lambda i, pt: (0, 0)), hbm, hbm],
            out_specs=seq(rows, AT_KV_DIM),
            scratch_shapes=[pltpu.VMEM((2, past, AT_KV_DIM), F32), pltpu.VMEM((2, past, AT_KV_DIM), F32),
                            pltpu.SemaphoreType.DMA((2, 2))]),
        out_shape=jax.ShapeDtypeStruct((bs, rows, AT_KV_DIM), F32),
        compiler_params=_params(1), name="dsa_attend_sample")(
            page_table.reshape(-1), qbd, zbd, bias, k_new, v_new, meta, cache_k, cache_v)


def _dsa_kernel(qaug_ref, qi_ref, wi_ref, z_ref, kaug_ref, vaug_ref, ki_ref, act_ref,
                keys_ref, bias_ref, mrun_ref, m_ref, acc_ref, p_ref, *, t, topk):
    kb_sz = KEY_BLOCK
    pos0 = pl.program_id(1) * t
    n_kb = (pos0 + t - 1) // kb_sz + 1
    rowpos = pos0 + lax.broadcasted_iota(jnp.int32, (t, 1), 0)
    lane = lax.broadcasted_iota(jnp.int32, (1, kb_sz), 1)

    def score_block(kb, carry):
        ks = pl.ds(pl.multiple_of(kb * kb_sz, kb_sz), kb_sz)
        kib = ki_ref[ks, :].astype(BF16)
        acc = jnp.zeros((t, kb_sz), F32)
        for i in range(IDX_HEADS):
            sc = _bdot_nt(qi_ref[:, i * IDX_DIM:(i + 1) * IDX_DIM], kib)
            acc = acc + jnp.maximum(sc, 0.0) * wi_ref[:, i:i + 1]
        acc = jnp.where(kb * kb_sz + lane <= rowpos, acc, -jnp.inf)
        bits = pltpu.bitcast(acc, jnp.int32)
        keys_ref[kb] = jnp.where(bits < 0, bits ^ jnp.int32(0x7FFFFFFF), bits)
        return carry
    lax.fori_loop(0, n_kb, score_block, 0)

    def bit_pass(b, carry):
        thr, cnt_gt = carry
        cand = thr + jnp.left_shift(jnp.int32(1), 31 - b)

        def body(kb, cnt):
            hit = jnp.where(keys_ref[kb] >= cand, 1.0, 0.0)
            for s in range(kb_sz // LANES):
                cnt = cnt + hit[:, s * LANES:(s + 1) * LANES]
            return cnt
        cnt = lax.fori_loop(0, n_kb, body, jnp.zeros((t, LANES), F32))
        total = jnp.sum(cnt, axis=1, keepdims=True)
        ok = total >= float(topk)
        return jnp.where(ok, cand, thr), jnp.where(ok, cnt_gt, total)
    thr, cnt_gt = lax.fori_loop(0, 32, bit_pass, (jnp.full((t, 1), INT_MIN, jnp.int32), jnp.zeros((t, 1), F32)))
    need_eq = float(topk) - cnt_gt

    mrun_ref[...] = jnp.full(mrun_ref.shape, NEG_BIG, F32)
    incl = jnp.where(lax.broadcasted_iota(jnp.int32, (kb_sz, kb_sz), 0)
                     <= lax.broadcasted_iota(jnp.int32, (kb_sz, kb_sz), 1), 1.0, 0.0).astype(BF16)
    r_per_g = AT_HEADS // AT_KV_HEADS
    nt = (((1,), (1,)), ((), ()))

    def logits(h, kg, mask_bias):
        return lax.dot_general(qaug_ref[:, h * LANES:(h + 1) * LANES], kg, nt,
                               preferred_element_type=F32) + mask_bias

    def max_block(kb, run_eq):
        ks = pl.ds(pl.multiple_of(kb * kb_sz, kb_sz), kb_sz)
        key = keys_ref[kb]
        eq = key == thr
        pre = jnp.dot(jnp.where(eq, 1.0, 0.0).astype(BF16), incl, preferred_element_type=F32) + run_eq
        sel = ((key > thr) | (eq & (pre <= need_eq))) & (kb * kb_sz + lane <= rowpos)
        mask_bias = jnp.where(sel, 0.0, NEG_BIG)
        bias_ref[kb] = mask_bias
        for g in range(AT_KV_HEADS):
            kg = kaug_ref[ks, g * LANES:(g + 1) * LANES]
            for r in range(r_per_g):
                h = g * r_per_g + r
                s = logits(h, kg, mask_bias)
                tile_max = s[:, 0:LANES]
                for c in range(1, kb_sz // LANES):
                    tile_max = jnp.maximum(tile_max, s[:, c * LANES:(c + 1) * LANES])
                mrun_ref[h] = jnp.maximum(mrun_ref[h], tile_max)
        return pre[:, kb_sz - 1:kb_sz]
    lax.fori_loop(0, n_kb, max_block, jnp.zeros((t, 1), F32))
    for h in range(AT_HEADS):
        m_ref[h] = jnp.max(mrun_ref[h], axis=1, keepdims=True)

    acc_ref[...] = jnp.zeros(acc_ref.shape, F32)

    def attend_block(kb, carry):
        ks = pl.ds(pl.multiple_of(kb * kb_sz, kb_sz), kb_sz)
        mask_bias = bias_ref[kb]
        for g in range(AT_KV_HEADS):
            kg = kaug_ref[ks, g * LANES:(g + 1) * LANES]
            for r in range(r_per_g):
                h = g * r_per_g + r
                p_ref[h] = jnp.exp2(logits(h, kg, mask_bias) - m_ref[h]).astype(BF16)
        for g in range(AT_KV_HEADS):
            vg = vaug_ref[ks, g * LANES:(g + 1) * LANES]
            for r in range(r_per_g):
                h = g * r_per_g + r
                acc_ref[h] = acc_ref[h] + jnp.dot(p_ref[h], vg, preferred_element_type=F32)
        return carry
    lax.fori_loop(0, n_kb, attend_block, 0)

    for h in range(AT_HEADS):
        hs = slice(h * AT_HEAD_DIM, (h + 1) * AT_HEAD_DIM)
        a = acc_ref[h]
        o = a[:, 0:AT_HEAD_DIM] / a[:, AT_HEAD_DIM:AT_HEAD_DIM + 1]
        act_ref[:, hs] = (o * z_ref[:, hs]).astype(act_ref.dtype)


def _dsa(qaug, qi, wi, z, kaug, vaug, ki_all, batch, n_qb, t, s_len, topk):
    n = qaug.shape[0]
    qrow = lambda c: pl.BlockSpec((t, c), lambda b, j: (b * n_qb + j, 0))
    krow = lambda c: pl.BlockSpec((s_len, c), lambda b, j: (b, 0))
    n_kb = s_len // KEY_BLOCK
    kern = functools.partial(_dsa_kernel, t=t, topk=topk)
    return pl.pallas_call(
        kern, grid=(batch, n_qb),
        in_specs=[qrow(AT_HEADS * LANES), qrow(IDX_HEADS * IDX_DIM), qrow(LANES), qrow(D_MODEL),
                  krow(AT_KV_HEADS * LANES), krow(AT_KV_HEADS * LANES), krow(IDX_DIM)],
        out_specs=qrow(D_MODEL), out_shape=jax.ShapeDtypeStruct((n, D_MODEL), BF16),
        scratch_shapes=[pltpu.VMEM((n_kb, t, KEY_BLOCK), jnp.int32), pltpu.VMEM((n_kb, t, KEY_BLOCK), F32),
                        pltpu.VMEM((AT_HEADS, t, LANES), F32), pltpu.VMEM((AT_HEADS, t, 1), F32),
                        pltpu.VMEM((AT_HEADS, t, LANES), F32), pltpu.VMEM((AT_HEADS, t, KEY_BLOCK), BF16)],
        compiler_params=_params(2), name="dsa")(qaug, qi, wi, z, kaug, vaug, ki_all)


def _gm_kernel(*refs, tm, chunk, emit_v):
    if emit_v:
        (x_ref, win_ref, bin_ref, g_ref, b_ref, wmix_ref, bmix_ref, act_ref, v_ref, ubuf, vbuf, zbuf) = refs
    else:
        (x_ref, win_ref, bin_ref, g_ref, b_ref, wmix_ref, bmix_ref, act_ref, ubuf, vbuf, zbuf) = refs
    xb = x_ref[...].astype(BF16)
    w = GM_WIDTH
    cols = 512
    for cc in range(w // cols):
        cs = slice(cc * cols, (cc + 1) * cols)
        proj = lambda off: (jnp.dot(xb, win_ref[:, off + cc * cols:off + (cc + 1) * cols],
                                    preferred_element_type=F32) + bin_ref[:, off + cc * cols:off + (cc + 1) * cols])
        ubuf[:, cs] = _gelu_tanh(proj(0))
        vbuf[:, cs] = _gelu_tanh(proj(w))
        zbuf[:, cs] = _silu(proj(2 * w))

    rows = 32

    def norm_rows(r, carry):
        rs = pl.ds(pl.multiple_of(r * rows, rows), rows)
        v = _layer_norm(vbuf[rs, :], g_ref[...], b_ref[...])
        vbuf[rs, :] = v
        if emit_v:
            v_ref[rs, :] = v
        return carry
    lax.fori_loop(0, tm // rows, norm_rows, 0)

    gw = w // GM_GROUPS
    for c in range(tm // chunk):
        rs = slice(c * chunk, (c + 1) * chunk)
        for g in range(GM_GROUPS):
            cs = slice(g * gw, (g + 1) * gw)
            s = jnp.dot(wmix_ref[g], vbuf[rs, cs].astype(BF16), preferred_element_type=F32) + bmix_ref[:, g:g + 1]
            act_ref[rs, cs] = (ubuf[rs, cs] * s * zbuf[rs, cs]).astype(act_ref.dtype)


def _gm(x, w, tm, chunk, emit_v):
    n = x.shape[0]
    row = lambda c: pl.BlockSpec((tm, c), lambda i: (i, 0))
    out_specs = [row(GM_WIDTH)]
    out_shape = [jax.ShapeDtypeStruct((n, GM_WIDTH), BF16)]
    if emit_v:
        out_specs.append(row(GM_WIDTH))
        out_shape.append(jax.ShapeDtypeStruct((n, GM_WIDTH), F32))
    kern = functools.partial(_gm_kernel, tm=tm, chunk=chunk, emit_v=emit_v)
    return pl.pallas_call(
        kern, grid=(n // tm,),
        in_specs=[row(D_MODEL)] + [_const_spec(a.shape) for a in w],
        out_specs=out_specs, out_shape=out_shape,
        scratch_shapes=[pltpu.VMEM((tm, GM_WIDTH), F32)] * 3,
        compiler_params=_params(1), name="gmlp")(x, *w)


def _pad_lanes(a, width=LANES):
    return jnp.pad(a, [(0, 0)] * (a.ndim - 1) + [(0, width - a.shape[-1])])


def kernel(x_prompt, x_sample, p_prompt, p_sample, state_cf_conv, state_mb_conv, state_mb_ssm,
           cache_k, cache_v, cache_kidx, page_table, post_ln_g, post_ln_b, ple_w, ple_gate_w,
           cf_w_in, cf_b_in, cf_w_dw, cf_b_dw, cf_ln_g, cf_ln_b, cf_w_out, cf_b_out,
           mb_w_in, mb_w_conv, mb_b_conv, mb_dt_bias, mb_a_log, mb_d_skip, mb_norm_g, mb_w_out,
           at_w_in, at_ki_ln_g, at_ki_ln_b, at_w_out,
           gm_w_in, gm_b_in, gm_ln_g, gm_ln_b, gm_w_s, gm_b_s, gm_w_out):
    bp, lp, d = x_prompt.shape
    bs, ls, _ = x_sample.shape
    row2 = lambda v: v.reshape(1, -1)
    bf = lambda v: v.astype(BF16)

    ple_wb, ple_gate_wb = bf(ple_w), bf(ple_gate_w)
    cf_w = (bf(cf_w_in), row2(cf_b_in), cf_w_dw, row2(cf_b_dw), row2(cf_ln_g), row2(cf_ln_b))
    mb_in_w = (bf(mb_w_in[:, :MB_INNER]), bf(mb_w_in[:, MB_INNER:MB_INNER + MB_CONV_DIM]),
               bf(_pad_lanes(mb_w_in[:, MB_INNER + MB_CONV_DIM:])), _pad_lanes(row2(mb_dt_bias)),
               mb_w_conv, row2(mb_b_conv))
    ssd_w = (_pad_lanes(row2(mb_a_log)), _pad_lanes(row2(mb_d_skip)), row2(mb_norm_g))
    sizes = [AT_HEADS * AT_HEAD_DIM, AT_KV_DIM, AT_KV_DIM, IDX_HEADS * IDX_DIM, IDX_DIM, IDX_HEADS,
             AT_HEADS * AT_HEAD_DIM]
    offs = np.concatenate([[0], np.cumsum(sizes)]).tolist()
    at_parts = [at_w_in[:, offs[i]:offs[i + 1]] for i in range(len(sizes))]
    at_parts[5] = _pad_lanes(at_parts[5])
    at_w = tuple(bf(a) for a in at_parts) + (row2(at_ki_ln_g), row2(at_ki_ln_b))
    wq_b, wk_b, wv_b, wqi_b, wki_b, wwi_b, wz_b = at_w[:7]
    at_w_prompt = (_slot_weights(wq_b, AT_HEADS), wk_b, wv_b, _slot_weights(wk_b, AT_KV_HEADS),
                   _slot_weights(wv_b, AT_KV_HEADS), wqi_b, wki_b, wwi_b, wz_b,
                   row2(at_ki_ln_g), row2(at_ki_ln_b), jnp.asarray(_alibi_query_columns()))
    gm_common = (bf(gm_w_in), row2(gm_b_in), row2(gm_ln_g), row2(gm_ln_b))
    zeros_d = jnp.zeros((1, d), F32)
    out_w = [(bf(cf_w_out), row2(cf_b_out)), (bf(mb_w_out), zeros_d), (bf(at_w_out), zeros_d),
             (bf(gm_w_out), zeros_d)]

    def tail(i, act, x, p, tm):
        wo, bo = out_w[i]
        return _tail(act, x, p, wo, bo, row2(post_ln_g[i]), row2(post_ln_b[i]), ple_gate_wb[i], ple_wb[i], tm)

    tm = 512
    n_p = bp * lp
    x = x_prompt.reshape(n_p, d)
    pp = p_prompt.reshape(DEPTH, n_p, D_PLE)

    act, cf_conv_p = _cf_prompt(x, cf_w, bp, lp, tm)
    x = tail(0, act, x, pp[0], tm)

    z, xbc, dt, mb_conv_p = _mb_in_prompt(x, mb_in_w, bp, lp, tm)
    act, mb_ssm_p = _ssd(xbc, dt, z, *ssd_w, None, bp, MB_CHUNK, MB_CHUNK, lp // MB_CHUNK, BF16)
    x = tail(1, act, x, pp[1], tm)

    k, v, qi, ki, wi, zs, qaug, kaug, vaug = _at_in_prompt(x, at_w_prompt, lp, tm)
    act = _dsa(qaug, qi, wi, zs, kaug, vaug, ki, bp, lp // Q_BLOCK, Q_BLOCK, lp, min(TOPK_MAX, lp // 4))
    x = tail(2, act, x, pp[2], tm)
    k_p = k.reshape(bp, lp, AT_KV_HEADS, AT_HEAD_DIM)
    v_p = v.reshape(bp, lp, AT_KV_HEADS, AT_HEAD_DIM)
    kidx_p = ki.reshape(bp, lp, IDX_DIM)

    gm_w_p = gm_common + (bf(jnp.tril(gm_w_s)), gm_b_s.T)
    (act,) = _gm(x, gm_w_p, tm, GM_CHUNK, False)
    y_prompt = tail(3, act, x, pp[3], tm).reshape(bp, lp, d)

    n_s = bs * ls
    bb = 32
    to_tm = lambda a: jnp.swapaxes(a, 0, 1)
    x_tm = to_tm(x_sample)
    x = x_tm.reshape(n_s, d)
    ps = jnp.swapaxes(p_sample, 1, 2).reshape(DEPTH, n_s, D_PLE)
    tm_s = n_s

    act, cf_st = _cf_sample(x_tm, to_tm(state_cf_conv), cf_w, bb)
    cf_conv_s = to_tm(cf_st)
    x = tail(0, act.reshape(n_s, d), x, ps[0], tm_s)

    z, xbc, dt, mb_st = _mb_in_sample(x.reshape(ls, bs, d), to_tm(state_mb_conv), mb_in_w, bb)
    mb_conv_s = to_tm(mb_st)
    qs = 8

    def to_bm(a):
        a = jnp.pad(to_tm(a), ((0, 0), (0, qs - ls), (0, 0)))
        return a.reshape(bs * qs, a.shape[-1])

    def from_bm(a):
        a = a.reshape(bs, qs, a.shape[-1])[:, :ls]
        return to_tm(a).reshape(n_s, a.shape[-1])

    act, mb_ssm_s = _ssd(to_bm(xbc), to_bm(dt), to_bm(z), *ssd_w, state_mb_ssm, bs, qs, ls, 1, F32)
    x = tail(1, from_bm(act), x, ps[1], tm_s)

    q, k, v, qi, ki, wi, zs = _at_in(x, at_w, tm_s)
    k_s = to_tm(k.reshape(ls, bs, AT_KV_HEADS, AT_HEAD_DIM))
    v_s = to_tm(v.reshape(ls, bs, AT_KV_HEADS, AT_HEAD_DIM))
    kidx_s = to_tm(ki.reshape(ls, bs, IDX_DIM))
    n_pool, page = cache_k.shape[:2]
    past = page_table.shape[1] * page
    seq_major = lambda a: to_tm(a.reshape(ls, bs, a.shape[-1]))
    r_per_g = AT_HEADS // AT_KV_HEADS
    group_of_head = (np.arange(AT_HEADS)[:, None] // r_per_g == np.arange(AT_KV_HEADS)[None, :]).astype(np.float32)

    def block_diag(a):
        a = a.reshape(bs, ls, AT_HEADS, 1, AT_HEAD_DIM) * group_of_head[None, None, :, :, None]
        return a.reshape(bs, ls * AT_HEADS, AT_KV_DIM)

    pad_steps = lambda a: jnp.pad(a, ((0, 0), (0, 8 - ls), (0, 0)))
    qi_st = seq_major(qi).reshape(bs, ls * IDX_HEADS, IDX_DIM)
    wi_col = jnp.broadcast_to(seq_major(wi)[:, :, :IDX_HEADS].reshape(bs, ls * IDX_HEADS, 1),
                              (bs, ls * IDX_HEADS, LANES))
    bias = _dsa_select_sample(page_table, qi_st, wi_col, pad_steps(seq_major(ki)), cache_kidx,
                              min(TOPK_MAX, (past + ls) // 4), 8)
    meta = np.zeros((ls * AT_HEADS, LANES), np.float32)
    meta[:, 0] = np.tile(np.asarray(ALIBI_SLOPES, np.float32), ls)
    meta[:, 1] = np.repeat(np.arange(ls, dtype=np.float32), AT_HEADS)
    o = _dsa_attend_sample(page_table, block_diag(seq_major(q)), block_diag(seq_major(zs)), bias,
                           pad_steps(seq_major(k)), pad_steps(seq_major(v)), jnp.asarray(meta),
                           cache_k.reshape(n_pool, page, AT_KV_DIM), cache_v.reshape(n_pool, page, AT_KV_DIM))
    act = o.reshape(bs, ls, AT_HEADS, AT_KV_HEADS, AT_HEAD_DIM).sum(axis=3).reshape(bs, ls, d)
    x = tail(2, to_tm(act).reshape(n_s, d), x, ps[2], tm_s)

    mix = jnp.stack([jnp.kron(jnp.tril(gm_w_s[g, :ls, :ls]), jnp.eye(bs, dtype=F32)) for g in range(GM_GROUPS)])
    gm_w_smp = gm_common + (bf(mix), jnp.repeat(gm_b_s[:, :ls].T, bs, axis=0))
    act, gm_v = _gm(x, gm_w_smp, tm_s, n_s, True)
    y_s = tail(3, act, x, ps[3], tm_s)
    y_sample = to_tm(y_s.reshape(ls, bs, d))
    gm_v_s = to_tm(gm_v.reshape(ls, bs, GM_WIDTH))

    return (y_prompt, y_sample, cf_conv_p, cf_conv_s, mb_conv_p, mb_conv_s, mb_ssm_p, mb_ssm_s,
            k_p, v_p, kidx_p, k_s, v_s, kidx_s, gm_v_s)
```

```python
import functools

import numpy as np
import jax
import jax.numpy as jnp
from jax import lax
from jax.experimental import pallas as pl
from jax.experimental.pallas import tpu as pltpu

F32 = jnp.float32
BF16 = jnp.bfloat16

D_MODEL = 1024
D_PLE = 256
DEPTH = 4
ALPHA_DN = (2 * DEPTH) ** 0.25
LN_EPS = 1e-5

CF_KERNEL = 31
CF_HIST = CF_KERNEL - 1
MB_INNER = 2048
MB_HEAD_DIM = 64
MB_HEADS = 32
MB_GROUPS = 8
MB_STATE = 128
MB_CONV = 4
MB_CONV_DIM = 4096
MB_CHUNK = 128
AT_HEADS = 16
AT_HEAD_DIM = 64
AT_KV_HEADS = 4
AT_KV_DIM = AT_KV_HEADS * AT_HEAD_DIM
IDX_HEADS = 8
IDX_DIM = 64
TOPK_MAX = 256
Q_BLOCK = 128
GM_WIDTH = 2048
GM_GROUPS = 4
GM_CHUNK = 128

LANES = 128
KEY_BLOCK = 512
NEG_BIG = -1e30
INT_MIN = -(2 ** 31)
VMEM_LIMIT = 56 * 1024 * 1024

ALIBI_SLOPES = [float(s) for s in
                (np.float32(2.0) ** (-8.0 * np.arange(1, AT_HEADS + 1, dtype=np.float32) / AT_HEADS))]


def _bdot(a, b):
    return jnp.dot(a.astype(BF16), b.astype(BF16), preferred_element_type=F32)


def _bdot_nt(a, b):
    return lax.dot_general(a.astype(BF16), b.astype(BF16), (((1,), (1,)), ((), ())),
                           preferred_element_type=F32)


def _bdot_tn(a, b):
    return lax.dot_general(a.astype(BF16), b.astype(BF16), (((0,), (0,)), ((), ())),
                           preferred_element_type=F32)


def _sigmoid(x):
    return 1.0 / (1.0 + jnp.exp(-x))


def _silu(x):
    return x * _sigmoid(x)


def _gelu_tanh(x):
    return x * (0.5 * (1.0 + jnp.tanh(np.sqrt(2.0 / np.pi).astype(np.float32) * (x + 0.044715 * (x * x * x)))))


def _softplus(x):
    return jnp.maximum(x, 0.0) + jnp.log1p(jnp.exp(-jnp.abs(x)))


def _layer_norm(x, g, b):
    mu = jnp.mean(x, axis=-1, keepdims=True)
    xc = x - mu
    var = jnp.mean(xc * xc, axis=-1, keepdims=True)
    return xc * lax.rsqrt(var + LN_EPS) * g + b


def _const_spec(shape):
    nd = len(shape)
    return pl.BlockSpec(shape, lambda *_: (0,) * nd)


def _params(n_axes):
    return pltpu.CompilerParams(dimension_semantics=("arbitrary",) * n_axes,
                                vmem_limit_bytes=VMEM_LIMIT)


def _tail_kernel(act_ref, x_ref, p_ref, wo_ref, bo_ref, g_ref, b_ref, wg_ref, wp_ref, o_ref):
    out = _bdot(act_ref[...], wo_ref[...]) + bo_ref[...]
    h = _layer_norm(ALPHA_DN * x_ref[...] + out, g_ref[...], b_ref[...])
    gate = _sigmoid(_bdot(h, wg_ref[...]))
    o_ref[...] = h + gate * _bdot(p_ref[...], wp_ref[...])


def _tail(act, x, p, wo, bo, g, b, wg, wp, tm):
    n, k = act.shape
    row = lambda c: pl.BlockSpec((tm, c), lambda i: (i, 0))
    return pl.pallas_call(
        _tail_kernel, grid=(n // tm,),
        in_specs=[row(k), row(D_MODEL), row(D_PLE), _const_spec(wo.shape), _const_spec(bo.shape),
                  _const_spec(g.shape), _const_spec(b.shape), _const_spec(wg.shape), _const_spec(wp.shape)],
        out_specs=row(D_MODEL), out_shape=jax.ShapeDtypeStruct((n, D_MODEL), F32),
        compiler_params=_params(1), name="layer_tail")(act, x, p, wo, bo, g, b, wg, wp)


def _cf_project(x2d, win_ref, bin_ref):
    xb = x2d.astype(BF16)
    d = D_MODEL
    a = jnp.dot(xb, win_ref[:, 0:d], preferred_element_type=F32) + bin_ref[:, 0:d]
    gl = jnp.dot(xb, win_ref[:, d:2 * d], preferred_element_type=F32) + bin_ref[:, d:2 * d]
    z = jnp.dot(xb, win_ref[:, 2 * d:3 * d], preferred_element_type=F32) + bin_ref[:, 2 * d:3 * d]
    return a * _sigmoid(gl), _silu(z)


def _cf_finish(cbuf, zbuf, g_ref, b_ref, store, n_rows, chunk):
    def body(r, carry):
        rs = pl.ds(pl.multiple_of(r * chunk, chunk), chunk)
        c = _silu(_layer_norm(cbuf[rs, :], g_ref[...], b_ref[...]))
        store(r, rs, (c * zbuf[rs, :]).astype(BF16))
        return carry
    lax.fori_loop(0, n_rows // chunk, body, 0, unroll=2)


def _cf_prompt_kernel(x_ref, win_ref, bin_ref, wdw_ref, bdw_ref, g_ref, b_ref, act_ref, st_ref,
                      ubuf, zbuf, cbuf, *, tm, tiles_per_seq):
    head = 32
    i = pl.program_id(0)
    first = (i % tiles_per_seq) == 0

    @pl.when(first)
    def _():
        ubuf[0:head, :] = jnp.zeros((head, D_MODEL), F32)

    @pl.when(jnp.logical_not(first))
    def _():
        ubuf[0:head, :] = ubuf[tm:tm + head, :]

    u, zs = _cf_project(x_ref[...], win_ref, bin_ref)
    ubuf[head:head + tm, :] = u
    zbuf[...] = zs
    off = head - CF_HIST
    rows, cols = 64, 256
    for rc in range(tm // rows):
        for cc in range(D_MODEL // cols):
            cs = slice(cc * cols, (cc + 1) * cols)
            acc = jnp.broadcast_to(bdw_ref[:, cs], (rows, cols))
            for j in range(CF_KERNEL):
                r0 = rc * rows + off + j
                acc = acc + wdw_ref[j:j + 1, cs] * ubuf[r0:r0 + rows, cs]
            cbuf[rc * rows:(rc + 1) * rows, cs] = acc
    def store(r, rs, val):
        act_ref[rs, :] = val
    _cf_finish(cbuf, zbuf, g_ref, b_ref, store, tm, 64)

    @pl.when((i % tiles_per_seq) == tiles_per_seq - 1)
    def _():
        st_ref[0] = ubuf[head + tm - CF_HIST:head + tm, :]


def _cf_prompt(x, w, batch, seq, tm):
    n = x.shape[0]
    tps = seq // tm
    row = lambda c: pl.BlockSpec((tm, c), lambda i: (i, 0))
    kern = functools.partial(_cf_prompt_kernel, tm=tm, tiles_per_seq=tps)
    return pl.pallas_call(
        kern, grid=(n // tm,),
        in_specs=[row(D_MODEL)] + [_const_spec(a.shape) for a in w],
        out_specs=[row(D_MODEL), pl.BlockSpec((1, CF_HIST, D_MODEL), lambda i: (i // tps, 0, 0))],
        out_shape=[jax.ShapeDtypeStruct((n, D_MODEL), BF16),
                   jax.ShapeDtypeStruct((batch, CF_HIST, D_MODEL), F32)],
        scratch_shapes=[pltpu.VMEM((tm + 32, D_MODEL), F32), pltpu.VMEM((tm, D_MODEL), F32),
                        pltpu.VMEM((tm, D_MODEL), F32)],
        compiler_params=_params(1), name="cf_prompt")(x, *w)


def _cf_sample_kernel(x_ref, hist_ref, win_ref, bin_ref, wdw_ref, bdw_ref, g_ref, b_ref, act_ref, st_ref,
                      ubuf, zbuf, cbuf, *, steps, bb):
    n = steps * bb
    u, zs = _cf_project(x_ref[...].reshape(n, D_MODEL), win_ref, bin_ref)
    zbuf[...] = zs
    ubuf[0:CF_HIST] = hist_ref[...]
    for t in range(steps):
        ubuf[CF_HIST + t] = u[t * bb:(t + 1) * bb, :]
    cols = 512
    for t in range(steps):
        for cc in range(D_MODEL // cols):
            cs = slice(cc * cols, (cc + 1) * cols)
            acc = jnp.broadcast_to(bdw_ref[:, cs], (bb, cols))
            for j in range(CF_KERNEL):
                acc = acc + wdw_ref[j:j + 1, cs] * ubuf[t + j, :, cs]
            cbuf[t * bb:(t + 1) * bb, cs] = acc
    def store(r, rs, val):
        act_ref[r] = val
    _cf_finish(cbuf, zbuf, g_ref, b_ref, store, n, bb)
    st_ref[...] = ubuf[steps:steps + CF_HIST]


def _cf_sample(x_tm, hist_tm, w, bb):
    steps, batch, _ = x_tm.shape
    kern = functools.partial(_cf_sample_kernel, steps=steps, bb=bb)
    blk = lambda t, c: pl.BlockSpec((t, bb, c), lambda i: (0, i, 0))
    return pl.pallas_call(
        kern, grid=(batch // bb,),
        in_specs=[blk(steps, D_MODEL), blk(CF_HIST, D_MODEL)] + [_const_spec(a.shape) for a in w],
        out_specs=[blk(steps, D_MODEL), blk(CF_HIST, D_MODEL)],
        out_shape=[jax.ShapeDtypeStruct((steps, batch, D_MODEL), BF16),
                   jax.ShapeDtypeStruct((CF_HIST, batch, D_MODEL), F32)],
        scratch_shapes=[pltpu.VMEM((CF_HIST + steps, bb, D_MODEL), F32),
                        pltpu.VMEM((steps * bb, D_MODEL), F32), pltpu.VMEM((steps * bb, D_MODEL), F32)],
        compiler_params=_params(1), name="cf_sample")(x_tm, hist_tm, *w)


def _mb_project(x2d, wz_ref, wxbc_ref, wdt_ref, dtb_ref, store_z, store_dt, store_xbc):
    xb = x2d.astype(BF16)
    store_z(jnp.dot(xb, wz_ref[...], preferred_element_type=F32))
    store_dt(_softplus(jnp.dot(xb, wdt_ref[...], preferred_element_type=F32) + dtb_ref[...]))
    cols = 1024
    for cc in range(MB_CONV_DIM // cols):
        cs = slice(cc * cols, (cc + 1) * cols)
        store_xbc(cs, jnp.dot(xb, wxbc_ref[:, cs], preferred_element_type=F32))


def _mb_in_prompt_kernel(x_ref, wz_ref, wxbc_ref, wdt_ref, dtb_ref, cw_ref, cb_ref,
                         z_ref, xbc_ref, dt_ref, st_ref, xbuf, *, tm, tiles_per_seq):
    head = 8
    i = pl.program_id(0)
    first = (i % tiles_per_seq) == 0

    @pl.when(first)
    def _():
        xbuf[0:head, :] = jnp.zeros((head, MB_CONV_DIM), F32)

    @pl.when(jnp.logical_not(first))
    def _():
        xbuf[0:head, :] = xbuf[tm:tm + head, :]

    def store_xbc(cs, val):
        xbuf[head:head + tm, cs] = val
    def store_z(val):
        z_ref[...] = val

    def store_dt(val):
        dt_ref[...] = val
    _mb_project(x_ref[...], wz_ref, wxbc_ref, wdt_ref, dtb_ref, store_z, store_dt, store_xbc)
    off = head - (MB_CONV - 1)
    rows, cols = 32, 512
    for rc in range(tm // rows):
        for cc in range(MB_CONV_DIM // cols):
            cs = slice(cc * cols, (cc + 1) * cols)
            acc = jnp.broadcast_to(cb_ref[:, cs], (rows, cols))
            for j in range(MB_CONV):
                r0 = rc * rows + off + j
                acc = acc + cw_ref[j:j + 1, cs] * xbuf[r0:r0 + rows, cs]
            xbc_ref[rc * rows:(rc + 1) * rows, cs] = _silu(acc)

    @pl.when((i % tiles_per_seq) == tiles_per_seq - 1)
    def _():
        st_ref[0] = xbuf[head + tm - (MB_CONV - 1):head + tm, :]


def _mb_in_prompt(x, w, batch, seq, tm):
    n = x.shape[0]
    tps = seq // tm
    row = lambda c: pl.BlockSpec((tm, c), lambda i: (i, 0))
    kern = functools.partial(_mb_in_prompt_kernel, tm=tm, tiles_per_seq=tps)
    return pl.pallas_call(
        kern, grid=(n // tm,),
        in_specs=[row(D_MODEL)] + [_const_spec(a.shape) for a in w],
        out_specs=[row(MB_INNER), row(MB_CONV_DIM), row(LANES),
                   pl.BlockSpec((1, MB_CONV - 1, MB_CONV_DIM), lambda i: (i // tps, 0, 0))],
        out_shape=[jax.ShapeDtypeStruct((n, MB_INNER), F32), jax.ShapeDtypeStruct((n, MB_CONV_DIM), F32),
                   jax.ShapeDtypeStruct((n, LANES), F32),
                   jax.ShapeDtypeStruct((batch, MB_CONV - 1, MB_CONV_DIM), F32)],
        scratch_shapes=[pltpu.VMEM((tm + 8, MB_CONV_DIM), F32)],
        compiler_params=_params(1), name="mb_in_prompt")(x, *w)


def _mb_in_sample_kernel(x_ref, hist_ref, wz_ref, wxbc_ref, wdt_ref, dtb_ref, cw_ref, cb_ref,
                         z_ref, xbc_ref, dt_ref, st_ref, xbuf, *, steps, bb):
    n = steps * bb
    hist = MB_CONV - 1
    xbuf[0:hist] = hist_ref[...]

    def store_xbc(cs, val):
        for t in range(steps):
            xbuf[hist + t, :, cs] = val[t * bb:(t + 1) * bb, :]
    def store_z(val):
        for t in range(steps):
            z_ref[t] = val[t * bb:(t + 1) * bb, :]

    def store_dt(val):
        for t in range(steps):
            dt_ref[t] = val[t * bb:(t + 1) * bb, :]
    _mb_project(x_ref[...].reshape(n, D_MODEL), wz_ref, wxbc_ref, wdt_ref, dtb_ref,
                store_z, store_dt, store_xbc)
    cols = 512
    for t in range(steps):
        for cc in range(MB_CONV_DIM // cols):
            cs = slice(cc * cols, (cc + 1) * cols)
            acc = jnp.broadcast_to(cb_ref[:, cs], (bb, cols))
            for j in range(MB_CONV):
                acc = acc + cw_ref[j:j + 1, cs] * xbuf[t + j, :, cs]
            xbc_ref[t, :, cs] = _silu(acc)
    st_ref[...] = xbuf[steps:steps + hist]


def _mb_in_sample(x_tm, hist_tm, w, bb):
    steps, batch, _ = x_tm.shape
    hist = MB_CONV - 1
    kern = functools.partial(_mb_in_sample_kernel, steps=steps, bb=bb)
    blk = lambda t, c: pl.BlockSpec((t, bb, c), lambda i: (0, i, 0))
    return pl.pallas_call(
        kern, grid=(batch // bb,),
        in_specs=[blk(steps, D_MODEL), blk(hist, MB_CONV_DIM)] + [_const_spec(a.shape) for a in w],
        out_specs=[blk(steps, MB_INNER), blk(steps, MB_CONV_DIM), blk(steps, LANES), blk(hist, MB_CONV_DIM)],
        out_shape=[jax.ShapeDtypeStruct((steps, batch, MB_INNER), F32),
                   jax.ShapeDtypeStruct((steps, batch, MB_CONV_DIM), F32),
                   jax.ShapeDtypeStruct((steps, batch, LANES), F32),
                   jax.ShapeDtypeStruct((hist, batch, MB_CONV_DIM), F32)],
        scratch_shapes=[pltpu.VMEM((hist + steps, bb, MB_CONV_DIM), F32)],
        compiler_params=_params(1), name="mb_in_sample")(x_tm, hist_tm, *w)


def _ssd_kernel(*refs, q, valid, n_chunks, has_s0):
    if has_s0:
        (xbc_x, xbc_b, xbc_c, dt_ref, z_ref, alog_ref, dskip_ref, ng_ref, s0_ref,
         act_ref, sfin_ref, s_ref, ybuf) = refs
    else:
        (xbc_x, xbc_b, xbc_c, dt_ref, z_ref, alog_ref, dskip_ref, ng_ref,
         act_ref, sfin_ref, s_ref, ybuf) = refs
    c = pl.program_id(1)
    r_per_g = MB_HEADS // MB_GROUPS
    gw = r_per_g * MB_HEAD_DIM

    @pl.when(c == 0)
    def _():
        if has_s0:
            for g in range(MB_GROUPS):
                s_ref[g] = s0_ref[0, g * r_per_g:(g + 1) * r_per_g].reshape(gw, MB_STATE).T
        else:
            s_ref[...] = jnp.zeros(s_ref.shape, F32)

    dt = dt_ref[...]
    if valid < q:
        dt = jnp.where(lax.broadcasted_iota(jnp.int32, (q, LANES), 0) < valid, dt, 0.0)
    a = dt * (-jnp.exp(alog_ref[...]))
    rid = lax.broadcasted_iota(jnp.int32, (q, q), 0)
    cid = lax.broadcasted_iota(jnp.int32, (q, q), 1)
    tri = rid >= cid
    lower = jnp.where(tri, 1.0, 0.0)
    upper = jnp.where(rid <= cid, 1.0, 0.0)
    eye = jnp.where(rid == cid, 1.0, 0.0)
    hi = lax.Precision.HIGHEST
    cum = jnp.dot(lower, a, precision=hi, preferred_element_type=F32)
    tn = (((0,), (0,)), ((), ()))
    cum_t = lax.dot_general(a, upper, tn, precision=hi, preferred_element_type=F32)
    dt_t = lax.dot_general(dt, eye, tn, precision=hi, preferred_element_type=F32)

    def hi_lo(v):
        v_hi = v.astype(BF16)
        return jnp.concatenate([v_hi, (v - v_hi.astype(F32)).astype(BF16)], axis=1)
    ecum_hl = hi_lo(jnp.exp(cum))
    wend_hl = hi_lo(jnp.exp(cum[q - 1:q, :] - cum) * dt)
    src_head = lax.broadcasted_iota(jnp.int32, (2 * LANES, gw), 0) & (LANES - 1)
    dst_head = lax.broadcasted_iota(jnp.int32, (2 * LANES, gw), 1) // MB_HEAD_DIM
    lane_head = lax.broadcasted_iota(jnp.int32, (1, gw), 1) // MB_HEAD_DIM
    for g in range(MB_GROUPS):
        gs = slice(g * MB_STATE, (g + 1) * MB_STATE)
        xs = slice(g * gw, (g + 1) * gw)
        cg = xbc_c[:, gs].astype(BF16)
        bg = xbc_b[:, gs].astype(BF16)
        cb = _bdot_nt(cg, bg)
        spread = jnp.where(src_head == g * r_per_g + dst_head, 1.0, 0.0).astype(BF16)
        ecum_g = jnp.dot(ecum_hl, spread, preferred_element_type=F32)
        wend_g = jnp.dot(wend_hl, spread, preferred_element_type=F32)
        xg = xbc_x[:, xs]
        st = s_ref[g]
        y = _bdot(cg, st) * ecum_g + dskip_ref[:, xs] * xg
        for r in range(r_per_g):
            h = g * r_per_g + r
            decay = jnp.exp(jnp.where(tri, cum[:, h:h + 1] - cum_t[h:h + 1, :], -jnp.inf))
            m = cb * decay * dt_t[h:h + 1, :]
            y = y + _bdot(m, jnp.where(lane_head == r, xg, 0.0))
        ybuf[:, xs] = y
        s_ref[g] = ecum_g[q - 1:q, :] * st + _bdot_tn(bg, xg * wend_g)
    y = ybuf[...] * _silu(z_ref[...])
    y = y * lax.rsqrt(jnp.mean(y * y, axis=-1, keepdims=True) + LN_EPS) * ng_ref[...]
    act_ref[...] = y.astype(act_ref.dtype)

    @pl.when(c == n_chunks - 1)
    def _():
        for g in range(MB_GROUPS):
            sfin_ref[0, g * r_per_g:(g + 1) * r_per_g] = s_ref[g].T.reshape(r_per_g, MB_HEAD_DIM, MB_STATE)


def _ssd(xbc, dt, z, alog, dskip, ng, s0, batch, q, valid, n_chunks, act_dtype):
    n = xbc.shape[0]
    has_s0 = s0 is not None
    rowblk = lambda cols, cb: pl.BlockSpec((q, cols), lambda b, c: (b * n_chunks + c, cb))
    st_spec = pl.BlockSpec((1, MB_HEADS, MB_HEAD_DIM, MB_STATE), lambda b, c: (b, 0, 0, 0))
    in_specs = [rowblk(MB_INNER, 0), rowblk(MB_GROUPS * MB_STATE, 2), rowblk(MB_GROUPS * MB_STATE, 3),
                rowblk(LANES, 0), rowblk(MB_INNER, 0),
                _const_spec(alog.shape), _const_spec(dskip.shape), _const_spec(ng.shape)]
    args = [xbc, xbc, xbc, dt, z, alog, dskip, ng]
    if has_s0:
        in_specs.append(st_spec)
        args.append(s0)
    kern = functools.partial(_ssd_kernel, q=q, valid=valid, n_chunks=n_chunks, has_s0=has_s0)
    return pl.pallas_call(
        kern, grid=(batch, n_chunks), in_specs=in_specs,
        out_specs=[rowblk(MB_INNER, 0), st_spec],
        out_shape=[jax.ShapeDtypeStruct((n, MB_INNER), act_dtype),
                   jax.ShapeDtypeStruct((batch, MB_HEADS, MB_HEAD_DIM, MB_STATE), F32)],
        scratch_shapes=[pltpu.VMEM((MB_GROUPS, MB_STATE, MB_INNER // MB_GROUPS), F32),
                        pltpu.VMEM((q, MB_INNER), F32)],
        compiler_params=_params(2), name="ssd")(*args)


def _at_in_kernel(x_ref, wq_ref, wk_ref, wv_ref, wqi_ref, wki_ref, wwi_ref, wz_ref, kg_ref, kb_ref,
                  q_ref, k_ref, v_ref, qi_ref, ki_ref, wi_ref, z_ref):
    xb = x_ref[...].astype(BF16)
    dot = lambda w: jnp.dot(xb, w[...], preferred_element_type=F32)
    q_ref[...] = dot(wq_ref) * (AT_HEAD_DIM ** -0.5)
    k_ref[...] = dot(wk_ref)
    v_ref[...] = dot(wv_ref)
    qi_ref[...] = dot(wqi_ref) * (IDX_DIM ** -0.5)
    ki_ref[...] = _layer_norm(dot(wki_ref), kg_ref[...], kb_ref[...])
    wi_ref[...] = dot(wwi_ref) * (IDX_HEADS ** -0.5)
    z_ref[...] = _silu(dot(wz_ref))


def _at_in(x, w, tm):
    n = x.shape[0]
    row = lambda c: pl.BlockSpec((tm, c), lambda i: (i, 0))
    widths = [D_MODEL, AT_KV_DIM, AT_KV_DIM, IDX_HEADS * IDX_DIM, IDX_DIM, LANES, D_MODEL]
    return pl.pallas_call(
        _at_in_kernel, grid=(n // tm,),
        in_specs=[row(D_MODEL)] + [_const_spec(a.shape) for a in w],
        out_specs=[row(c) for c in widths],
        out_shape=[jax.ShapeDtypeStruct((n, c), F32) for c in widths],
        compiler_params=_params(1), name="at_in")(x, *w)


ALIBI_COLS = 6


def _split3_bf16(x):
    hi = x.astype(jnp.bfloat16).astype(np.float32)
    mid = (x - hi).astype(jnp.bfloat16).astype(np.float32)
    lo = (x - hi - mid).astype(jnp.bfloat16).astype(np.float32)
    return hi, mid, lo


def _alibi_query_columns():
    s = np.asarray(ALIBI_SLOPES, np.float32) * np.float32(np.log2(np.e))
    parts = _split3_bf16(s)
    out = np.zeros((AT_HEADS, LANES), np.float32)
    for i, part in enumerate(parts):
        out[:, AT_HEAD_DIM + i] = 16.0 * part
        out[:, AT_HEAD_DIM + 3 + i] = part
    return out.reshape(1, AT_HEADS * LANES)


def _slot_weights(w, n_slots):
    d = w.shape[0]
    w = w.reshape(d, n_slots, AT_HEAD_DIM)
    return jnp.pad(w, ((0, 0), (0, 0), (0, LANES - AT_HEAD_DIM))).reshape(d, n_slots * LANES)


def _at_in_prompt_kernel(x_ref, wqs_ref, wk_ref, wv_ref, wks_ref, wvs_ref, wqi_ref, wki_ref, wwi_ref, wz_ref,
                         kg_ref, kb_ref, qcol_ref,
                         k_ref, v_ref, qi_ref, ki_ref, wi_ref, z_ref, qaug_ref, kaug_ref, vaug_ref,
                         *, tm, tiles_per_seq):
    xb = x_ref[...].astype(BF16)
    dot = lambda w: jnp.dot(xb, w[...], preferred_element_type=F32)
    k_ref[...] = dot(wk_ref)
    v_ref[...] = dot(wv_ref)
    qi_ref[...] = dot(wqi_ref) * (IDX_DIM ** -0.5)
    ki_ref[...] = _layer_norm(dot(wki_ref), kg_ref[...], kb_ref[...])
    wi_ref[...] = dot(wwi_ref) * (IDX_HEADS ** -0.5)
    z_ref[...] = _silu(dot(wz_ref))
    qaug_ref[...] = (dot(wqs_ref) * (AT_HEAD_DIM ** -0.5 * float(np.log2(np.e))) + qcol_ref[...]).astype(BF16)
    pos = (pl.program_id(0) % tiles_per_seq) * tm + lax.broadcasted_iota(jnp.int32, (tm, 1), 0)
    a = (pos >> 4).astype(F32)
    c = (pos & 15).astype(F32)
    col = lax.broadcasted_iota(jnp.int32, (1, AT_KV_HEADS * LANES), 1) & (LANES - 1)
    in_a = (col >= AT_HEAD_DIM) & (col < AT_HEAD_DIM + 3)
    in_c = (col >= AT_HEAD_DIM + 3) & (col < AT_HEAD_DIM + ALIBI_COLS)
    kaug_ref[...] = (dot(wks_ref) + jnp.where(in_a, a, jnp.where(in_c, c, 0.0))).astype(BF16)
    vaug_ref[...] = (dot(wvs_ref) + jnp.where(col == AT_HEAD_DIM, 1.0, 0.0)).astype(BF16)


def _at_in_prompt(x, w, seq, tm):
    n = x.shape[0]
    row = lambda c: pl.BlockSpec((tm, c), lambda i: (i, 0))
    outs = [(AT_KV_DIM, F32), (AT_KV_DIM, F32), (IDX_HEADS * IDX_DIM, F32), (IDX_DIM, F32), (LANES, F32),
            (D_MODEL, F32), (AT_HEADS * LANES, BF16), (AT_KV_HEADS * LANES, BF16), (AT_KV_HEADS * LANES, BF16)]
    kern = functools.partial(_at_in_prompt_kernel, tm=tm, tiles_per_seq=seq // tm)
    return pl.pallas_call(
        kern, grid=(n // tm,),
        in_specs=[row(D_MODEL)] + [_const_spec(a.shape) for a in w],
        out_specs=[row(c) for c, _ in outs],
        out_shape=[jax.ShapeDtypeStruct((n, c), dt) for c, dt in outs],
        compiler_params=_params(1), name="at_in_prompt")(x, *w)


def _page_dmas(pt_ref, seq0, n_seq, n_pages, src_hbm, dst_of, sem):
    out = []
    for g in range(n_seq):
        for p in range(n_pages):
            out.append(pltpu.make_async_copy(src_hbm.at[pt_ref[(seq0 + g) * n_pages + p]], dst_of(g, p), sem))
    return out


def _prefetch_step(dmas):
    i = pl.program_id(0)
    slot = i % 2

    @pl.when(i == 0)
    def _():
        for c in dmas(0, 0):
            c.start()

    @pl.when(i + 1 < pl.num_programs(0))
    def _():
        for c in dmas(i + 1, 1 - slot):
            c.start()
    for c in dmas(i, slot):
        c.wait()
    return slot


def _dsa_select_sample_kernel(pt_ref, qi_ref, wi_ref, kin_ref, kidx_hbm, bias_ref, kibuf, sem, idxbuf, keybuf,
                              *, g_seq, n_pages, page, steps, topk):
    past = n_pages * page
    s_pad = past + LANES
    n_tiles = s_pad // LANES
    rows = g_seq * 8
    slot = _prefetch_step(lambda step, sl: _page_dmas(
        pt_ref, step * g_seq, g_seq, n_pages, kidx_hbm,
        lambda g, p: kibuf.at[sl, g, :, pl.ds(p * page, page)], sem.at[sl]))

    idxbuf[...] = jnp.full((rows, s_pad), -jnp.inf, F32)
    tt = lax.broadcasted_iota(jnp.int32, (steps, 8), 0)
    jj = lax.broadcasted_iota(jnp.int32, (steps, 8), 1)
    for g in range(g_seq):
        qs = qi_ref[g].astype(BF16)
        w = wi_ref[g]
        sc = _bdot(qs, kibuf[slot, g])
        val = jnp.maximum(sc, 0.0) * jnp.concatenate([w] * (past // LANES), axis=1)
        idxbuf[g * 8:g * 8 + steps, 0:past] = jnp.sum(val.reshape(steps, IDX_HEADS, past), axis=1)
        scn = _bdot_nt(qs, kin_ref[g])
        valn = jnp.maximum(scn, 0.0) * w[:, 0:8]
        idn = jnp.sum(valn.reshape(steps, IDX_HEADS, 8), axis=1)
        idxbuf[g * 8:g * 8 + steps, past:past + 8] = jnp.where(jj <= tt, idn, -jnp.inf)

    bits = pltpu.bitcast(idxbuf[...], jnp.int32)
    keybuf[...] = jnp.where(bits < 0, bits ^ jnp.int32(0x7FFFFFFF), bits)

    def tile(c):
        return keybuf[:, c * LANES:(c + 1) * LANES]

    def bit_pass(b, carry):
        thr, cnt_gt = carry
        cand = thr + jnp.left_shift(jnp.int32(1), 31 - b)
        part = jnp.zeros((rows, LANES), F32)
        for c in range(n_tiles):
            part = part + jnp.where(tile(c) >= cand, 1.0, 0.0)
        total = jnp.sum(part, axis=1, keepdims=True)
        ok = total >= float(topk)
        return jnp.where(ok, cand, thr), jnp.where(ok, cnt_gt, total)
    thr, cnt_gt = lax.fori_loop(0, 32, bit_pass, (jnp.full((rows, 1), INT_MIN, jnp.int32),
                                                  jnp.zeros((rows, 1), F32)))
    need_eq = float(topk) - cnt_gt

    incl = jnp.where(lax.broadcasted_iota(jnp.int32, (LANES, LANES), 0)
                     <= lax.broadcasted_iota(jnp.int32, (LANES, LANES), 1), 1.0, 0.0).astype(BF16)
    eq_rows = jnp.concatenate([jnp.where(tile(c) == thr, 1.0, 0.0).astype(BF16) for c in range(n_tiles)], axis=0)
    pre = jnp.dot(eq_rows, incl, preferred_element_type=F32)
    run = jnp.zeros((rows, 1), F32)
    for c in range(n_tiles):
        key = tile(c)
        pc = pre[c * rows:(c + 1) * rows] + run
        sel = (key > thr) | ((key == thr) & (pc <= need_eq))
        idxbuf[:, c * LANES:(c + 1) * LANES] = jnp.where(sel, 0.0, NEG_BIG)
        run = pc[:, LANES - 1:LANES]
    for g in range(g_seq):
        bias_ref[g] = idxbuf[g * 8:(g + 1) * 8, :]


def _dsa_select_sample(page_table, qi_st, wi_col, ki_new, cache_ki, topk, g_seq):
    bs, n_pages = page_table.shape
    page = cache_ki.shape[2]
    past = n_pages * page
    s_pad = past + LANES
    rows_q = qi_st.shape[1]
    blk = lambda r, c: pl.BlockSpec((g_seq, r, c), lambda i, pt: (i, 0, 0))
    kern = functools.partial(_dsa_select_sample_kernel, g_seq=g_seq, n_pages=n_pages, page=page,
                             steps=rows_q // IDX_HEADS, topk=topk)
    return pl.pallas_call(
        kern,
        grid_spec=pltpu.PrefetchScalarGridSpec(
            num_scalar_prefetch=1, grid=(bs // g_seq,),
            in_specs=[blk(rows_q, IDX_DIM), blk(rows_q, LANES), blk(8, IDX_DIM), pl.BlockSpec(memory_space=pl.ANY)],
            out_specs=blk(8, s_pad),
            scratch_shapes=[pltpu.VMEM((2, g_seq, IDX_DIM, past), F32), pltpu.SemaphoreType.DMA((2,)),
                            pltpu.VMEM((g_seq * 8, s_pad), F32), pltpu.VMEM((g_seq * 8, s_pad), jnp.int32)]),
        out_shape=jax.ShapeDtypeStruct((bs, 8, s_pad), F32),
        compiler_params=_params(1), name="dsa_select_sample")(page_table.reshape(-1), qi_st, wi_col, ki_new, cache_ki)


def _dsa_attend_sample_kernel(pt_ref, qbd_ref, zbd_ref, bias_ref, knew_ref, vnew_ref, meta_ref, k_hbm, v_hbm,
                              out_ref, kbuf, vbuf, sem, *, n_pages, page, steps):
    past = n_pages * page

    def dmas(step, sl):
        return (_page_dmas(pt_ref, step, 1, n_pages, k_hbm, lambda g, p: kbuf.at[sl, :, pl.ds(p * page, page)],
                           sem.at[0, sl])
                + _page_dmas(pt_ref, step, 1, n_pages, v_hbm, lambda g, p: vbuf.at[sl, :, pl.ds(p * page, page)],
                             sem.at[1, sl]))
    slot = _prefetch_step(dmas)

    q = qbd_ref[0].astype(BF16)
    slope = meta_ref[:, 0:1]
    tq = meta_ref[:, 1:2]
    bias = jnp.concatenate([jnp.broadcast_to(bias_ref[0, t:t + 1, :], (AT_HEADS, past + LANES))
                            for t in range(steps)], axis=0)
    pos_old = lax.broadcasted_iota(jnp.int32, (1, past), 1).astype(F32)
    l_old = _bdot(q, kbuf[slot]) - slope * ((float(past) + tq) - pos_old) + bias[:, 0:past]
    pos_new = lax.broadcasted_iota(jnp.int32, (1, 8), 1).astype(F32)
    l_new = _bdot_nt(q, knew_ref[0]) - slope * (tq - pos_new) + bias[:, past:past + 8]
    m = jnp.maximum(jnp.max(l_old, axis=1, keepdims=True), jnp.max(l_new, axis=1, keepdims=True))
    p_old = jnp.exp(l_old - m)
    p_new = jnp.exp(l_new - m)
    den = jnp.sum(p_old, axis=1, keepdims=True) + jnp.sum(p_new, axis=1, keepdims=True)
    o = _bdot_nt(p_old, vbuf[slot]) + _bdot(p_new, vnew_ref[0])
    out_ref[0] = o / den * zbd_ref[0]


def _dsa_attend_sample(page_table, qbd, zbd, bias, k_new, v_new, meta, cache_k, cache_v):
    bs, n_pages = page_table.shape
    page = cache_k.shape[2]
    past = n_pages * page
    rows = qbd.shape[1]
    seq = lambda r, c: pl.BlockSpec((1, r, c), lambda i, pt: (i, 0, 0))
    hbm = pl.BlockSpec(memory_space=pl.ANY)
    kern = functools.partial(_dsa_attend_sample_kernel, n_pages=n_pages, page=page, steps=rows // AT_HEADS)
    return pl.pallas_call(
        kern,
        grid_spec=pltpu.PrefetchScalarGridSpec(
            num_scalar_prefetch=1, grid=(bs,),
            in_specs=[seq(rows, AT_KV_DIM), seq(rows, AT_KV_DIM), seq(8, past + LANES), seq(8, AT_KV_DIM),
                      seq(8, AT_KV_DIM), pl.BlockSpec(meta.shape, lambda i, pt: (0, 0)), hbm, hbm],
            out_specs=seq(rows, AT_KV_DIM),
            scratch_shapes=[pltpu.VMEM((2, AT_KV_DIM, past), F32), pltpu.VMEM((2, AT_KV_DIM, past), F32),
                            pltpu.SemaphoreType.DMA((2, 2))]),
        out_shape=jax.ShapeDtypeStruct((bs, rows, AT_KV_DIM), F32),
        compiler_params=_params(1), name="dsa_attend_sample")(
            page_table.reshape(-1), qbd, zbd, bias, k_new, v_new, meta, cache_k, cache_v)


def _dsa_kernel(qaug_ref, qi_ref, wi_ref, z_ref, kaug_ref, vaug_ref, ki_ref, act_ref,
                keys_ref, bias_ref, mrun_ref, m_ref, acc_ref, p_ref, *, t, topk):
    kb_sz = KEY_BLOCK
    pos0 = pl.program_id(1) * t
    n_kb = (pos0 + t - 1) // kb_sz + 1
    rowpos = pos0 + lax.broadcasted_iota(jnp.int32, (t, 1), 0)
    lane = lax.broadcasted_iota(jnp.int32, (1, kb_sz), 1)

    def score_block(kb, carry):
        ks = pl.ds(pl.multiple_of(kb * kb_sz, kb_sz), kb_sz)
        kib = ki_ref[ks, :].astype(BF16)
        acc = jnp.zeros((t, kb_sz), F32)
        for i in range(IDX_HEADS):
            sc = _bdot_nt(qi_ref[:, i * IDX_DIM:(i + 1) * IDX_DIM], kib)
            acc = acc + jnp.maximum(sc, 0.0) * wi_ref[:, i:i + 1]
        acc = jnp.where(kb * kb_sz + lane <= rowpos, acc, -jnp.inf)
        bits = pltpu.bitcast(acc, jnp.int32)
        keys_ref[kb] = jnp.where(bits < 0, bits ^ jnp.int32(0x7FFFFFFF), bits)
        return carry
    lax.fori_loop(0, n_kb, score_block, 0)

    def bit_pass(b, carry):
        thr, cnt_gt = carry
        cand = thr + jnp.left_shift(jnp.int32(1), 31 - b)

        def body(kb, cnt):
            hit = jnp.where(keys_ref[kb] >= cand, 1.0, 0.0)
            for s in range(kb_sz // LANES):
                cnt = cnt + hit[:, s * LANES:(s + 1) * LANES]
            return cnt
        cnt = lax.fori_loop(0, n_kb, body, jnp.zeros((t, LANES), F32))
        total = jnp.sum(cnt, axis=1, keepdims=True)
        ok = total >= float(topk)
        return jnp.where(ok, cand, thr), jnp.where(ok, cnt_gt, total)
    thr, cnt_gt = lax.fori_loop(0, 32, bit_pass, (jnp.full((t, 1), INT_MIN, jnp.int32), jnp.zeros((t, 1), F32)))
    need_eq = float(topk) - cnt_gt

    mrun_ref[...] = jnp.full(mrun_ref.shape, NEG_BIG, F32)
    incl = jnp.where(lax.broadcasted_iota(jnp.int32, (kb_sz, kb_sz), 0)
                     <= lax.broadcasted_iota(jnp.int32, (kb_sz, kb_sz), 1), 1.0, 0.0).astype(BF16)
    r_per_g = AT_HEADS // AT_KV_HEADS
    nt = (((1,), (1,)), ((), ()))

    def logits(h, kg, mask_bias):
        return lax.dot_general(qaug_ref[:, h * LANES:(h + 1) * LANES], kg, nt,
                               preferred_element_type=F32) + mask_bias

    def max_block(kb, run_eq):
        ks = pl.ds(pl.multiple_of(kb * kb_sz, kb_sz), kb_sz)
        key = keys_ref[kb]
        eq = key == thr
        pre = jnp.dot(jnp.where(eq, 1.0, 0.0).astype(BF16), incl, preferred_element_type=F32) + run_eq
        sel = ((key > thr) | (eq & (pre <= need_eq))) & (kb * kb_sz + lane <= rowpos)
        mask_bias = jnp.where(sel, 0.0, NEG_BIG)
        bias_ref[kb] = mask_bias
        for g in range(AT_KV_HEADS):
            kg = kaug_ref[ks, g * LANES:(g + 1) * LANES]
            for r in range(r_per_g):
                h = g * r_per_g + r
                s = logits(h, kg, mask_bias)
                tile_max = s[:, 0:LANES]
                for c in range(1, kb_sz // LANES):
                    tile_max = jnp.maximum(tile_max, s[:, c * LANES:(c + 1) * LANES])
                mrun_ref[h] = jnp.maximum(mrun_ref[h], tile_max)
        return pre[:, kb_sz - 1:kb_sz]
    lax.fori_loop(0, n_kb, max_block, jnp.zeros((t, 1), F32))
    for h in range(AT_HEADS):
        m_ref[h] = jnp.max(mrun_ref[h], axis=1, keepdims=True)

    acc_ref[...] = jnp.zeros(acc_ref.shape, F32)

    def attend_block(kb, carry):
        ks = pl.ds(pl.multiple_of(kb * kb_sz, kb_sz), kb_sz)
        mask_bias = bias_ref[kb]
        for g in range(AT_KV_HEADS):
            kg = kaug_ref[ks, g * LANES:(g + 1) * LANES]
            for r in range(r_per_g):
                h = g * r_per_g + r
                p_ref[h] = jnp.exp2(logits(h, kg, mask_bias) - m_ref[h]).astype(BF16)
        for g in range(AT_KV_HEADS):
            vg = vaug_ref[ks, g * LANES:(g + 1) * LANES]
            for r in range(r_per_g):
                h = g * r_per_g + r
                acc_ref[h] = acc_ref[h] + jnp.dot(p_ref[h], vg, preferred_element_type=F32)
        return carry
    lax.fori_loop(0, n_kb, attend_block, 0)

    for h in range(AT_HEADS):
        hs = slice(h * AT_HEAD_DIM, (h + 1) * AT_HEAD_DIM)
        a = acc_ref[h]
        o = a[:, 0:AT_HEAD_DIM] / a[:, AT_HEAD_DIM:AT_HEAD_DIM + 1]
        act_ref[:, hs] = (o * z_ref[:, hs]).astype(act_ref.dtype)


def _dsa(qaug, qi, wi, z, kaug, vaug, ki_all, batch, n_qb, t, s_len, topk):
    n = qaug.shape[0]
    qrow = lambda c: pl.BlockSpec((t, c), lambda b, j: (b * n_qb + j, 0))
    krow = lambda c: pl.BlockSpec((s_len, c), lambda b, j: (b, 0))
    n_kb = s_len // KEY_BLOCK
    kern = functools.partial(_dsa_kernel, t=t, topk=topk)
    return pl.pallas_call(
        kern, grid=(batch, n_qb),
        in_specs=[qrow(AT_HEADS * LANES), qrow(IDX_HEADS * IDX_DIM), qrow(LANES), qrow(D_MODEL),
                  krow(AT_KV_HEADS * LANES), krow(AT_KV_HEADS * LANES), krow(IDX_DIM)],
        out_specs=qrow(D_MODEL), out_shape=jax.ShapeDtypeStruct((n, D_MODEL), BF16),
        scratch_shapes=[pltpu.VMEM((n_kb, t, KEY_BLOCK), jnp.int32), pltpu.VMEM((n_kb, t, KEY_BLOCK), F32),
                        pltpu.VMEM((AT_HEADS, t, LANES), F32), pltpu.VMEM((AT_HEADS, t, 1), F32),
                        pltpu.VMEM((AT_HEADS, t, LANES), F32), pltpu.VMEM((AT_HEADS, t, KEY_BLOCK), BF16)],
        compiler_params=_params(2), name="dsa")(qaug, qi, wi, z, kaug, vaug, ki_all)


def _gm_kernel(*refs, tm, chunk, emit_v):
    if emit_v:
        (x_ref, win_ref, bin_ref, g_ref, b_ref, wmix_ref, bmix_ref, act_ref, v_ref, ubuf, vbuf, zbuf) = refs
    else:
        (x_ref, win_ref, bin_ref, g_ref, b_ref, wmix_ref, bmix_ref, act_ref, ubuf, vbuf, zbuf) = refs
    xb = x_ref[...].astype(BF16)
    w = GM_WIDTH
    cols = 512
    for cc in range(w // cols):
        cs = slice(cc * cols, (cc + 1) * cols)
        proj = lambda off: (jnp.dot(xb, win_ref[:, off + cc * cols:off + (cc + 1) * cols],
                                    preferred_element_type=F32) + bin_ref[:, off + cc * cols:off + (cc + 1) * cols])
        ubuf[:, cs] = _gelu_tanh(proj(0))
        vbuf[:, cs] = _gelu_tanh(proj(w))
        zbuf[:, cs] = _silu(proj(2 * w))

    rows = 32

    def norm_rows(r, carry):
        rs = pl.ds(pl.multiple_of(r * rows, rows), rows)
        v = _layer_norm(vbuf[rs, :], g_ref[...], b_ref[...])
        vbuf[rs, :] = v
        if emit_v:
            v_ref[rs, :] = v
        return carry
    lax.fori_loop(0, tm // rows, norm_rows, 0)

    gw = w // GM_GROUPS
    for c in range(tm // chunk):
        rs = slice(c * chunk, (c + 1) * chunk)
        for g in range(GM_GROUPS):
            cs = slice(g * gw, (g + 1) * gw)
            s = jnp.dot(wmix_ref[g], vbuf[rs, cs].astype(BF16), preferred_element_type=F32) + bmix_ref[:, g:g + 1]
            act_ref[rs, cs] = (ubuf[rs, cs] * s * zbuf[rs, cs]).astype(act_ref.dtype)


def _gm(x, w, tm, chunk, emit_v):
    n = x.shape[0]
    row = lambda c: pl.BlockSpec((tm, c), lambda i: (i, 0))
    out_specs = [row(GM_WIDTH)]
    out_shape = [jax.ShapeDtypeStruct((n, GM_WIDTH), BF16)]
    if emit_v:
        out_specs.append(row(GM_WIDTH))
        out_shape.append(jax.ShapeDtypeStruct((n, GM_WIDTH), F32))
    kern = functools.partial(_gm_kernel, tm=tm, chunk=chunk, emit_v=emit_v)
    return pl.pallas_call(
        kern, grid=(n // tm,),
        in_specs=[row(D_MODEL)] + [_const_spec(a.shape) for a in w],
        out_specs=out_specs, out_shape=out_shape,
        scratch_shapes=[pltpu.VMEM((tm, GM_WIDTH), F32)] * 3,
        compiler_params=_params(1), name="gmlp")(x, *w)


def _pad_lanes(a, width=LANES):
    return jnp.pad(a, [(0, 0)] * (a.ndim - 1) + [(0, width - a.shape[-1])])


def kernel(x_prompt, x_sample, p_prompt, p_sample, state_cf_conv, state_mb_conv, state_mb_ssm,
           cache_k, cache_v, cache_kidx, page_table, post_ln_g, post_ln_b, ple_w, ple_gate_w,
           cf_w_in, cf_b_in, cf_w_dw, cf_b_dw, cf_ln_g, cf_ln_b, cf_w_out, cf_b_out,
           mb_w_in, mb_w_conv, mb_b_conv, mb_dt_bias, mb_a_log, mb_d_skip, mb_norm_g, mb_w_out,
           at_w_in, at_ki_ln_g, at_ki_ln_b, at_w_out,
           gm_w_in, gm_b_in, gm_ln_g, gm_ln_b, gm_w_s, gm_b_s, gm_w_out):
    bp, lp, d = x_prompt.shape
    bs, ls, _ = x_sample.shape
    row2 = lambda v: v.reshape(1, -1)
    bf = lambda v: v.astype(BF16)

    ple_wb, ple_gate_wb = bf(ple_w), bf(ple_gate_w)
    cf_w = (bf(cf_w_in), row2(cf_b_in), cf_w_dw, row2(cf_b_dw), row2(cf_ln_g), row2(cf_ln_b))
    mb_in_w = (bf(mb_w_in[:, :MB_INNER]), bf(mb_w_in[:, MB_INNER:MB_INNER + MB_CONV_DIM]),
               bf(_pad_lanes(mb_w_in[:, MB_INNER + MB_CONV_DIM:])), _pad_lanes(row2(mb_dt_bias)),
               mb_w_conv, row2(mb_b_conv))
    ssd_w = (_pad_lanes(row2(mb_a_log)), row2(jnp.repeat(mb_d_skip, MB_HEAD_DIM)), row2(mb_norm_g))
    sizes = [AT_HEADS * AT_HEAD_DIM, AT_KV_DIM, AT_KV_DIM, IDX_HEADS * IDX_DIM, IDX_DIM, IDX_HEADS,
             AT_HEADS * AT_HEAD_DIM]
    offs = np.concatenate([[0], np.cumsum(sizes)]).tolist()
    at_parts = [at_w_in[:, offs[i]:offs[i + 1]] for i in range(len(sizes))]
    at_parts[5] = _pad_lanes(at_parts[5])
    at_w = tuple(bf(a) for a in at_parts) + (row2(at_ki_ln_g), row2(at_ki_ln_b))
    wq_b, wk_b, wv_b, wqi_b, wki_b, wwi_b, wz_b = at_w[:7]
    at_w_prompt = (_slot_weights(wq_b, AT_HEADS), wk_b, wv_b, _slot_weights(wk_b, AT_KV_HEADS),
                   _slot_weights(wv_b, AT_KV_HEADS), wqi_b, wki_b, wwi_b, wz_b,
                   row2(at_ki_ln_g), row2(at_ki_ln_b), jnp.asarray(_alibi_query_columns()))
    gm_common = (bf(gm_w_in), row2(gm_b_in), row2(gm_ln_g), row2(gm_ln_b))
    zeros_d = jnp.zeros((1, d), F32)
    out_w = [(bf(cf_w_out), row2(cf_b_out)), (bf(mb_w_out), zeros_d), (bf(at_w_out), zeros_d),
             (bf(gm_w_out), zeros_d)]

    def tail(i, act, x, p, tm):
        wo, bo = out_w[i]
        return _tail(act, x, p, wo, bo, row2(post_ln_g[i]), row2(post_ln_b[i]), ple_gate_wb[i], ple_wb[i], tm)

    tm = 512
    n_p = bp * lp
    x = x_prompt.reshape(n_p, d)
    pp = p_prompt.reshape(DEPTH, n_p, D_PLE)

    act, cf_conv_p = _cf_prompt(x, cf_w, bp, lp, tm)
    x = tail(0, act, x, pp[0], tm)

    z, xbc, dt, mb_conv_p = _mb_in_prompt(x, mb_in_w, bp, lp, tm)
    act, mb_ssm_p = _ssd(xbc, dt, z, *ssd_w, None, bp, MB_CHUNK, MB_CHUNK, lp // MB_CHUNK, BF16)
    x = tail(1, act, x, pp[1], tm)

    k, v, qi, ki, wi, zs, qaug, kaug, vaug = _at_in_prompt(x, at_w_prompt, lp, tm)
    act = _dsa(qaug, qi, wi, zs, kaug, vaug, ki, bp, lp // Q_BLOCK, Q_BLOCK, lp, min(TOPK_MAX, lp // 4))
    x = tail(2, act, x, pp[2], tm)
    k_p = k.reshape(bp, lp, AT_KV_HEADS, AT_HEAD_DIM)
    v_p = v.reshape(bp, lp, AT_KV_HEADS, AT_HEAD_DIM)
    kidx_p = ki.reshape(bp, lp, IDX_DIM)

    gm_w_p = gm_common + (bf(jnp.tril(gm_w_s)), gm_b_s.T)
    (act,) = _gm(x, gm_w_p, tm, GM_CHUNK, False)
    y_prompt = tail(3, act, x, pp[3], tm).reshape(bp, lp, d)

    n_s = bs * ls
    bb = 32
    to_tm = lambda a: jnp.swapaxes(a, 0, 1)
    x_tm = to_tm(x_sample)
    x = x_tm.reshape(n_s, d)
    ps = jnp.swapaxes(p_sample, 1, 2).reshape(DEPTH, n_s, D_PLE)
    tm_s = n_s

    act, cf_st = _cf_sample(x_tm, to_tm(state_cf_conv), cf_w, bb)
    cf_conv_s = to_tm(cf_st)
    x = tail(0, act.reshape(n_s, d), x, ps[0], tm_s)

    z, xbc, dt, mb_st = _mb_in_sample(x.reshape(ls, bs, d), to_tm(state_mb_conv), mb_in_w, bb)
    mb_conv_s = to_tm(mb_st)
    qs = 8

    def to_bm(a):
        a = jnp.pad(to_tm(a), ((0, 0), (0, qs - ls), (0, 0)))
        return a.reshape(bs * qs, a.shape[-1])

    def from_bm(a):
        a = a.reshape(bs, qs, a.shape[-1])[:, :ls]
        return to_tm(a).reshape(n_s, a.shape[-1])

    act, mb_ssm_s = _ssd(to_bm(xbc), to_bm(dt), to_bm(z), *ssd_w, state_mb_ssm, bs, qs, ls, 1, F32)
    x = tail(1, from_bm(act), x, ps[1], tm_s)

    q, k, v, qi, ki, wi, zs = _at_in(x, at_w, tm_s)
    k_s = to_tm(k.reshape(ls, bs, AT_KV_HEADS, AT_HEAD_DIM))
    v_s = to_tm(v.reshape(ls, bs, AT_KV_HEADS, AT_HEAD_DIM))
    kidx_s = to_tm(ki.reshape(ls, bs, IDX_DIM))
    n_pool, page = cache_k.shape[:2]
    past = page_table.shape[1] * page
    seq_major = lambda a: to_tm(a.reshape(ls, bs, a.shape[-1]))
    r_per_g = AT_HEADS // AT_KV_HEADS
    group_of_head = (np.arange(AT_HEADS)[:, None] // r_per_g == np.arange(AT_KV_HEADS)[None, :]).astype(np.float32)

    def block_diag(a):
        a = a.reshape(bs, ls, AT_HEADS, 1, AT_HEAD_DIM) * group_of_head[None, None, :, :, None]
        return a.reshape(bs, ls * AT_HEADS, AT_KV_DIM)

    pad_steps = lambda a: jnp.pad(a, ((0, 0), (0, 8 - ls), (0, 0)))
    qi_st = seq_major(qi).reshape(bs, ls * IDX_HEADS, IDX_DIM)
    wi_col = jnp.broadcast_to(seq_major(wi)[:, :, :IDX_HEADS].reshape(bs, ls * IDX_HEADS, 1),
                              (bs, ls * IDX_HEADS, LANES))
    pages_t = lambda c: jnp.moveaxis(c, 1, -1).reshape(n_pool, -1, page)
    bias = _dsa_select_sample(page_table, qi_st, wi_col, pad_steps(seq_major(ki)), pages_t(cache_kidx),
                              min(TOPK_MAX, (past + ls) // 4), 8)
    meta = np.zeros((ls * AT_HEADS, LANES), np.float32)
    meta[:, 0] = np.tile(np.asarray(ALIBI_SLOPES, np.float32), ls)
    meta[:, 1] = np.repeat(np.arange(ls, dtype=np.float32), AT_HEADS)
    o = _dsa_attend_sample(page_table, block_diag(seq_major(q)), block_diag(seq_major(zs)), bias,
                           pad_steps(seq_major(k)), pad_steps(seq_major(v)), jnp.asarray(meta),
                           pages_t(cache_k), pages_t(cache_v))
    act = o.reshape(bs, ls, AT_HEADS, AT_KV_HEADS, AT_HEAD_DIM).sum(axis=3).reshape(bs, ls, d)
    x = tail(2, to_tm(act).reshape(n_s, d), x, ps[2], tm_s)

    mix = jnp.stack([jnp.kron(jnp.tril(gm_w_s[g, :ls, :ls]), jnp.eye(bs, dtype=F32)) for g in range(GM_GROUPS)])
    gm_w_smp = gm_common + (bf(mix), jnp.repeat(gm_b_s[:, :ls].T, bs, axis=0))
    act, gm_v = _gm(x, gm_w_smp, tm_s, n_s, True)
    y_s = tail(3, act, x, ps[3], tm_s)
    y_sample = to_tm(y_s.reshape(ls, bs, d))
    gm_v_s = to_tm(gm_v.reshape(ls, bs, GM_WIDTH))

    return (y_prompt, y_sample, cf_conv_p, cf_conv_s, mb_conv_p, mb_conv_s, mb_ssm_p, mb_ssm_s,
            k_p, v_p, kidx_p, k_s, v_s, kidx_s, gm_v_s)
```

```python
import functools

import numpy as np
import jax
import jax.numpy as jnp
from jax import lax
from jax.experimental import pallas as pl
from jax.experimental.pallas import tpu as pltpu

F32 = jnp.float32
BF16 = jnp.bfloat16

D_MODEL = 1024
D_PLE = 256
DEPTH = 4
ALPHA_DN = (2 * DEPTH) ** 0.25
LN_EPS = 1e-5

CF_KERNEL = 31
CF_HIST = CF_KERNEL - 1
MB_INNER = 2048
MB_HEAD_DIM = 64
MB_HEADS = 32
MB_GROUPS = 8
MB_STATE = 128
MB_CONV = 4
MB_CONV_DIM = 4096
MB_CHUNK = 128
AT_HEADS = 16
AT_HEAD_DIM = 64
AT_KV_HEADS = 4
AT_KV_DIM = AT_KV_HEADS * AT_HEAD_DIM
IDX_HEADS = 8
IDX_DIM = 64
TOPK_MAX = 256
DSA_ROWS = 256
GM_WIDTH = 2048
GM_GROUPS = 4
GM_CHUNK = 128

LANES = 128
KEY_BLOCK = 512
NEG_BIG = -1e30
INT_MIN = -(2 ** 31)
VMEM_LIMIT = 56 * 1024 * 1024

ALIBI_SLOPES = [float(s) for s in
                (np.float32(2.0) ** (-8.0 * np.arange(1, AT_HEADS + 1, dtype=np.float32) / AT_HEADS))]


def _bdot(a, b):
    return jnp.dot(a.astype(BF16), b.astype(BF16), preferred_element_type=F32)


def _bdot_nt(a, b):
    return lax.dot_general(a.astype(BF16), b.astype(BF16), (((1,), (1,)), ((), ())),
                           preferred_element_type=F32)


def _bdot_tn(a, b):
    return lax.dot_general(a.astype(BF16), b.astype(BF16), (((0,), (0,)), ((), ())),
                           preferred_element_type=F32)


def _sigmoid(x):
    return 1.0 / (1.0 + jnp.exp(-x))


def _silu(x):
    return x * _sigmoid(x)


def _gelu_tanh(x):
    return x * (0.5 * (1.0 + jnp.tanh(np.sqrt(2.0 / np.pi).astype(np.float32) * (x + 0.044715 * (x * x * x)))))


def _softplus(x):
    return jnp.maximum(x, 0.0) + jnp.log1p(jnp.exp(-jnp.abs(x)))


def _layer_norm(x, g, b):
    mu = jnp.mean(x, axis=-1, keepdims=True)
    xc = x - mu
    var = jnp.mean(xc * xc, axis=-1, keepdims=True)
    return xc * lax.rsqrt(var + LN_EPS) * g + b


def _const_spec(shape):
    nd = len(shape)
    return pl.BlockSpec(shape, lambda *_: (0,) * nd)


def _params(n_axes):
    return pltpu.CompilerParams(dimension_semantics=("arbitrary",) * n_axes,
                                vmem_limit_bytes=VMEM_LIMIT)


def _tail_kernel(act_ref, x_ref, p_ref, wo_ref, bo_ref, g_ref, b_ref, wg_ref, wp_ref, o_ref):
    out = _bdot(act_ref[...], wo_ref[...]) + bo_ref[...]
    h = _layer_norm(ALPHA_DN * x_ref[...] + out, g_ref[...], b_ref[...])
    gate = _sigmoid(_bdot(h, wg_ref[...]))
    o_ref[...] = h + gate * _bdot(p_ref[...], wp_ref[...])


def _tail(act, x, p, wo, bo, g, b, wg, wp, tm):
    n, k = act.shape
    row = lambda c: pl.BlockSpec((tm, c), lambda i: (i, 0))
    return pl.pallas_call(
        _tail_kernel, grid=(n // tm,),
        in_specs=[row(k), row(D_MODEL), row(D_PLE), _const_spec(wo.shape), _const_spec(bo.shape),
                  _const_spec(g.shape), _const_spec(b.shape), _const_spec(wg.shape), _const_spec(wp.shape)],
        out_specs=row(D_MODEL), out_shape=jax.ShapeDtypeStruct((n, D_MODEL), F32),
        compiler_params=_params(1), name="layer_tail")(act, x, p, wo, bo, g, b, wg, wp)


def _cf_project(x2d, win_ref, bin_ref):
    xb = x2d.astype(BF16)
    d = D_MODEL
    a = jnp.dot(xb, win_ref[:, 0:d], preferred_element_type=F32) + bin_ref[:, 0:d]
    gl = jnp.dot(xb, win_ref[:, d:2 * d], preferred_element_type=F32) + bin_ref[:, d:2 * d]
    z = jnp.dot(xb, win_ref[:, 2 * d:3 * d], preferred_element_type=F32) + bin_ref[:, 2 * d:3 * d]
    return a * _sigmoid(gl), _silu(z)


def _cf_finish(cbuf, zbuf, g_ref, b_ref, store, n_rows, chunk):
    def body(r, carry):
        rs = pl.ds(pl.multiple_of(r * chunk, chunk), chunk)
        c = _silu(_layer_norm(cbuf[rs, :], g_ref[...], b_ref[...]))
        store(r, rs, (c * zbuf[rs, :]).astype(BF16))
        return carry
    lax.fori_loop(0, n_rows // chunk, body, 0, unroll=2)


def _cf_prompt_kernel(x_ref, win_ref, bin_ref, wdw_ref, bdw_ref, g_ref, b_ref, act_ref, st_ref,
                      ubuf, zbuf, cbuf, *, tm, tiles_per_seq):
    head = 32
    i = pl.program_id(0)
    first = (i % tiles_per_seq) == 0

    @pl.when(first)
    def _():
        ubuf[0:head, :] = jnp.zeros((head, D_MODEL), F32)

    @pl.when(jnp.logical_not(first))
    def _():
        ubuf[0:head, :] = ubuf[tm:tm + head, :]

    u, zs = _cf_project(x_ref[...], win_ref, bin_ref)
    ubuf[head:head + tm, :] = u
    zbuf[...] = zs
    off = head - CF_HIST
    rows, cols = 64, 256
    for rc in range(tm // rows):
        for cc in range(D_MODEL // cols):
            cs = slice(cc * cols, (cc + 1) * cols)
            acc = jnp.broadcast_to(bdw_ref[:, cs], (rows, cols))
            for b in range(8):
                span = rows + (8 if b else 0)
                part = None
                for a in range((off + CF_KERNEL + 7) // 8):
                    j = 8 * a + b - off
                    if 0 <= j < CF_KERNEL:
                        r0 = rc * rows + 8 * a
                        term = wdw_ref[j:j + 1, cs] * ubuf[r0:r0 + span, cs]
                        part = term if part is None else part + term
                acc = acc + part[b:b + rows, :]
            cbuf[rc * rows:(rc + 1) * rows, cs] = acc
    def store(r, rs, val):
        act_ref[rs, :] = val
    _cf_finish(cbuf, zbuf, g_ref, b_ref, store, tm, 64)

    @pl.when((i % tiles_per_seq) == tiles_per_seq - 1)
    def _():
        st_ref[0] = ubuf[head + tm - CF_HIST:head + tm, :]


def _cf_prompt(x, w, batch, seq, tm):
    n = x.shape[0]
    tps = seq // tm
    row = lambda c: pl.BlockSpec((tm, c), lambda i: (i, 0))
    kern = functools.partial(_cf_prompt_kernel, tm=tm, tiles_per_seq=tps)
    return pl.pallas_call(
        kern, grid=(n // tm,),
        in_specs=[row(D_MODEL)] + [_const_spec(a.shape) for a in w],
        out_specs=[row(D_MODEL), pl.BlockSpec((1, CF_HIST, D_MODEL), lambda i: (i // tps, 0, 0))],
        out_shape=[jax.ShapeDtypeStruct((n, D_MODEL), BF16),
                   jax.ShapeDtypeStruct((batch, CF_HIST, D_MODEL), F32)],
        scratch_shapes=[pltpu.VMEM((tm + 32, D_MODEL), F32), pltpu.VMEM((tm, D_MODEL), F32),
                        pltpu.VMEM((tm, D_MODEL), F32)],
        compiler_params=_params(1), name="cf_prompt")(x, *w)


def _cf_sample_kernel(x_ref, hist_ref, win_ref, bin_ref, wdw_ref, bdw_ref, g_ref, b_ref, act_ref, st_ref,
                      ubuf, zbuf, cbuf, *, steps, bb):
    n = steps * bb
    u, zs = _cf_project(x_ref[...].reshape(n, D_MODEL), win_ref, bin_ref)
    zbuf[...] = zs
    ubuf[0:CF_HIST] = hist_ref[...]
    for t in range(steps):
        ubuf[CF_HIST + t] = u[t * bb:(t + 1) * bb, :]
    cols = 512
    for t in range(steps):
        for cc in range(D_MODEL // cols):
            cs = slice(cc * cols, (cc + 1) * cols)
            acc = jnp.broadcast_to(bdw_ref[:, cs], (bb, cols))
            for j in range(CF_KERNEL):
                acc = acc + wdw_ref[j:j + 1, cs] * ubuf[t + j, :, cs]
            cbuf[t * bb:(t + 1) * bb, cs] = acc
    def store(r, rs, val):
        act_ref[r] = val
    _cf_finish(cbuf, zbuf, g_ref, b_ref, store, n, bb)
    st_ref[...] = ubuf[steps:steps + CF_HIST]


def _cf_sample(x_tm, hist_tm, w, bb):
    steps, batch, _ = x_tm.shape
    kern = functools.partial(_cf_sample_kernel, steps=steps, bb=bb)
    blk = lambda t, c: pl.BlockSpec((t, bb, c), lambda i: (0, i, 0))
    return pl.pallas_call(
        kern, grid=(batch // bb,),
        in_specs=[blk(steps, D_MODEL), blk(CF_HIST, D_MODEL)] + [_const_spec(a.shape) for a in w],
        out_specs=[blk(steps, D_MODEL), blk(CF_HIST, D_MODEL)],
        out_shape=[jax.ShapeDtypeStruct((steps, batch, D_MODEL), BF16),
                   jax.ShapeDtypeStruct((CF_HIST, batch, D_MODEL), F32)],
        scratch_shapes=[pltpu.VMEM((CF_HIST + steps, bb, D_MODEL), F32),
                        pltpu.VMEM((steps * bb, D_MODEL), F32), pltpu.VMEM((steps * bb, D_MODEL), F32)],
        compiler_params=_params(1), name="cf_sample")(x_tm, hist_tm, *w)


def _mb_project(x2d, wz_ref, wxbc_ref, wdt_ref, dtb_ref, store_z, store_dt, store_xbc):
    xb = x2d.astype(BF16)
    store_z(jnp.dot(xb, wz_ref[...], preferred_element_type=F32))
    store_dt(_softplus(jnp.dot(xb, wdt_ref[...], preferred_element_type=F32) + dtb_ref[...]))
    cols = 1024
    for cc in range(MB_CONV_DIM // cols):
        cs = slice(cc * cols, (cc + 1) * cols)
        store_xbc(cs, jnp.dot(xb, wxbc_ref[:, cs], preferred_element_type=F32))


def _mb_in_prompt_kernel(x_ref, wz_ref, wxbc_ref, wdt_ref, dtb_ref, cw_ref, cb_ref,
                         z_ref, xbc_ref, dt_ref, st_ref, xbuf, *, tm, tiles_per_seq):
    head = 8
    i = pl.program_id(0)
    first = (i % tiles_per_seq) == 0

    @pl.when(first)
    def _():
        xbuf[0:head, :] = jnp.zeros((head, MB_CONV_DIM), F32)

    @pl.when(jnp.logical_not(first))
    def _():
        xbuf[0:head, :] = xbuf[tm:tm + head, :]

    def store_xbc(cs, val):
        xbuf[head:head + tm, cs] = val
    def store_z(val):
        z_ref[...] = val

    def store_dt(val):
        dt_ref[...] = val
    _mb_project(x_ref[...], wz_ref, wxbc_ref, wdt_ref, dtb_ref, store_z, store_dt, store_xbc)
    off = head - (MB_CONV - 1)
    rows, cols = 32, 512
    for rc in range(tm // rows):
        for cc in range(MB_CONV_DIM // cols):
            cs = slice(cc * cols, (cc + 1) * cols)
            win = xbuf[rc * rows:rc * rows + rows + head, cs]
            acc = jnp.broadcast_to(cb_ref[:, cs], (rows, cols))
            for j in range(MB_CONV):
                acc = acc + cw_ref[j:j + 1, cs] * win[off + j:off + j + rows, :]
            xbc_ref[rc * rows:(rc + 1) * rows, cs] = _silu(acc)

    @pl.when((i % tiles_per_seq) == tiles_per_seq - 1)
    def _():
        st_ref[0] = xbuf[head + tm - (MB_CONV - 1):head + tm, :]


def _mb_in_prompt(x, w, batch, seq, tm):
    n = x.shape[0]
    tps = seq // tm
    row = lambda c: pl.BlockSpec((tm, c), lambda i: (i, 0))
    kern = functools.partial(_mb_in_prompt_kernel, tm=tm, tiles_per_seq=tps)
    return pl.pallas_call(
        kern, grid=(n // tm,),
        in_specs=[row(D_MODEL)] + [_const_spec(a.shape) for a in w],
        out_specs=[row(MB_INNER), row(MB_CONV_DIM), row(LANES),
                   pl.BlockSpec((1, MB_CONV - 1, MB_CONV_DIM), lambda i: (i // tps, 0, 0))],
        out_shape=[jax.ShapeDtypeStruct((n, MB_INNER), F32), jax.ShapeDtypeStruct((n, MB_CONV_DIM), F32),
                   jax.ShapeDtypeStruct((n, LANES), F32),
                   jax.ShapeDtypeStruct((batch, MB_CONV - 1, MB_CONV_DIM), F32)],
        scratch_shapes=[pltpu.VMEM((tm + 8, MB_CONV_DIM), F32)],
        compiler_params=_params(1), name="mb_in_prompt")(x, *w)


def _mb_in_sample_kernel(x_ref, hist_ref, wz_ref, wxbc_ref, wdt_ref, dtb_ref, cw_ref, cb_ref,
                         z_ref, xbc_ref, dt_ref, st_ref, xbuf, *, steps, bb):
    n = steps * bb
    hist = MB_CONV - 1
    xbuf[0:hist] = hist_ref[...]

    def store_xbc(cs, val):
        for t in range(steps):
            xbuf[hist + t, :, cs] = val[t * bb:(t + 1) * bb, :]
    def store_z(val):
        for t in range(steps):
            z_ref[t] = val[t * bb:(t + 1) * bb, :]

    def store_dt(val):
        for t in range(steps):
            dt_ref[t] = val[t * bb:(t + 1) * bb, :]
    _mb_project(x_ref[...].reshape(n, D_MODEL), wz_ref, wxbc_ref, wdt_ref, dtb_ref,
                store_z, store_dt, store_xbc)
    cols = 512
    for t in range(steps):
        for cc in range(MB_CONV_DIM // cols):
            cs = slice(cc * cols, (cc + 1) * cols)
            acc = jnp.broadcast_to(cb_ref[:, cs], (bb, cols))
            for j in range(MB_CONV):
                acc = acc + cw_ref[j:j + 1, cs] * xbuf[t + j, :, cs]
            xbc_ref[t, :, cs] = _silu(acc)
    st_ref[...] = xbuf[steps:steps + hist]


def _mb_in_sample(x_tm, hist_tm, w, bb):
    steps, batch, _ = x_tm.shape
    hist = MB_CONV - 1
    kern = functools.partial(_mb_in_sample_kernel, steps=steps, bb=bb)
    blk = lambda t, c: pl.BlockSpec((t, bb, c), lambda i: (0, i, 0))
    return pl.pallas_call(
        kern, grid=(batch // bb,),
        in_specs=[blk(steps, D_MODEL), blk(hist, MB_CONV_DIM)] + [_const_spec(a.shape) for a in w],
        out_specs=[blk(steps, MB_INNER), blk(steps, MB_CONV_DIM), blk(steps, LANES), blk(hist, MB_CONV_DIM)],
        out_shape=[jax.ShapeDtypeStruct((steps, batch, MB_INNER), F32),
                   jax.ShapeDtypeStruct((steps, batch, MB_CONV_DIM), F32),
                   jax.ShapeDtypeStruct((steps, batch, LANES), F32),
                   jax.ShapeDtypeStruct((hist, batch, MB_CONV_DIM), F32)],
        scratch_shapes=[pltpu.VMEM((hist + steps, bb, MB_CONV_DIM), F32)],
        compiler_params=_params(1), name="mb_in_sample")(x_tm, hist_tm, *w)


def _ssd_kernel(*refs, q, valid, n_chunks, has_s0):
    if has_s0:
        (xbc_x, xbc_b, xbc_c, dt_ref, z_ref, alog_ref, dskip_ref, ng_ref, s0_ref,
         act_ref, sfin_ref, s_ref, ybuf) = refs
    else:
        (xbc_x, xbc_b, xbc_c, dt_ref, z_ref, alog_ref, dskip_ref, ng_ref,
         act_ref, sfin_ref, s_ref, ybuf) = refs
    c = pl.program_id(1)
    r_per_g = MB_HEADS // MB_GROUPS
    gw = r_per_g * MB_HEAD_DIM

    @pl.when(c == 0)
    def _():
        if has_s0:
            for g in range(MB_GROUPS):
                s_ref[g] = s0_ref[0, g * r_per_g:(g + 1) * r_per_g].reshape(gw, MB_STATE).T
        else:
            s_ref[...] = jnp.zeros(s_ref.shape, F32)

    dt = dt_ref[...]
    if valid < q:
        dt = jnp.where(lax.broadcasted_iota(jnp.int32, (q, LANES), 0) < valid, dt, 0.0)
    a = dt * (-jnp.exp(alog_ref[...]))
    rid = lax.broadcasted_iota(jnp.int32, (q, q), 0)
    cid = lax.broadcasted_iota(jnp.int32, (q, q), 1)
    tri = rid >= cid
    lower = jnp.where(tri, 1.0, 0.0)
    upper = jnp.where(rid <= cid, 1.0, 0.0)
    eye = jnp.where(rid == cid, 1.0, 0.0)
    hi = lax.Precision.HIGHEST
    cum = jnp.dot(lower, a, precision=hi, preferred_element_type=F32)
    tn = (((0,), (0,)), ((), ()))
    cum_t = lax.dot_general(a, upper, tn, precision=hi, preferred_element_type=F32)
    dt_t = lax.dot_general(dt, eye, tn, precision=hi, preferred_element_type=F32)

    def hi_lo(v):
        v_hi = v.astype(BF16)
        return jnp.concatenate([v_hi, (v - v_hi.astype(F32)).astype(BF16)], axis=1)
    ecum_hl = hi_lo(jnp.exp(cum))
    wend_hl = hi_lo(jnp.exp(cum[q - 1:q, :] - cum) * dt)
    src_head = lax.broadcasted_iota(jnp.int32, (2 * LANES, gw), 0) & (LANES - 1)
    dst_head = lax.broadcasted_iota(jnp.int32, (2 * LANES, gw), 1) // MB_HEAD_DIM
    lane_head = lax.broadcasted_iota(jnp.int32, (1, gw), 1) // MB_HEAD_DIM
    for g in range(MB_GROUPS):
        gs = slice(g * MB_STATE, (g + 1) * MB_STATE)
        xs = slice(g * gw, (g + 1) * gw)
        cg = xbc_c[:, gs].astype(BF16)
        bg = xbc_b[:, gs].astype(BF16)
        cb = _bdot_nt(cg, bg)
        spread = jnp.where(src_head == g * r_per_g + dst_head, 1.0, 0.0).astype(BF16)
        ecum_g = jnp.dot(ecum_hl, spread, preferred_element_type=F32)
        wend_g = jnp.dot(wend_hl, spread, preferred_element_type=F32)
        xg = xbc_x[:, xs]
        st = s_ref[g]
        y = _bdot(cg, st) * ecum_g + dskip_ref[:, xs] * xg
        for r in range(r_per_g):
            h = g * r_per_g + r
            decay = jnp.exp(jnp.where(tri, cum[:, h:h + 1] - cum_t[h:h + 1, :], -jnp.inf))
            m = cb * decay * dt_t[h:h + 1, :]
            y = y + _bdot(m, jnp.where(lane_head == r, xg, 0.0))
        ybuf[:, xs] = y
        s_ref[g] = ecum_g[q - 1:q, :] * st + _bdot_tn(bg, xg * wend_g)
    y = ybuf[...] * _silu(z_ref[...])
    y = y * lax.rsqrt(jnp.mean(y * y, axis=-1, keepdims=True) + LN_EPS) * ng_ref[...]
    act_ref[...] = y.astype(act_ref.dtype)

    @pl.when(c == n_chunks - 1)
    def _():
        for g in range(MB_GROUPS):
            sfin_ref[0, g * r_per_g:(g + 1) * r_per_g] = s_ref[g].T.reshape(r_per_g, MB_HEAD_DIM, MB_STATE)


def _ssd(xbc, dt, z, alog, dskip, ng, s0, batch, q, valid, n_chunks, act_dtype):
    n = xbc.shape[0]
    has_s0 = s0 is not None
    rowblk = lambda cols, cb: pl.BlockSpec((q, cols), lambda b, c: (b * n_chunks + c, cb))
    st_spec = pl.BlockSpec((1, MB_HEADS, MB_HEAD_DIM, MB_STATE), lambda b, c: (b, 0, 0, 0))
    in_specs = [rowblk(MB_INNER, 0), rowblk(MB_GROUPS * MB_STATE, 2), rowblk(MB_GROUPS * MB_STATE, 3),
                rowblk(LANES, 0), rowblk(MB_INNER, 0),
                _const_spec(alog.shape), _const_spec(dskip.shape), _const_spec(ng.shape)]
    args = [xbc, xbc, xbc, dt, z, alog, dskip, ng]
    if has_s0:
        in_specs.append(st_spec)
        args.append(s0)
    kern = functools.partial(_ssd_kernel, q=q, valid=valid, n_chunks=n_chunks, has_s0=has_s0)
    return pl.pallas_call(
        kern, grid=(batch, n_chunks), in_specs=in_specs,
        out_specs=[rowblk(MB_INNER, 0), st_spec],
        out_shape=[jax.ShapeDtypeStruct((n, MB_INNER), act_dtype),
                   jax.ShapeDtypeStruct((batch, MB_HEADS, MB_HEAD_DIM, MB_STATE), F32)],
        scratch_shapes=[pltpu.VMEM((MB_GROUPS, MB_STATE, MB_INNER // MB_GROUPS), F32),
                        pltpu.VMEM((q, MB_INNER), F32)],
        compiler_params=_params(2), name="ssd")(*args)


def _at_in_kernel(x_ref, wq_ref, wk_ref, wv_ref, wqi_ref, wki_ref, wwi_ref, wz_ref, kg_ref, kb_ref,
                  q_ref, k_ref, v_ref, qi_ref, ki_ref, wi_ref, z_ref):
    xb = x_ref[...].astype(BF16)
    dot = lambda w: jnp.dot(xb, w[...], preferred_element_type=F32)
    q_ref[...] = dot(wq_ref) * (AT_HEAD_DIM ** -0.5)
    k_ref[...] = dot(wk_ref)
    v_ref[...] = dot(wv_ref)
    qi_ref[...] = dot(wqi_ref) * (IDX_DIM ** -0.5)
    ki_ref[...] = _layer_norm(dot(wki_ref), kg_ref[...], kb_ref[...])
    wi_ref[...] = dot(wwi_ref) * (IDX_HEADS ** -0.5)
    z_ref[...] = _silu(dot(wz_ref))


def _at_in(x, w, tm):
    n = x.shape[0]
    row = lambda c: pl.BlockSpec((tm, c), lambda i: (i, 0))
    widths = [D_MODEL, AT_KV_DIM, AT_KV_DIM, IDX_HEADS * IDX_DIM, IDX_DIM, LANES, D_MODEL]
    return pl.pallas_call(
        _at_in_kernel, grid=(n // tm,),
        in_specs=[row(D_MODEL)] + [_const_spec(a.shape) for a in w],
        out_specs=[row(c) for c in widths],
        out_shape=[jax.ShapeDtypeStruct((n, c), F32) for c in widths],
        compiler_params=_params(1), name="at_in")(x, *w)


ALIBI_COLS = 6


def _split3_bf16(x):
    hi = x.astype(jnp.bfloat16).astype(np.float32)
    mid = (x - hi).astype(jnp.bfloat16).astype(np.float32)
    lo = (x - hi - mid).astype(jnp.bfloat16).astype(np.float32)
    return hi, mid, lo


def _alibi_query_columns():
    s = np.asarray(ALIBI_SLOPES, np.float32) * np.float32(np.log2(np.e))
    parts = _split3_bf16(s)
    out = np.zeros((AT_HEADS, LANES), np.float32)
    for i, part in enumerate(parts):
        out[:, AT_HEAD_DIM + i] = 16.0 * part
        out[:, AT_HEAD_DIM + 3 + i] = part
    return out.reshape(1, AT_HEADS * LANES)


def _slot_weights(w, n_slots):
    d = w.shape[0]
    w = w.reshape(d, n_slots, AT_HEAD_DIM)
    return jnp.pad(w, ((0, 0), (0, 0), (0, LANES - AT_HEAD_DIM))).reshape(d, n_slots * LANES)


def _at_in_prompt_kernel(x_ref, wqs_ref, wk_ref, wv_ref, wks_ref, wvs_ref, wqi_ref, wki_ref, wwi_ref, wz_ref,
                         kg_ref, kb_ref, qcol_ref,
                         k_ref, v_ref, qi_ref, ki_ref, wi_ref, z_ref, qaug_ref, kaug_ref, vaug_ref,
                         *, tm, tiles_per_seq):
    xb = x_ref[...].astype(BF16)
    dot = lambda w: jnp.dot(xb, w[...], preferred_element_type=F32)
    k_ref[...] = dot(wk_ref)
    v_ref[...] = dot(wv_ref)
    qi_ref[...] = dot(wqi_ref) * (IDX_DIM ** -0.5)
    ki_ref[...] = _layer_norm(dot(wki_ref), kg_ref[...], kb_ref[...])
    wi_ref[...] = dot(wwi_ref) * (IDX_HEADS ** -0.5)
    z_ref[...] = _silu(dot(wz_ref))
    qaug_ref[...] = (dot(wqs_ref) * (AT_HEAD_DIM ** -0.5 * float(np.log2(np.e))) + qcol_ref[...]).astype(BF16)
    pos = (pl.program_id(0) % tiles_per_seq) * tm + lax.broadcasted_iota(jnp.int32, (tm, 1), 0)
    a = (pos >> 4).astype(F32)
    c = (pos & 15).astype(F32)
    col = lax.broadcasted_iota(jnp.int32, (1, AT_KV_HEADS * LANES), 1) & (LANES - 1)
    in_a = (col >= AT_HEAD_DIM) & (col < AT_HEAD_DIM + 3)
    in_c = (col >= AT_HEAD_DIM + 3) & (col < AT_HEAD_DIM + ALIBI_COLS)
    kaug_ref[...] = (dot(wks_ref) + jnp.where(in_a, a, jnp.where(in_c, c, 0.0))).astype(BF16)
    vaug_ref[...] = (dot(wvs_ref) + jnp.where(col == AT_HEAD_DIM, 1.0, 0.0)).astype(BF16)


def _at_in_prompt(x, w, seq, tm):
    n = x.shape[0]
    row = lambda c: pl.BlockSpec((tm, c), lambda i: (i, 0))
    outs = [(AT_KV_DIM, F32), (AT_KV_DIM, F32), (IDX_HEADS * IDX_DIM, F32), (IDX_DIM, F32), (LANES, F32),
            (D_MODEL, F32), (AT_HEADS * LANES, BF16), (AT_KV_HEADS * LANES, BF16), (AT_KV_HEADS * LANES, BF16)]
    kern = functools.partial(_at_in_prompt_kernel, tm=tm, tiles_per_seq=seq // tm)
    return pl.pallas_call(
        kern, grid=(n // tm,),
        in_specs=[row(D_MODEL)] + [_const_spec(a.shape) for a in w],
        out_specs=[row(c) for c, _ in outs],
        out_shape=[jax.ShapeDtypeStruct((n, c), dt) for c, dt in outs],
        compiler_params=_params(1), name="at_in_prompt")(x, *w)


def _page_dmas(pt_ref, seq0, n_seq, n_pages, src_hbm, dst_of, sem):
    out = []
    for g in range(n_seq):
        for p in range(n_pages):
            out.append(pltpu.make_async_copy(src_hbm.at[pt_ref[(seq0 + g) * n_pages + p]], dst_of(g, p), sem))
    return out


def _prefetch_step(dmas):
    i = pl.program_id(0)
    slot = i % 2

    @pl.when(i == 0)
    def _():
        for c in dmas(0, 0):
            c.start()

    @pl.when(i + 1 < pl.num_programs(0))
    def _():
        for c in dmas(i + 1, 1 - slot):
            c.start()
    for c in dmas(i, slot):
        c.wait()
    return slot


def _dsa_select_sample_kernel(pt_ref, qi_ref, wi_ref, kin_ref, kidx_hbm, bias_ref, kibuf, sem, idxbuf, keybuf,
                              *, g_seq, n_pages, page, steps, topk):
    past = n_pages * page
    s_pad = past + LANES
    n_tiles = s_pad // LANES
    rows = g_seq * 8
    slot = _prefetch_step(lambda step, sl: _page_dmas(
        pt_ref, step * g_seq, g_seq, n_pages, kidx_hbm,
        lambda g, p: kibuf.at[sl, g, :, pl.ds(p * page, page)], sem.at[sl]))

    idxbuf[...] = jnp.full((rows, s_pad), -jnp.inf, F32)
    tt = lax.broadcasted_iota(jnp.int32, (steps, 8), 0)
    jj = lax.broadcasted_iota(jnp.int32, (steps, 8), 1)
    for g in range(g_seq):
        qs = qi_ref[g].astype(BF16)
        w = wi_ref[g]
        sc = _bdot(qs, kibuf[slot, g])
        val = jnp.maximum(sc, 0.0) * jnp.concatenate([w] * (past // LANES), axis=1)
        idxbuf[g * 8:g * 8 + steps, 0:past] = jnp.sum(val.reshape(steps, IDX_HEADS, past), axis=1)
        scn = _bdot_nt(qs, kin_ref[g])
        valn = jnp.maximum(scn, 0.0) * w[:, 0:8]
        idn = jnp.sum(valn.reshape(steps, IDX_HEADS, 8), axis=1)
        idxbuf[g * 8:g * 8 + steps, past:past + 8] = jnp.where(jj <= tt, idn, -jnp.inf)

    bits = pltpu.bitcast(idxbuf[...], jnp.int32)
    keybuf[...] = jnp.where(bits < 0, bits ^ jnp.int32(0x7FFFFFFF), bits)

    def tile(c):
        return keybuf[:, c * LANES:(c + 1) * LANES]

    def bit_pass(b, carry):
        thr, cnt_gt = carry
        cand = thr + jnp.left_shift(jnp.int32(1), 31 - b)
        part = jnp.zeros((rows, LANES), F32)
        for c in range(n_tiles):
            part = part + jnp.where(tile(c) >= cand, 1.0, 0.0)
        total = jnp.sum(part, axis=1, keepdims=True)
        ok = total >= float(topk)
        return jnp.where(ok, cand, thr), jnp.where(ok, cnt_gt, total)
    thr, cnt_gt = lax.fori_loop(0, 32, bit_pass, (jnp.full((rows, 1), INT_MIN, jnp.int32),
                                                  jnp.zeros((rows, 1), F32)))
    need_eq = float(topk) - cnt_gt

    incl = jnp.where(lax.broadcasted_iota(jnp.int32, (LANES, LANES), 0)
                     <= lax.broadcasted_iota(jnp.int32, (LANES, LANES), 1), 1.0, 0.0).astype(BF16)
    eq_rows = jnp.concatenate([jnp.where(tile(c) == thr, 1.0, 0.0).astype(BF16) for c in range(n_tiles)], axis=0)
    pre = jnp.dot(eq_rows, incl, preferred_element_type=F32)
    run = jnp.zeros((rows, 1), F32)
    for c in range(n_tiles):
        key = tile(c)
        pc = pre[c * rows:(c + 1) * rows] + run
        sel = (key > thr) | ((key == thr) & (pc <= need_eq))
        idxbuf[:, c * LANES:(c + 1) * LANES] = jnp.where(sel, 0.0, NEG_BIG)
        run = pc[:, LANES - 1:LANES]
    for g in range(g_seq):
        bias_ref[g] = idxbuf[g * 8:(g + 1) * 8, :]


def _dsa_select_sample(page_table, qi_st, wi_col, ki_new, cache_ki, topk, g_seq):
    bs, n_pages = page_table.shape
    page = cache_ki.shape[2]
    past = n_pages * page
    s_pad = past + LANES
    rows_q = qi_st.shape[1]
    blk = lambda r, c: pl.BlockSpec((g_seq, r, c), lambda i, pt: (i, 0, 0))
    kern = functools.partial(_dsa_select_sample_kernel, g_seq=g_seq, n_pages=n_pages, page=page,
                             steps=rows_q // IDX_HEADS, topk=topk)
    return pl.pallas_call(
        kern,
        grid_spec=pltpu.PrefetchScalarGridSpec(
            num_scalar_prefetch=1, grid=(bs // g_seq,),
            in_specs=[blk(rows_q, IDX_DIM), blk(rows_q, LANES), blk(8, IDX_DIM), pl.BlockSpec(memory_space=pl.ANY)],
            out_specs=blk(8, s_pad),
            scratch_shapes=[pltpu.VMEM((2, g_seq, IDX_DIM, past), F32), pltpu.SemaphoreType.DMA((2,)),
                            pltpu.VMEM((g_seq * 8, s_pad), F32), pltpu.VMEM((g_seq * 8, s_pad), jnp.int32)]),
        out_shape=jax.ShapeDtypeStruct((bs, 8, s_pad), F32),
        compiler_params=_params(1), name="dsa_select_sample")(page_table.reshape(-1), qi_st, wi_col, ki_new, cache_ki)


def _dsa_attend_sample_kernel(pt_ref, qbd_ref, zbd_ref, bias_ref, knew_ref, vnew_ref, meta_ref, k_hbm, v_hbm,
                              out_ref, kbuf, vbuf, sem, *, n_pages, page, steps):
    past = n_pages * page

    def dmas(step, sl):
        return (_page_dmas(pt_ref, step, 1, n_pages, k_hbm, lambda g, p: kbuf.at[sl, :, pl.ds(p * page, page)],
                           sem.at[0, sl])
                + _page_dmas(pt_ref, step, 1, n_pages, v_hbm, lambda g, p: vbuf.at[sl, :, pl.ds(p * page, page)],
                             sem.at[1, sl]))
    slot = _prefetch_step(dmas)

    q = qbd_ref[0].astype(BF16)
    slope = meta_ref[:, 0:1]
    tq = meta_ref[:, 1:2]
    bias = jnp.concatenate([jnp.broadcast_to(bias_ref[0, t:t + 1, :], (AT_HEADS, past + LANES))
                            for t in range(steps)], axis=0)
    pos_old = lax.broadcasted_iota(jnp.int32, (1, past), 1).astype(F32)
    l_old = _bdot(q, kbuf[slot]) - slope * ((float(past) + tq) - pos_old) + bias[:, 0:past]
    pos_new = lax.broadcasted_iota(jnp.int32, (1, 8), 1).astype(F32)
    l_new = _bdot_nt(q, knew_ref[0]) - slope * (tq - pos_new) + bias[:, past:past + 8]
    m = jnp.maximum(jnp.max(l_old, axis=1, keepdims=True), jnp.max(l_new, axis=1, keepdims=True))
    p_old = jnp.exp(l_old - m)
    p_new = jnp.exp(l_new - m)
    den = jnp.sum(p_old, axis=1, keepdims=True) + jnp.sum(p_new, axis=1, keepdims=True)
    o = _bdot_nt(p_old, vbuf[slot]) + _bdot(p_new, vnew_ref[0])
    out_ref[0] = o / den * zbd_ref[0]


def _dsa_attend_sample(page_table, qbd, zbd, bias, k_new, v_new, meta, cache_k, cache_v):
    bs, n_pages = page_table.shape
    page = cache_k.shape[2]
    past = n_pages * page
    rows = qbd.shape[1]
    seq = lambda r, c: pl.BlockSpec((1, r, c), lambda i, pt: (i, 0, 0))
    hbm = pl.BlockSpec(memory_space=pl.ANY)
    kern = functools.partial(_dsa_attend_sample_kernel, n_pages=n_pages, page=page, steps=rows // AT_HEADS)
    return pl.pallas_call(
        kern,
        grid_spec=pltpu.PrefetchScalarGridSpec(
            num_scalar_prefetch=1, grid=(bs,),
            in_specs=[seq(rows, AT_KV_DIM), seq(rows, AT_KV_DIM), seq(8, past + LANES), seq(8, AT_KV_DIM),
                      seq(8, AT_KV_DIM), pl.BlockSpec(meta.shape, lambda i, pt: (0, 0)), hbm, hbm],
            out_specs=seq(rows, AT_KV_DIM),
            scratch_shapes=[pltpu.VMEM((2, AT_KV_DIM, past), F32), pltpu.VMEM((2, AT_KV_DIM, past), F32),
                            pltpu.SemaphoreType.DMA((2, 2))]),
        out_shape=jax.ShapeDtypeStruct((bs, rows, AT_KV_DIM), F32),
        compiler_params=_params(1), name="dsa_attend_sample")(
            page_table.reshape(-1), qbd, zbd, bias, k_new, v_new, meta, cache_k, cache_v)


def _dsa_kernel(qaug_ref, qi_ref, wi_ref, z_ref, kaug_ref, vaug_ref, ki_ref, act_ref,
                keys_ref, half_ref, bias_ref, mrun_ref, m_ref, acc_ref, p_ref, *, t, topk):
    kb_sz = KEY_BLOCK
    pos0 = pl.program_id(1) * t
    n_kb = (pos0 + t - 1) // kb_sz + 1
    rowpos = pos0 + lax.broadcasted_iota(jnp.int32, (t, 1), 0)
    lane = lax.broadcasted_iota(jnp.int32, (1, kb_sz), 1)

    def score_block(kb, carry):
        ks = pl.ds(pl.multiple_of(kb * kb_sz, kb_sz), kb_sz)
        kib = ki_ref[ks, :].astype(BF16)
        acc = jnp.zeros((t, kb_sz), F32)
        for i in range(IDX_HEADS):
            sc = _bdot_nt(qi_ref[:, i * IDX_DIM:(i + 1) * IDX_DIM], kib)
            acc = acc + jnp.maximum(sc, 0.0) * wi_ref[:, i:i + 1]
        acc = jnp.where(kb * kb_sz + lane <= rowpos, acc, -jnp.inf)
        bits = pltpu.bitcast(acc, jnp.int32)
        key = jnp.where(bits < 0, bits ^ jnp.int32(0x7FFFFFFF), bits)
        keys_ref[kb] = key
        half_ref[kb] = (key >> 16).astype(jnp.int16)
        return carry
    lax.fori_loop(0, n_kb, score_block, 0)

    half_min = -(1 << 15)

    def search_half(carry):
        def bit_pass(b, carry):
            res, cnt_gt = carry
            cand = res + jnp.left_shift(jnp.int32(1), 15 - b)
            cand_b = jnp.broadcast_to(cand, (t, LANES)).astype(jnp.int16)

            def body(kb, cnt):
                tile = half_ref[kb]
                for s in range(kb_sz // LANES):
                    cnt = cnt + jnp.where(tile[:, s * LANES:(s + 1) * LANES] >= cand_b, jnp.int16(1), jnp.int16(0))
                return cnt
            cnt = lax.fori_loop(0, n_kb, body, jnp.zeros((t, LANES), jnp.int16))
            total = jnp.sum(cnt.astype(F32), axis=1, keepdims=True)
            ok = total >= float(topk)
            return jnp.where(ok, cand, res), jnp.where(ok, cnt_gt, total)
        return lax.fori_loop(0, 16, bit_pass, carry)
    thr_hi, cnt_gt = search_half((jnp.full((t, 1), half_min, jnp.int32), jnp.zeros((t, 1), F32)))

    def low_half_block(kb, carry):
        key = keys_ref[kb]
        hi = key >> 16
        lo = (key & jnp.int32(0xFFFF)) + half_min
        half_ref[kb] = jnp.where(hi == thr_hi, lo, jnp.where(hi > thr_hi, -half_min - 1, half_min)).astype(jnp.int16)
        return carry
    lax.fori_loop(0, n_kb, low_half_block, 0)
    thr_lo, cnt_gt = search_half((jnp.full((t, 1), half_min, jnp.int32), cnt_gt))
    thr = (thr_hi << 16) + (thr_lo - half_min)
    need_eq = float(topk) - cnt_gt

    mrun_ref[...] = jnp.full(mrun_ref.shape, NEG_BIG, F32)
    incl = jnp.where(lax.broadcasted_iota(jnp.int32, (kb_sz, kb_sz), 0)
                     <= lax.broadcasted_iota(jnp.int32, (kb_sz, kb_sz), 1), 1.0, 0.0).astype(BF16)
    r_per_g = AT_HEADS // AT_KV_HEADS
    nt = (((1,), (1,)), ((), ()))

    def logits(h, kg, mask_bias):
        return lax.dot_general(qaug_ref[:, h * LANES:(h + 1) * LANES], kg, nt,
                               preferred_element_type=F32) + mask_bias

    def max_block(kb, run_eq):
        ks = pl.ds(pl.multiple_of(kb * kb_sz, kb_sz), kb_sz)
        key = keys_ref[kb]
        eq = key == thr
        pre = jnp.dot(jnp.where(eq, 1.0, 0.0).astype(BF16), incl, preferred_element_type=F32) + run_eq
        sel = ((key > thr) | (eq & (pre <= need_eq))) & (kb * kb_sz + lane <= rowpos)
        mask_bias = jnp.where(sel, 0.0, NEG_BIG)
        bias_ref[kb] = mask_bias
        for g in range(AT_KV_HEADS):
            kg = kaug_ref[ks, g * LANES:(g + 1) * LANES]
            for r in range(r_per_g):
                h = g * r_per_g + r
                s = logits(h, kg, mask_bias)
                tile_max = s[:, 0:LANES]
                for c in range(1, kb_sz // LANES):
                    tile_max = jnp.maximum(tile_max, s[:, c * LANES:(c + 1) * LANES])
                mrun_ref[h] = jnp.maximum(mrun_ref[h], tile_max)
        return pre[:, kb_sz - 1:kb_sz]
    lax.fori_loop(0, n_kb, max_block, jnp.zeros((t, 1), F32))
    for h in range(AT_HEADS):
        m_ref[h] = jnp.max(mrun_ref[h], axis=1, keepdims=True)

    acc_ref[...] = jnp.zeros(acc_ref.shape, F32)

    def attend_block(kb, carry):
        ks = pl.ds(pl.multiple_of(kb * kb_sz, kb_sz), kb_sz)
        mask_bias = bias_ref[kb]
        for g in range(AT_KV_HEADS):
            kg = kaug_ref[ks, g * LANES:(g + 1) * LANES]
            for r in range(r_per_g):
                h = g * r_per_g + r
                p_ref[h] = jnp.exp2(logits(h, kg, mask_bias) - m_ref[h]).astype(BF16)
        for g in range(AT_KV_HEADS):
            vg = vaug_ref[ks, g * LANES:(g + 1) * LANES]
            for r in range(r_per_g):
                h = g * r_per_g + r
                acc_ref[h] = acc_ref[h] + jnp.dot(p_ref[h], vg, preferred_element_type=F32)
        return carry
    lax.fori_loop(0, n_kb, attend_block, 0)

    for h in range(AT_HEADS):
        hs = slice(h * AT_HEAD_DIM, (h + 1) * AT_HEAD_DIM)
        a = acc_ref[h]
        o = a[:, 0:AT_HEAD_DIM] / a[:, AT_HEAD_DIM:AT_HEAD_DIM + 1]
        act_ref[:, hs] = (o * z_ref[:, hs]).astype(act_ref.dtype)


def _dsa(qaug, qi, wi, z, kaug, vaug, ki_all, batch, n_qb, t, s_len, topk):
    n = qaug.shape[0]
    qrow = lambda c: pl.BlockSpec((t, c), lambda b, j: (b * n_qb + j, 0))
    krow = lambda c: pl.BlockSpec((s_len, c), lambda b, j: (b, 0))
    n_kb = s_len // KEY_BLOCK
    kern = functools.partial(_dsa_kernel, t=t, topk=topk)
    return pl.pallas_call(
        kern, grid=(batch, n_qb),
        in_specs=[qrow(AT_HEADS * LANES), qrow(IDX_HEADS * IDX_DIM), qrow(LANES), qrow(D_MODEL),
                  krow(AT_KV_HEADS * LANES), krow(AT_KV_HEADS * LANES), krow(IDX_DIM)],
        out_specs=qrow(D_MODEL), out_shape=jax.ShapeDtypeStruct((n, D_MODEL), BF16),
        scratch_shapes=[pltpu.VMEM((n_kb, t, KEY_BLOCK), jnp.int32), pltpu.VMEM((n_kb, t, KEY_BLOCK), jnp.int16),
                        pltpu.VMEM((n_kb, t, KEY_BLOCK), F32),
                        pltpu.VMEM((AT_HEADS, t, LANES), F32), pltpu.VMEM((AT_HEADS, t, 1), F32),
                        pltpu.VMEM((AT_HEADS, t, LANES), F32), pltpu.VMEM((AT_HEADS, t, KEY_BLOCK), BF16)],
        compiler_params=_params(2), name="dsa")(qaug, qi, wi, z, kaug, vaug, ki_all)


def _gm_kernel(*refs, tm, chunk, emit_v):
    if emit_v:
        (x_ref, win_ref, bin_ref, g_ref, b_ref, wmix_ref, bmix_ref, act_ref, v_ref, ubuf, vbuf, zbuf) = refs
    else:
        (x_ref, win_ref, bin_ref, g_ref, b_ref, wmix_ref, bmix_ref, act_ref, ubuf, vbuf, zbuf) = refs
    xb = x_ref[...].astype(BF16)
    w = GM_WIDTH
    cols = 512
    for cc in range(w // cols):
        cs = slice(cc * cols, (cc + 1) * cols)
        proj = lambda off: (jnp.dot(xb, win_ref[:, off + cc * cols:off + (cc + 1) * cols],
                                    preferred_element_type=F32) + bin_ref[:, off + cc * cols:off + (cc + 1) * cols])
        ubuf[:, cs] = _gelu_tanh(proj(0))
        vbuf[:, cs] = _gelu_tanh(proj(w))
        zbuf[:, cs] = _silu(proj(2 * w))

    rows = 32

    def norm_rows(r, carry):
        rs = pl.ds(pl.multiple_of(r * rows, rows), rows)
        v = _layer_norm(vbuf[rs, :], g_ref[...], b_ref[...])
        vbuf[rs, :] = v
        if emit_v:
            v_ref[rs, :] = v
        return carry
    lax.fori_loop(0, tm // rows, norm_rows, 0)

    gw = w // GM_GROUPS
    for c in range(tm // chunk):
        rs = slice(c * chunk, (c + 1) * chunk)
        for g in range(GM_GROUPS):
            cs = slice(g * gw, (g + 1) * gw)
            s = jnp.dot(wmix_ref[g], vbuf[rs, cs].astype(BF16), preferred_element_type=F32) + bmix_ref[:, g:g + 1]
            act_ref[rs, cs] = (ubuf[rs, cs] * s * zbuf[rs, cs]).astype(act_ref.dtype)


def _gm(x, w, tm, chunk, emit_v):
    n = x.shape[0]
    row = lambda c: pl.BlockSpec((tm, c), lambda i: (i, 0))
    out_specs = [row(GM_WIDTH)]
    out_shape = [jax.ShapeDtypeStruct((n, GM_WIDTH), BF16)]
    if emit_v:
        out_specs.append(row(GM_WIDTH))
        out_shape.append(jax.ShapeDtypeStruct((n, GM_WIDTH), F32))
    kern = functools.partial(_gm_kernel, tm=tm, chunk=chunk, emit_v=emit_v)
    return pl.pallas_call(
        kern, grid=(n // tm,),
        in_specs=[row(D_MODEL)] + [_const_spec(a.shape) for a in w],
        out_specs=out_specs, out_shape=out_shape,
        scratch_shapes=[pltpu.VMEM((tm, GM_WIDTH), F32)] * 3,
        compiler_params=_params(1), name="gmlp")(x, *w)


def _pad_lanes(a, width=LANES):
    return jnp.pad(a, [(0, 0)] * (a.ndim - 1) + [(0, width - a.shape[-1])])


def kernel(x_prompt, x_sample, p_prompt, p_sample, state_cf_conv, state_mb_conv, state_mb_ssm,
           cache_k, cache_v, cache_kidx, page_table, post_ln_g, post_ln_b, ple_w, ple_gate_w,
           cf_w_in, cf_b_in, cf_w_dw, cf_b_dw, cf_ln_g, cf_ln_b, cf_w_out, cf_b_out,
           mb_w_in, mb_w_conv, mb_b_conv, mb_dt_bias, mb_a_log, mb_d_skip, mb_norm_g, mb_w_out,
           at_w_in, at_ki_ln_g, at_ki_ln_b, at_w_out,
           gm_w_in, gm_b_in, gm_ln_g, gm_ln_b, gm_w_s, gm_b_s, gm_w_out):
    bp, lp, d = x_prompt.shape
    bs, ls, _ = x_sample.shape
    row2 = lambda v: v.reshape(1, -1)
    bf = lambda v: v.astype(BF16)

    ple_wb, ple_gate_wb = bf(ple_w), bf(ple_gate_w)
    cf_w = (bf(cf_w_in), row2(cf_b_in), cf_w_dw, row2(cf_b_dw), row2(cf_ln_g), row2(cf_ln_b))
    mb_in_w = (bf(mb_w_in[:, :MB_INNER]), bf(mb_w_in[:, MB_INNER:MB_INNER + MB_CONV_DIM]),
               bf(_pad_lanes(mb_w_in[:, MB_INNER + MB_CONV_DIM:])), _pad_lanes(row2(mb_dt_bias)),
               mb_w_conv, row2(mb_b_conv))
    ssd_w = (_pad_lanes(row2(mb_a_log)), row2(jnp.repeat(mb_d_skip, MB_HEAD_DIM)), row2(mb_norm_g))
    sizes = [AT_HEADS * AT_HEAD_DIM, AT_KV_DIM, AT_KV_DIM, IDX_HEADS * IDX_DIM, IDX_DIM, IDX_HEADS,
             AT_HEADS * AT_HEAD_DIM]
    offs = np.concatenate([[0], np.cumsum(sizes)]).tolist()
    at_parts = [at_w_in[:, offs[i]:offs[i + 1]] for i in range(len(sizes))]
    at_parts[5] = _pad_lanes(at_parts[5])
    at_w = tuple(bf(a) for a in at_parts) + (row2(at_ki_ln_g), row2(at_ki_ln_b))
    wq_b, wk_b, wv_b, wqi_b, wki_b, wwi_b, wz_b = at_w[:7]
    at_w_prompt = (_slot_weights(wq_b, AT_HEADS), wk_b, wv_b, _slot_weights(wk_b, AT_KV_HEADS),
                   _slot_weights(wv_b, AT_KV_HEADS), wqi_b, wki_b, wwi_b, wz_b,
                   row2(at_ki_ln_g), row2(at_ki_ln_b), jnp.asarray(_alibi_query_columns()))
    gm_common = (bf(gm_w_in), row2(gm_b_in), row2(gm_ln_g), row2(gm_ln_b))
    zeros_d = jnp.zeros((1, d), F32)
    out_w = [(bf(cf_w_out), row2(cf_b_out)), (bf(mb_w_out), zeros_d), (bf(at_w_out), zeros_d),
             (bf(gm_w_out), zeros_d)]

    def tail(i, act, x, p, tm):
        wo, bo = out_w[i]
        return _tail(act, x, p, wo, bo, row2(post_ln_g[i]), row2(post_ln_b[i]), ple_gate_wb[i], ple_wb[i], tm)

    tm = 512
    n_p = bp * lp
    x = x_prompt.reshape(n_p, d)
    pp = p_prompt.reshape(DEPTH, n_p, D_PLE)

    act, cf_conv_p = _cf_prompt(x, cf_w, bp, lp, tm)
    x = tail(0, act, x, pp[0], tm)

    z, xbc, dt, mb_conv_p = _mb_in_prompt(x, mb_in_w, bp, lp, tm)
    act, mb_ssm_p = _ssd(xbc, dt, z, *ssd_w, None, bp, MB_CHUNK, MB_CHUNK, lp // MB_CHUNK, BF16)
    x = tail(1, act, x, pp[1], tm)

    k, v, qi, ki, wi, zs, qaug, kaug, vaug = _at_in_prompt(x, at_w_prompt, lp, tm)
    act = _dsa(qaug, qi, wi, zs, kaug, vaug, ki, bp, lp // DSA_ROWS, DSA_ROWS, lp, min(TOPK_MAX, lp // 4))
    x = tail(2, act, x, pp[2], tm)
    k_p = k.reshape(bp, lp, AT_KV_HEADS, AT_HEAD_DIM)
    v_p = v.reshape(bp, lp, AT_KV_HEADS, AT_HEAD_DIM)
    kidx_p = ki.reshape(bp, lp, IDX_DIM)

    gm_w_p = gm_common + (bf(jnp.tril(gm_w_s)), gm_b_s.T)
    (act,) = _gm(x, gm_w_p, tm, GM_CHUNK, False)
    y_prompt = tail(3, act, x, pp[3], tm).reshape(bp, lp, d)

    n_s = bs * ls
    bb = 32
    to_tm = lambda a: jnp.swapaxes(a, 0, 1)
    x_tm = to_tm(x_sample)
    x = x_tm.reshape(n_s, d)
    ps = jnp.swapaxes(p_sample, 1, 2).reshape(DEPTH, n_s, D_PLE)
    tm_s = n_s

    act, cf_st = _cf_sample(x_tm, to_tm(state_cf_conv), cf_w, bb)
    cf_conv_s = to_tm(cf_st)
    x = tail(0, act.reshape(n_s, d), x, ps[0], tm_s)

    z, xbc, dt, mb_st = _mb_in_sample(x.reshape(ls, bs, d), to_tm(state_mb_conv), mb_in_w, bb)
    mb_conv_s = to_tm(mb_st)
    qs = 8

    def to_bm(a):
        a = jnp.pad(to_tm(a), ((0, 0), (0, qs - ls), (0, 0)))
        return a.reshape(bs * qs, a.shape[-1])

    def from_bm(a):
        a = a.reshape(bs, qs, a.shape[-1])[:, :ls]
        return to_tm(a).reshape(n_s, a.shape[-1])

    act, mb_ssm_s = _ssd(to_bm(xbc), to_bm(dt), to_bm(z), *ssd_w, state_mb_ssm, bs, qs, ls, 1, F32)
    x = tail(1, from_bm(act), x, ps[1], tm_s)

    q, k, v, qi, ki, wi, zs = _at_in(x, at_w, tm_s)
    k_s = to_tm(k.reshape(ls, bs, AT_KV_HEADS, AT_HEAD_DIM))
    v_s = to_tm(v.reshape(ls, bs, AT_KV_HEADS, AT_HEAD_DIM))
    kidx_s = to_tm(ki.reshape(ls, bs, IDX_DIM))
    n_pool, page = cache_k.shape[:2]
    past = page_table.shape[1] * page
    seq_major = lambda a: to_tm(a.reshape(ls, bs, a.shape[-1]))
    r_per_g = AT_HEADS // AT_KV_HEADS
    group_of_head = (np.arange(AT_HEADS)[:, None] // r_per_g == np.arange(AT_KV_HEADS)[None, :]).astype(np.float32)

    def block_diag(a):
        a = a.reshape(bs, ls, AT_HEADS, 1, AT_HEAD_DIM) * group_of_head[None, None, :, :, None]
        return a.reshape(bs, ls * AT_HEADS, AT_KV_DIM)

    pad_steps = lambda a: jnp.pad(a, ((0, 0), (0, 8 - ls), (0, 0)))
    qi_st = seq_major(qi).reshape(bs, ls * IDX_HEADS, IDX_DIM)
    wi_col = jnp.broadcast_to(seq_major(wi)[:, :, :IDX_HEADS].reshape(bs, ls * IDX_HEADS, 1),
                              (bs, ls * IDX_HEADS, LANES))
    pages_t = lambda c: jnp.moveaxis(c, 1, -1).reshape(n_pool, -1, page)
    bias = _dsa_select_sample(page_table, qi_st, wi_col, pad_steps(seq_major(ki)), pages_t(cache_kidx),
                              min(TOPK_MAX, (past + ls) // 4), 8)
    meta = np.zeros((ls * AT_HEADS, LANES), np.float32)
    meta[:, 0] = np.tile(np.asarray(ALIBI_SLOPES, np.float32), ls)
    meta[:, 1] = np.repeat(np.arange(ls, dtype=np.float32), AT_HEADS)
    o = _dsa_attend_sample(page_table, block_diag(seq_major(q)), block_diag(seq_major(zs)), bias,
                           pad_steps(seq_major(k)), pad_steps(seq_major(v)), jnp.asarray(meta),
                           pages_t(cache_k), pages_t(cache_v))
    act = o.reshape(bs, ls, AT_HEADS, AT_KV_HEADS, AT_HEAD_DIM).sum(axis=3).reshape(bs, ls, d)
    x = tail(2, to_tm(act).reshape(n_s, d), x, ps[2], tm_s)

    mix = jnp.stack([jnp.kron(jnp.tril(gm_w_s[g, :ls, :ls]), jnp.eye(bs, dtype=F32)) for g in range(GM_GROUPS)])
    gm_w_smp = gm_common + (bf(mix), jnp.repeat(gm_b_s[:, :ls].T, bs, axis=0))
    act, gm_v = _gm(x, gm_w_smp, tm_s, n_s, True)
    y_s = tail(3, act, x, ps[3], tm_s)
    y_sample = to_tm(y_s.reshape(ls, bs, d))
    gm_v_s = to_tm(gm_v.reshape(ls, bs, GM_WIDTH))

    return (y_prompt, y_sample, cf_conv_p, cf_conv_s, mb_conv_p, mb_conv_s, mb_ssm_p, mb_ssm_s,
            k_p, v_p, kidx_p, k_s, v_s, kidx_s, gm_v_s)
```

```python
import functools

import numpy as np
import jax
import jax.numpy as jnp
from jax import lax
from jax.experimental import pallas as pl
from jax.experimental.pallas import tpu as pltpu

F32 = jnp.float32
BF16 = jnp.bfloat16

D_MODEL = 1024
D_PLE = 256
DEPTH = 4
ALPHA_DN = (2 * DEPTH) ** 0.25
LN_EPS = 1e-5

CF_KERNEL = 31
CF_HIST = CF_KERNEL - 1
MB_INNER = 2048
MB_HEAD_DIM = 64
MB_HEADS = 32
MB_GROUPS = 8
MB_STATE = 128
MB_CONV = 4
MB_CONV_DIM = 4096
MB_CHUNK = 128
AT_HEADS = 16
AT_HEAD_DIM = 64
AT_KV_HEADS = 4
AT_KV_DIM = AT_KV_HEADS * AT_HEAD_DIM
IDX_HEADS = 8
IDX_DIM = 64
TOPK_MAX = 256
DSA_ROWS = 128
GM_WIDTH = 2048
GM_GROUPS = 4
GM_CHUNK = 128

LANES = 128
KEY_BLOCK = 512
NEG_BIG = -1e30
INT_MIN = -(2 ** 31)
VMEM_LIMIT = 56 * 1024 * 1024

ALIBI_SLOPES = [float(s) for s in
                (np.float32(2.0) ** (-8.0 * np.arange(1, AT_HEADS + 1, dtype=np.float32) / AT_HEADS))]


def _bdot(a, b):
    return jnp.dot(a.astype(BF16), b.astype(BF16), preferred_element_type=F32)


def _bdot_nt(a, b):
    return lax.dot_general(a.astype(BF16), b.astype(BF16), (((1,), (1,)), ((), ())),
                           preferred_element_type=F32)


def _bdot_tn(a, b):
    return lax.dot_general(a.astype(BF16), b.astype(BF16), (((0,), (0,)), ((), ())),
                           preferred_element_type=F32)


def _sigmoid(x):
    return 1.0 / (1.0 + jnp.exp(-x))


def _silu(x):
    return x * _sigmoid(x)


def _gelu_tanh(x):
    return x * (0.5 * (1.0 + jnp.tanh(np.sqrt(2.0 / np.pi).astype(np.float32) * (x + 0.044715 * (x * x * x)))))


def _softplus(x):
    return jnp.maximum(x, 0.0) + jnp.log1p(jnp.exp(-jnp.abs(x)))


def _layer_norm(x, g, b):
    mu = jnp.mean(x, axis=-1, keepdims=True)
    xc = x - mu
    var = jnp.mean(xc * xc, axis=-1, keepdims=True)
    return xc * lax.rsqrt(var + LN_EPS) * g + b


def _const_spec(shape):
    nd = len(shape)
    return pl.BlockSpec(shape, lambda *_: (0,) * nd)


def _params(n_axes):
    return pltpu.CompilerParams(dimension_semantics=("arbitrary",) * n_axes,
                                vmem_limit_bytes=VMEM_LIMIT)


def _tail_kernel(act_ref, x_ref, p_ref, wo_ref, bo_ref, g_ref, b_ref, wg_ref, wp_ref, o_ref):
    out = _bdot(act_ref[...], wo_ref[...]) + bo_ref[...]
    h = _layer_norm(ALPHA_DN * x_ref[...] + out, g_ref[...], b_ref[...])
    gate = _sigmoid(_bdot(h, wg_ref[...]))
    o_ref[...] = h + gate * _bdot(p_ref[...], wp_ref[...])


def _tail(layer, act, x, p, wo, bo, g, b, wg, wp, tm):
    n, k = act.shape
    row = lambda c: pl.BlockSpec((tm, c), lambda i: (i, 0))
    of_layer = lambda a: pl.BlockSpec((None,) + a.shape[1:], lambda i: (layer,) + (0,) * (a.ndim - 1))
    return pl.pallas_call(
        _tail_kernel, grid=(n // tm,),
        in_specs=[row(k), row(D_MODEL), pl.BlockSpec((None, tm, D_PLE), lambda i: (layer, i, 0)),
                  _const_spec(wo.shape), _const_spec(bo.shape),
                  of_layer(g), of_layer(b), of_layer(wg), of_layer(wp)],
        out_specs=row(D_MODEL), out_shape=jax.ShapeDtypeStruct((n, D_MODEL), F32),
        compiler_params=_params(1), name="layer_tail")(act, x, p, wo, bo, g, b, wg, wp)


def _cf_project(x2d, win_ref, bin_ref):
    xb = x2d.astype(BF16)
    d = D_MODEL
    a = jnp.dot(xb, win_ref[:, 0:d], preferred_element_type=F32) + bin_ref[:, 0:d]
    gl = jnp.dot(xb, win_ref[:, d:2 * d], preferred_element_type=F32) + bin_ref[:, d:2 * d]
    z = jnp.dot(xb, win_ref[:, 2 * d:3 * d], preferred_element_type=F32) + bin_ref[:, 2 * d:3 * d]
    return a * _sigmoid(gl), _silu(z)


def _cf_finish(cbuf, zbuf, g_ref, b_ref, store, n_rows, chunk):
    def body(r, carry):
        rs = pl.ds(pl.multiple_of(r * chunk, chunk), chunk)
        c = _silu(_layer_norm(cbuf[rs, :], g_ref[...], b_ref[...]))
        store(r, rs, (c * zbuf[rs, :]).astype(BF16))
        return carry
    lax.fori_loop(0, n_rows // chunk, body, 0, unroll=2)


def _cf_prompt_kernel(x_ref, win_ref, bin_ref, wdw_ref, bdw_ref, g_ref, b_ref, act_ref, st_ref,
                      ubuf, zbuf, cbuf, *, tm, tiles_per_seq):
    head = 32
    i = pl.program_id(0)
    first = (i % tiles_per_seq) == 0

    @pl.when(first)
    def _():
        ubuf[0:head, :] = jnp.zeros((head, D_MODEL), F32)

    @pl.when(jnp.logical_not(first))
    def _():
        ubuf[0:head, :] = ubuf[tm:tm + head, :]

    u, zs = _cf_project(x_ref[...], win_ref, bin_ref)
    ubuf[head:head + tm, :] = u
    zbuf[...] = zs
    off = head - CF_HIST
    rows, cols = 64, 256
    for rc in range(tm // rows):
        for cc in range(D_MODEL // cols):
            cs = slice(cc * cols, (cc + 1) * cols)
            acc = jnp.broadcast_to(bdw_ref[:, cs], (rows, cols))
            for b in range(8):
                span = rows + (8 if b else 0)
                part = None
                for a in range((off + CF_KERNEL + 7) // 8):
                    j = 8 * a + b - off
                    if 0 <= j < CF_KERNEL:
                        r0 = rc * rows + 8 * a
                        term = wdw_ref[j:j + 1, cs] * ubuf[r0:r0 + span, cs]
                        part = term if part is None else part + term
                acc = acc + part[b:b + rows, :]
            cbuf[rc * rows:(rc + 1) * rows, cs] = acc
    def store(r, rs, val):
        act_ref[rs, :] = val
    _cf_finish(cbuf, zbuf, g_ref, b_ref, store, tm, 64)

    @pl.when((i % tiles_per_seq) == tiles_per_seq - 1)
    def _():
        st_ref[0] = ubuf[head + tm - CF_HIST:head + tm, :]


def _cf_prompt(x, w, batch, seq, tm):
    n = x.shape[0]
    tps = seq // tm
    row = lambda c: pl.BlockSpec((tm, c), lambda i: (i, 0))
    kern = functools.partial(_cf_prompt_kernel, tm=tm, tiles_per_seq=tps)
    return pl.pallas_call(
        kern, grid=(n // tm,),
        in_specs=[row(D_MODEL)] + [_const_spec(a.shape) for a in w],
        out_specs=[row(D_MODEL), pl.BlockSpec((1, CF_HIST, D_MODEL), lambda i: (i // tps, 0, 0))],
        out_shape=[jax.ShapeDtypeStruct((n, D_MODEL), BF16),
                   jax.ShapeDtypeStruct((batch, CF_HIST, D_MODEL), F32)],
        scratch_shapes=[pltpu.VMEM((tm + 32, D_MODEL), F32), pltpu.VMEM((tm, D_MODEL), F32),
                        pltpu.VMEM((tm, D_MODEL), F32)],
        compiler_params=_params(1), name="cf_prompt")(x, *w)


def _cf_sample_kernel(x_ref, hist_ref, win_ref, bin_ref, wdw_ref, bdw_ref, g_ref, b_ref, act_ref, st_ref,
                      ubuf, zbuf, cbuf, *, steps, bb):
    n = steps * bb
    u, zs = _cf_project(x_ref[...].reshape(n, D_MODEL), win_ref, bin_ref)
    zbuf[...] = zs
    ubuf[0:CF_HIST] = hist_ref[...]
    for t in range(steps):
        ubuf[CF_HIST + t] = u[t * bb:(t + 1) * bb, :]
    cols = 512
    for t in range(steps):
        for cc in range(D_MODEL // cols):
            cs = slice(cc * cols, (cc + 1) * cols)
            acc = jnp.broadcast_to(bdw_ref[:, cs], (bb, cols))
            for j in range(CF_KERNEL):
                acc = acc + wdw_ref[j:j + 1, cs] * ubuf[t + j, :, cs]
            cbuf[t * bb:(t + 1) * bb, cs] = acc
    def store(r, rs, val):
        act_ref[r] = val
    _cf_finish(cbuf, zbuf, g_ref, b_ref, store, n, bb)
    st_ref[...] = ubuf[steps:steps + CF_HIST]


def _cf_sample(x_tm, hist_tm, w, bb):
    steps, batch, _ = x_tm.shape
    kern = functools.partial(_cf_sample_kernel, steps=steps, bb=bb)
    blk = lambda t, c: pl.BlockSpec((t, bb, c), lambda i: (0, i, 0))
    return pl.pallas_call(
        kern, grid=(batch // bb,),
        in_specs=[blk(steps, D_MODEL), blk(CF_HIST, D_MODEL)] + [_const_spec(a.shape) for a in w],
        out_specs=[blk(steps, D_MODEL), blk(CF_HIST, D_MODEL)],
        out_shape=[jax.ShapeDtypeStruct((steps, batch, D_MODEL), BF16),
                   jax.ShapeDtypeStruct((CF_HIST, batch, D_MODEL), F32)],
        scratch_shapes=[pltpu.VMEM((CF_HIST + steps, bb, D_MODEL), F32),
                        pltpu.VMEM((steps * bb, D_MODEL), F32), pltpu.VMEM((steps * bb, D_MODEL), F32)],
        compiler_params=_params(1), name="cf_sample")(x_tm, hist_tm, *w)


def _mb_project(x2d, wz_ref, wxbc_ref, wdt_ref, dtb_ref, store_z, store_dt, store_xbc):
    xb = x2d.astype(BF16)
    store_z(jnp.dot(xb, wz_ref[...], preferred_element_type=F32))
    store_dt(_softplus(jnp.dot(xb, wdt_ref[...], preferred_element_type=F32) + dtb_ref[...]))
    cols = 1024
    for cc in range(MB_CONV_DIM // cols):
        cs = slice(cc * cols, (cc + 1) * cols)
        store_xbc(cs, jnp.dot(xb, wxbc_ref[:, cs], preferred_element_type=F32))


def _mb_in_prompt_kernel(x_ref, wz_ref, wxbc_ref, wdt_ref, dtb_ref, cw_ref, cb_ref,
                         z_ref, xbc_ref, dt_ref, st_ref, xbuf, *, tm, tiles_per_seq):
    head = 8
    i = pl.program_id(0)
    first = (i % tiles_per_seq) == 0

    @pl.when(first)
    def _():
        xbuf[0:head, :] = jnp.zeros((head, MB_CONV_DIM), F32)

    @pl.when(jnp.logical_not(first))
    def _():
        xbuf[0:head, :] = xbuf[tm:tm + head, :]

    def store_xbc(cs, val):
        xbuf[head:head + tm, cs] = val
    def store_z(val):
        z_ref[...] = val

    def store_dt(val):
        dt_ref[...] = val
    _mb_project(x_ref[...], wz_ref, wxbc_ref, wdt_ref, dtb_ref, store_z, store_dt, store_xbc)
    off = head - (MB_CONV - 1)
    rows, cols = 32, 512
    for rc in range(tm // rows):
        for cc in range(MB_CONV_DIM // cols):
            cs = slice(cc * cols, (cc + 1) * cols)
            win = xbuf[rc * rows:rc * rows + rows + head, cs]
            acc = jnp.broadcast_to(cb_ref[:, cs], (rows, cols))
            for j in range(MB_CONV):
                acc = acc + cw_ref[j:j + 1, cs] * win[off + j:off + j + rows, :]
            xbc_ref[rc * rows:(rc + 1) * rows, cs] = _silu(acc)

    @pl.when((i % tiles_per_seq) == tiles_per_seq - 1)
    def _():
        st_ref[0] = xbuf[head + tm - (MB_CONV - 1):head + tm, :]


def _mb_in_prompt(x, w, batch, seq, tm):
    n = x.shape[0]
    tps = seq // tm
    row = lambda c: pl.BlockSpec((tm, c), lambda i: (i, 0))
    kern = functools.partial(_mb_in_prompt_kernel, tm=tm, tiles_per_seq=tps)
    return pl.pallas_call(
        kern, grid=(n // tm,),
        in_specs=[row(D_MODEL)] + [_const_spec(a.shape) for a in w],
        out_specs=[row(MB_INNER), row(MB_CONV_DIM), row(LANES),
                   pl.BlockSpec((1, MB_CONV - 1, MB_CONV_DIM), lambda i: (i // tps, 0, 0))],
        out_shape=[jax.ShapeDtypeStruct((n, MB_INNER), F32), jax.ShapeDtypeStruct((n, MB_CONV_DIM), F32),
                   jax.ShapeDtypeStruct((n, LANES), F32),
                   jax.ShapeDtypeStruct((batch, MB_CONV - 1, MB_CONV_DIM), F32)],
        scratch_shapes=[pltpu.VMEM((tm + 8, MB_CONV_DIM), F32)],
        compiler_params=_params(1), name="mb_in_prompt")(x, *w)


def _mb_in_sample_kernel(x_ref, hist_ref, wz_ref, wxbc_ref, wdt_ref, dtb_ref, cw_ref, cb_ref,
                         z_ref, xbc_ref, dt_ref, st_ref, xbuf, *, steps, bb):
    n = steps * bb
    hist = MB_CONV - 1
    xbuf[0:hist] = hist_ref[...]

    def store_xbc(cs, val):
        for t in range(steps):
            xbuf[hist + t, :, cs] = val[t * bb:(t + 1) * bb, :]
    def store_z(val):
        for t in range(steps):
            z_ref[t] = val[t * bb:(t + 1) * bb, :]

    def store_dt(val):
        for t in range(steps):
            dt_ref[t] = val[t * bb:(t + 1) * bb, :]
    _mb_project(x_ref[...].reshape(n, D_MODEL), wz_ref, wxbc_ref, wdt_ref, dtb_ref,
                store_z, store_dt, store_xbc)
    cols = 512
    for t in range(steps):
        for cc in range(MB_CONV_DIM // cols):
            cs = slice(cc * cols, (cc + 1) * cols)
            acc = jnp.broadcast_to(cb_ref[:, cs], (bb, cols))
            for j in range(MB_CONV):
                acc = acc + cw_ref[j:j + 1, cs] * xbuf[t + j, :, cs]
            xbc_ref[t, :, cs] = _silu(acc)
    st_ref[...] = xbuf[steps:steps + hist]


def _mb_in_sample(x_tm, hist_tm, w, bb):
    steps, batch, _ = x_tm.shape
    hist = MB_CONV - 1
    kern = functools.partial(_mb_in_sample_kernel, steps=steps, bb=bb)
    blk = lambda t, c: pl.BlockSpec((t, bb, c), lambda i: (0, i, 0))
    return pl.pallas_call(
        kern, grid=(batch // bb,),
        in_specs=[blk(steps, D_MODEL), blk(hist, MB_CONV_DIM)] + [_const_spec(a.shape) for a in w],
        out_specs=[blk(steps, MB_INNER), blk(steps, MB_CONV_DIM), blk(steps, LANES), blk(hist, MB_CONV_DIM)],
        out_shape=[jax.ShapeDtypeStruct((steps, batch, MB_INNER), F32),
                   jax.ShapeDtypeStruct((steps, batch, MB_CONV_DIM), F32),
                   jax.ShapeDtypeStruct((steps, batch, LANES), F32),
                   jax.ShapeDtypeStruct((hist, batch, MB_CONV_DIM), F32)],
        scratch_shapes=[pltpu.VMEM((hist + steps, bb, MB_CONV_DIM), F32)],
        compiler_params=_params(1), name="mb_in_sample")(x_tm, hist_tm, *w)


def _ssd_kernel(*refs, q, valid, n_chunks, has_s0):
    if has_s0:
        (xbc_x, xbc_b, xbc_c, dt_ref, z_ref, alog_ref, dskip_ref, ng_ref, s0_ref,
         act_ref, sfin_ref, s_ref, ybuf) = refs
    else:
        (xbc_x, xbc_b, xbc_c, dt_ref, z_ref, alog_ref, dskip_ref, ng_ref,
         act_ref, sfin_ref, s_ref, ybuf) = refs
    c = pl.program_id(1)
    r_per_g = MB_HEADS // MB_GROUPS
    gw = r_per_g * MB_HEAD_DIM

    @pl.when(c == 0)
    def _():
        if has_s0:
            for g in range(MB_GROUPS):
                s_ref[g] = s0_ref[0, g * r_per_g:(g + 1) * r_per_g].reshape(gw, MB_STATE).T
        else:
            s_ref[...] = jnp.zeros(s_ref.shape, F32)

    dt = dt_ref[...]
    if valid < q:
        dt = jnp.where(lax.broadcasted_iota(jnp.int32, (q, LANES), 0) < valid, dt, 0.0)
    a = dt * (-jnp.exp(alog_ref[...]))
    rid = lax.broadcasted_iota(jnp.int32, (q, q), 0)
    cid = lax.broadcasted_iota(jnp.int32, (q, q), 1)
    tri = rid >= cid
    lower = jnp.where(tri, 1.0, 0.0)
    upper = jnp.where(rid <= cid, 1.0, 0.0)
    eye = jnp.where(rid == cid, 1.0, 0.0)
    hi = lax.Precision.HIGHEST
    cum = jnp.dot(lower, a, precision=hi, preferred_element_type=F32)
    tn = (((0,), (0,)), ((), ()))
    cum_t = lax.dot_general(a, upper, tn, precision=hi, preferred_element_type=F32)
    dt_t = lax.dot_general(dt, eye, tn, precision=hi, preferred_element_type=F32)

    def hi_lo(v):
        v_hi = v.astype(BF16)
        return jnp.concatenate([v_hi, (v - v_hi.astype(F32)).astype(BF16)], axis=1)
    ecum_hl = hi_lo(jnp.exp(cum))
    wend_hl = hi_lo(jnp.exp(cum[q - 1:q, :] - cum) * dt)
    src_head = lax.broadcasted_iota(jnp.int32, (2 * LANES, gw), 0) & (LANES - 1)
    dst_head = lax.broadcasted_iota(jnp.int32, (2 * LANES, gw), 1) // MB_HEAD_DIM
    lane_head = lax.broadcasted_iota(jnp.int32, (1, gw), 1) // MB_HEAD_DIM
    for g in range(MB_GROUPS):
        gs = slice(g * MB_STATE, (g + 1) * MB_STATE)
        xs = slice(g * gw, (g + 1) * gw)
        cg = xbc_c[:, gs].astype(BF16)
        bg = xbc_b[:, gs].astype(BF16)
        cb = _bdot_nt(cg, bg)
        spread = jnp.where(src_head == g * r_per_g + dst_head, 1.0, 0.0).astype(BF16)
        ecum_g = jnp.dot(ecum_hl, spread, preferred_element_type=F32)
        wend_g = jnp.dot(wend_hl, spread, preferred_element_type=F32)
        xg = xbc_x[:, xs]
        st = s_ref[g]
        y = _bdot(cg, st) * ecum_g + dskip_ref[:, xs] * xg
        for r in range(r_per_g):
            h = g * r_per_g + r
            decay = jnp.exp(jnp.where(tri, cum[:, h:h + 1] - cum_t[h:h + 1, :], -jnp.inf))
            m = cb * decay * dt_t[h:h + 1, :]
            y = y + _bdot(m, jnp.where(lane_head == r, xg, 0.0))
        ybuf[:, xs] = y
        s_ref[g] = ecum_g[q - 1:q, :] * st + _bdot_tn(bg, xg * wend_g)
    y = ybuf[...] * _silu(z_ref[...])
    y = y * lax.rsqrt(jnp.mean(y * y, axis=-1, keepdims=True) + LN_EPS) * ng_ref[...]
    act_ref[...] = y.astype(act_ref.dtype)

    @pl.when(c == n_chunks - 1)
    def _():
        for g in range(MB_GROUPS):
            sfin_ref[0, g * r_per_g:(g + 1) * r_per_g] = s_ref[g].T.reshape(r_per_g, MB_HEAD_DIM, MB_STATE)


def _ssd(xbc, dt, z, alog, dskip, ng, s0, batch, q, valid, n_chunks, act_dtype):
    n = xbc.shape[0]
    has_s0 = s0 is not None
    rowblk = lambda cols, cb: pl.BlockSpec((q, cols), lambda b, c: (b * n_chunks + c, cb))
    st_spec = pl.BlockSpec((1, MB_HEADS, MB_HEAD_DIM, MB_STATE), lambda b, c: (b, 0, 0, 0))
    in_specs = [rowblk(MB_INNER, 0), rowblk(MB_GROUPS * MB_STATE, 2), rowblk(MB_GROUPS * MB_STATE, 3),
                rowblk(LANES, 0), rowblk(MB_INNER, 0),
                _const_spec(alog.shape), _const_spec(dskip.shape), _const_spec(ng.shape)]
    args = [xbc, xbc, xbc, dt, z, alog, dskip, ng]
    if has_s0:
        in_specs.append(st_spec)
        args.append(s0)
    kern = functools.partial(_ssd_kernel, q=q, valid=valid, n_chunks=n_chunks, has_s0=has_s0)
    return pl.pallas_call(
        kern, grid=(batch, n_chunks), in_specs=in_specs,
        out_specs=[rowblk(MB_INNER, 0), st_spec],
        out_shape=[jax.ShapeDtypeStruct((n, MB_INNER), act_dtype),
                   jax.ShapeDtypeStruct((batch, MB_HEADS, MB_HEAD_DIM, MB_STATE), F32)],
        scratch_shapes=[pltpu.VMEM((MB_GROUPS, MB_STATE, MB_INNER // MB_GROUPS), F32),
                        pltpu.VMEM((q, MB_INNER), F32)],
        compiler_params=_params(2), name="ssd")(*args)


def _at_in_kernel(x_ref, wq_ref, wk_ref, wv_ref, wqi_ref, wki_ref, wwi_ref, wz_ref, kg_ref, kb_ref,
                  q_ref, k_ref, v_ref, qi_ref, ki_ref, wi_ref, z_ref):
    xb = x_ref[...].astype(BF16)
    dot = lambda w: jnp.dot(xb, w[...], preferred_element_type=F32)
    q_ref[...] = dot(wq_ref) * (AT_HEAD_DIM ** -0.5)
    k_ref[...] = dot(wk_ref)
    v_ref[...] = dot(wv_ref)
    qi_ref[...] = dot(wqi_ref) * (IDX_DIM ** -0.5)
    ki_ref[...] = _layer_norm(dot(wki_ref), kg_ref[...], kb_ref[...])
    wi_ref[...] = dot(wwi_ref) * (IDX_HEADS ** -0.5)
    z_ref[...] = _silu(dot(wz_ref))


def _at_in(x, w, tm):
    n = x.shape[0]
    row = lambda c: pl.BlockSpec((tm, c), lambda i: (i, 0))
    widths = [D_MODEL, AT_KV_DIM, AT_KV_DIM, IDX_HEADS * IDX_DIM, IDX_DIM, LANES, D_MODEL]
    return pl.pallas_call(
        _at_in_kernel, grid=(n // tm,),
        in_specs=[row(D_MODEL)] + [_const_spec(a.shape) for a in w],
        out_specs=[row(c) for c in widths],
        out_shape=[jax.ShapeDtypeStruct((n, c), F32) for c in widths],
        compiler_params=_params(1), name="at_in")(x, *w)


ALIBI_COLS = 6


def _split3_bf16(x):
    hi = x.astype(jnp.bfloat16).astype(np.float32)
    mid = (x - hi).astype(jnp.bfloat16).astype(np.float32)
    lo = (x - hi - mid).astype(jnp.bfloat16).astype(np.float32)
    return hi, mid, lo


def _alibi_query_columns():
    s = np.asarray(ALIBI_SLOPES, np.float32) * np.float32(np.log2(np.e))
    parts = _split3_bf16(s)
    out = np.zeros((AT_HEADS, LANES), np.float32)
    for i, part in enumerate(parts):
        out[:, AT_HEAD_DIM + i] = 16.0 * part
        out[:, AT_HEAD_DIM + 3 + i] = part
    return out.reshape(1, AT_HEADS * LANES)


def _slot_weights(w, n_slots):
    d = w.shape[0]
    w = w.reshape(d, n_slots, AT_HEAD_DIM)
    return jnp.pad(w, ((0, 0), (0, 0), (0, LANES - AT_HEAD_DIM))).reshape(d, n_slots * LANES)


def _at_in_prompt_kernel(x_ref, wqs_ref, wkt_ref, wvt_ref, wks_ref, wvs_ref, wqi_ref, wkit_ref, wwi_ref, wz_ref,
                         kg_ref, kb_ref, qcol_ref,
                         kt_ref, vt_ref, kit_ref, kib_ref, qi_ref, wi_ref, z_ref, qaug_ref, kaug_ref, vaug_ref,
                         *, tm, tiles_per_seq):
    xb = x_ref[...].astype(BF16)
    dot = lambda w: jnp.dot(xb, w[...], preferred_element_type=F32)
    dot_t = lambda wt: lax.dot_general(wt[...], xb, (((1,), (1,)), ((), ())), preferred_element_type=F32)
    kt_ref[...] = dot_t(wkt_ref)
    vt_ref[...] = dot_t(wvt_ref)
    kit = dot_t(wkit_ref)
    mu = jnp.mean(kit, axis=0, keepdims=True)
    kc = kit - mu
    var = jnp.mean(kc * kc, axis=0, keepdims=True)
    kit = kc * lax.rsqrt(var + LN_EPS) * kg_ref[...] + kb_ref[...]
    kit_ref[...] = kit
    kib_ref[...] = kit
    qi_ref[...] = dot(wqi_ref) * (IDX_DIM ** -0.5)
    wi_ref[...] = dot(wwi_ref) * (IDX_HEADS ** -0.5)
    z_ref[...] = _silu(dot(wz_ref))
    qaug_ref[...] = (dot(wqs_ref) * (AT_HEAD_DIM ** -0.5 * float(np.log2(np.e))) + qcol_ref[...]).astype(BF16)
    pos = (pl.program_id(0) % tiles_per_seq) * tm + lax.broadcasted_iota(jnp.int32, (tm, 1), 0)
    a = (pos >> 4).astype(F32)
    c = (pos & 15).astype(F32)
    col = lax.broadcasted_iota(jnp.int32, (1, AT_KV_HEADS * LANES), 1) & (LANES - 1)
    in_a = (col >= AT_HEAD_DIM) & (col < AT_HEAD_DIM + 3)
    in_c = (col >= AT_HEAD_DIM + 3) & (col < AT_HEAD_DIM + ALIBI_COLS)
    kaug_ref[...] = (dot(wks_ref) + jnp.where(in_a, a, jnp.where(in_c, c, 0.0))).astype(BF16)
    vaug_ref[...] = (dot(wvs_ref) + jnp.where(col == AT_HEAD_DIM, 1.0, 0.0)).astype(BF16)


def _at_in_prompt(x, w, batch, seq, tm):
    assert tm == KEY_BLOCK
    n = x.shape[0]
    tps = seq // tm
    row = lambda c: pl.BlockSpec((tm, c), lambda i: (i, 0))
    feat = lambda c: pl.BlockSpec((None, c, tm), lambda i: (i // tps, 0, i % tps))
    rows_out = [(IDX_HEADS * IDX_DIM, F32), (LANES, F32), (D_MODEL, F32), (AT_HEADS * LANES, BF16),
                (AT_KV_HEADS * LANES, BF16), (AT_KV_HEADS * LANES, BF16)]
    kern = functools.partial(_at_in_prompt_kernel, tm=tm, tiles_per_seq=tps)
    return pl.pallas_call(
        kern, grid=(n // tm,),
        in_specs=[row(D_MODEL)] + [_const_spec(a.shape) for a in w],
        out_specs=[feat(AT_KV_DIM), feat(AT_KV_DIM), feat(IDX_DIM),
                   pl.BlockSpec((None, None, IDX_DIM, tm), lambda i: (i // tps, i % tps, 0, 0))]
        + [row(c) for c, _ in rows_out],
        out_shape=[jax.ShapeDtypeStruct((batch, AT_KV_DIM, seq), F32), jax.ShapeDtypeStruct((batch, AT_KV_DIM, seq), F32),
                   jax.ShapeDtypeStruct((batch, IDX_DIM, seq), F32),
                   jax.ShapeDtypeStruct((batch, tps, IDX_DIM, tm), F32)]
        + [jax.ShapeDtypeStruct((n, c), dt) for c, dt in rows_out],
        compiler_params=_params(1), name="at_in_prompt")(x, *w)


def _page_dmas(pt_ref, seq0, n_seq, n_pages, src_hbm, dst_of, sem):
    out = []
    for g in range(n_seq):
        for p in range(n_pages):
            out.append(pltpu.make_async_copy(src_hbm.at[pt_ref[(seq0 + g) * n_pages + p]], dst_of(g, p), sem))
    return out


def _prefetch_step(dmas):
    i = pl.program_id(0)
    slot = i % 2

    @pl.when(i == 0)
    def _():
        for c in dmas(0, 0):
            c.start()

    @pl.when(i + 1 < pl.num_programs(0))
    def _():
        for c in dmas(i + 1, 1 - slot):
            c.start()
    for c in dmas(i, slot):
        c.wait()
    return slot


def _dsa_select_sample_kernel(pt_ref, qi_ref, wi_ref, kin_ref, kidx_hbm, bias_ref, kibuf, sem, idxbuf, keybuf,
                              *, g_seq, n_pages, page, steps, topk):
    past = n_pages * page
    s_pad = past + LANES
    n_tiles = s_pad // LANES
    rows = g_seq * 8
    slot = _prefetch_step(lambda step, sl: _page_dmas(
        pt_ref, step * g_seq, g_seq, n_pages, kidx_hbm,
        lambda g, p: kibuf.at[sl, g, :, pl.ds(p * page, page)], sem.at[sl]))

    idxbuf[...] = jnp.full((rows, s_pad), -jnp.inf, F32)
    tt = lax.broadcasted_iota(jnp.int32, (steps, 8), 0)
    jj = lax.broadcasted_iota(jnp.int32, (steps, 8), 1)
    for g in range(g_seq):
        qs = qi_ref[g].astype(BF16)
        w = wi_ref[g]
        sc = _bdot(qs, kibuf[slot, g])
        val = jnp.maximum(sc, 0.0) * jnp.concatenate([w] * (past // LANES), axis=1)
        idxbuf[g * 8:g * 8 + steps, 0:past] = jnp.sum(val.reshape(steps, IDX_HEADS, past), axis=1)
        scn = _bdot_nt(qs, kin_ref[g])
        valn = jnp.maximum(scn, 0.0) * w[:, 0:8]
        idn = jnp.sum(valn.reshape(steps, IDX_HEADS, 8), axis=1)
        idxbuf[g * 8:g * 8 + steps, past:past + 8] = jnp.where(jj <= tt, idn, -jnp.inf)

    bits = pltpu.bitcast(idxbuf[...], jnp.int32)
    keybuf[...] = jnp.where(bits < 0, bits ^ jnp.int32(0x7FFFFFFF), bits)

    def tile(c):
        return keybuf[:, c * LANES:(c + 1) * LANES]

    def bit_pass(b, carry):
        thr, cnt_gt = carry
        cand = thr + jnp.left_shift(jnp.int32(1), 31 - b)
        part = jnp.zeros((rows, LANES), F32)
        for c in range(n_tiles):
            part = part + jnp.where(tile(c) >= cand, 1.0, 0.0)
        total = jnp.sum(part, axis=1, keepdims=True)
        ok = total >= float(topk)
        return jnp.where(ok, cand, thr), jnp.where(ok, cnt_gt, total)
    thr, cnt_gt = lax.fori_loop(0, 32, bit_pass, (jnp.full((rows, 1), INT_MIN, jnp.int32),
                                                  jnp.zeros((rows, 1), F32)))
    need_eq = float(topk) - cnt_gt

    incl = jnp.where(lax.broadcasted_iota(jnp.int32, (LANES, LANES), 0)
                     <= lax.broadcasted_iota(jnp.int32, (LANES, LANES), 1), 1.0, 0.0).astype(BF16)
    eq_rows = jnp.concatenate([jnp.where(tile(c) == thr, 1.0, 0.0).astype(BF16) for c in range(n_tiles)], axis=0)
    pre = jnp.dot(eq_rows, incl, preferred_element_type=F32)
    run = jnp.zeros((rows, 1), F32)
    for c in range(n_tiles):
        key = tile(c)
        pc = pre[c * rows:(c + 1) * rows] + run
        sel = (key > thr) | ((key == thr) & (pc <= need_eq))
        idxbuf[:, c * LANES:(c + 1) * LANES] = jnp.where(sel, 0.0, NEG_BIG)
        run = pc[:, LANES - 1:LANES]
    for g in range(g_seq):
        bias_ref[g] = idxbuf[g * 8:(g + 1) * 8, :]


def _dsa_select_sample(page_table, qi_st, wi_col, ki_new, cache_ki, topk, g_seq):
    bs, n_pages = page_table.shape
    page = cache_ki.shape[2]
    past = n_pages * page
    s_pad = past + LANES
    rows_q = qi_st.shape[1]
    blk = lambda r, c: pl.BlockSpec((g_seq, r, c), lambda i, pt: (i, 0, 0))
    kern = functools.partial(_dsa_select_sample_kernel, g_seq=g_seq, n_pages=n_pages, page=page,
                             steps=rows_q // IDX_HEADS, topk=topk)
    return pl.pallas_call(
        kern,
        grid_spec=pltpu.PrefetchScalarGridSpec(
            num_scalar_prefetch=1, grid=(bs // g_seq,),
            in_specs=[blk(rows_q, IDX_DIM), blk(rows_q, LANES), blk(8, IDX_DIM), pl.BlockSpec(memory_space=pl.ANY)],
            out_specs=blk(8, s_pad),
            scratch_shapes=[pltpu.VMEM((2, g_seq, IDX_DIM, past), F32), pltpu.SemaphoreType.DMA((2,)),
                            pltpu.VMEM((g_seq * 8, s_pad), F32), pltpu.VMEM((g_seq * 8, s_pad), jnp.int32)]),
        out_shape=jax.ShapeDtypeStruct((bs, 8, s_pad), F32),
        compiler_params=_params(1), name="dsa_select_sample")(page_table.reshape(-1), qi_st, wi_col, ki_new, cache_ki)


def _dsa_attend_sample_kernel(pt_ref, qbd_ref, zbd_ref, bias_ref, knew_ref, vnew_ref, meta_ref, k_hbm, v_hbm,
                              out_ref, kbuf, vbuf, sem, *, n_pages, page, steps):
    past = n_pages * page

    def dmas(step, sl):
        return (_page_dmas(pt_ref, step, 1, n_pages, k_hbm, lambda g, p: kbuf.at[sl, :, pl.ds(p * page, page)],
                           sem.at[0, sl])
                + _page_dmas(pt_ref, step, 1, n_pages, v_hbm, lambda g, p: vbuf.at[sl, :, pl.ds(p * page, page)],
                             sem.at[1, sl]))
    slot = _prefetch_step(dmas)

    q = qbd_ref[0].astype(BF16)
    slope = meta_ref[:, 0:1]
    tq = meta_ref[:, 1:2]
    bias = jnp.concatenate([jnp.broadcast_to(bias_ref[0, t:t + 1, :], (AT_HEADS, past + LANES))
                            for t in range(steps)], axis=0)
    pos_old = lax.broadcasted_iota(jnp.int32, (1, past), 1).astype(F32)
    l_old = _bdot(q, kbuf[slot]) - slope * ((float(past) + tq) - pos_old) + bias[:, 0:past]
    pos_new = lax.broadcasted_iota(jnp.int32, (1, 8), 1).astype(F32)
    l_new = _bdot_nt(q, knew_ref[0]) - slope * (tq - pos_new) + bias[:, past:past + 8]
    m = jnp.maximum(jnp.max(l_old, axis=1, keepdims=True), jnp.max(l_new, axis=1, keepdims=True))
    p_old = jnp.exp(l_old - m)
    p_new = jnp.exp(l_new - m)
    den = jnp.sum(p_old, axis=1, keepdims=True) + jnp.sum(p_new, axis=1, keepdims=True)
    o = _bdot_nt(p_old, vbuf[slot]) + _bdot(p_new, vnew_ref[0])
    out_ref[0] = o / den * zbd_ref[0]


def _dsa_attend_sample(page_table, qbd, zbd, bias, k_new, v_new, meta, cache_k, cache_v):
    bs, n_pages = page_table.shape
    page = cache_k.shape[2]
    past = n_pages * page
    rows = qbd.shape[1]
    seq = lambda r, c: pl.BlockSpec((1, r, c), lambda i, pt: (i, 0, 0))
    hbm = pl.BlockSpec(memory_space=pl.ANY)
    kern = functools.partial(_dsa_attend_sample_kernel, n_pages=n_pages, page=page, steps=rows // AT_HEADS)
    return pl.pallas_call(
        kern,
        grid_spec=pltpu.PrefetchScalarGridSpec(
            num_scalar_prefetch=1, grid=(bs,),
            in_specs=[seq(rows, AT_KV_DIM), seq(rows, AT_KV_DIM), seq(8, past + LANES), seq(8, AT_KV_DIM),
                      seq(8, AT_KV_DIM), pl.BlockSpec(meta.shape, lambda i, pt: (0, 0)), hbm, hbm],
            out_specs=seq(rows, AT_KV_DIM),
            scratch_shapes=[pltpu.VMEM((2, AT_KV_DIM, past), F32), pltpu.VMEM((2, AT_KV_DIM, past), F32),
                            pltpu.SemaphoreType.DMA((2, 2))]),
        out_shape=jax.ShapeDtypeStruct((bs, rows, AT_KV_DIM), F32),
        compiler_params=_params(1), name="dsa_attend_sample")(
            page_table.reshape(-1), qbd, zbd, bias, k_new, v_new, meta, cache_k, cache_v)


def _dsa_kernel(qaug_ref, qi_ref, wi_ref, z_ref, kaug_ref, vaug_ref, ki_ref, act_ref,
                keys_ref, half_ref, bias_ref, mrun_ref, m_ref, acc_ref, p_ref, *, t, topk):
    kb_sz = KEY_BLOCK
    pos0 = pl.program_id(1) * t
    n_kb = (pos0 + t - 1) // kb_sz + 1
    rowpos = pos0 + lax.broadcasted_iota(jnp.int32, (t, 1), 0)
    lane = lax.broadcasted_iota(jnp.int32, (1, kb_sz), 1)

    def score_block(kb, carry):
        kib = ki_ref[kb].astype(BF16)
        acc = jnp.zeros((t, kb_sz), F32)
        for i in range(IDX_HEADS):
            sc = _bdot(qi_ref[:, i * IDX_DIM:(i + 1) * IDX_DIM], kib)
            acc = acc + jnp.maximum(sc, 0.0) * wi_ref[:, i:i + 1]
        acc = jnp.where(kb * kb_sz + lane <= rowpos, acc, -jnp.inf)
        bits = pltpu.bitcast(acc, jnp.int32)
        key = jnp.where(bits < 0, bits ^ jnp.int32(0x7FFFFFFF), bits)
        keys_ref[kb] = key
        half_ref[kb] = (key >> 16).astype(jnp.int16)
        return carry
    lax.fori_loop(0, n_kb, score_block, 0)

    half_min = -(1 << 15)

    def search_half(carry):
        def bit_pass(b, carry):
            res, cnt_gt = carry
            cand = res + jnp.left_shift(jnp.int32(1), 15 - b)
            cand_b = jnp.broadcast_to(cand, (t, LANES)).astype(jnp.int16)

            def body(kb, cnt):
                tile = half_ref[kb]
                for s in range(kb_sz // LANES):
                    cnt = cnt + jnp.where(tile[:, s * LANES:(s + 1) * LANES] >= cand_b, jnp.int16(1), jnp.int16(0))
                return cnt
            cnt = lax.fori_loop(0, n_kb, body, jnp.zeros((t, LANES), jnp.int16))
            total = jnp.sum(cnt.astype(F32), axis=1, keepdims=True)
            ok = total >= float(topk)
            return jnp.where(ok, cand, res), jnp.where(ok, cnt_gt, total)
        return lax.fori_loop(0, 16, bit_pass, carry)
    thr_hi, cnt_gt = search_half((jnp.full((t, 1), half_min, jnp.int32), jnp.zeros((t, 1), F32)))

    def low_half_block(kb, carry):
        key = keys_ref[kb]
        hi = key >> 16
        lo = (key & jnp.int32(0xFFFF)) + half_min
        half_ref[kb] = jnp.where(hi == thr_hi, lo, jnp.where(hi > thr_hi, -half_min - 1, half_min)).astype(jnp.int16)
        return carry
    lax.fori_loop(0, n_kb, low_half_block, 0)
    thr_lo, cnt_gt = search_half((jnp.full((t, 1), half_min, jnp.int32), cnt_gt))
    thr = (thr_hi << 16) + (thr_lo - half_min)
    need_eq = float(topk) - cnt_gt

    mrun_ref[...] = jnp.full(mrun_ref.shape, NEG_BIG, F32)
    incl = jnp.where(lax.broadcasted_iota(jnp.int32, (kb_sz, kb_sz), 0)
                     <= lax.broadcasted_iota(jnp.int32, (kb_sz, kb_sz), 1), 1.0, 0.0).astype(BF16)
    r_per_g = AT_HEADS // AT_KV_HEADS
    nt = (((1,), (1,)), ((), ()))

    def logits(h, kg, mask_bias):
        return lax.dot_general(qaug_ref[:, h * LANES:(h + 1) * LANES], kg, nt,
                               preferred_element_type=F32) + mask_bias

    def max_block(kb, run_eq):
        ks = pl.ds(pl.multiple_of(kb * kb_sz, kb_sz), kb_sz)
        key = keys_ref[kb]
        eq = key == thr
        pre = jnp.dot(jnp.where(eq, 1.0, 0.0).astype(BF16), incl, preferred_element_type=F32) + run_eq
        sel = ((key > thr) | (eq & (pre <= need_eq))) & (kb * kb_sz + lane <= rowpos)
        mask_bias = jnp.where(sel, 0.0, NEG_BIG)
        bias_ref[kb] = mask_bias
        for g in range(AT_KV_HEADS):
            kg = kaug_ref[ks, g * LANES:(g + 1) * LANES]
            for r in range(r_per_g):
                h = g * r_per_g + r
                s = logits(h, kg, mask_bias)
                tile_max = s[:, 0:LANES]
                for c in range(1, kb_sz // LANES):
                    tile_max = jnp.maximum(tile_max, s[:, c * LANES:(c + 1) * LANES])
                mrun_ref[h] = jnp.maximum(mrun_ref[h], tile_max)
        return pre[:, kb_sz - 1:kb_sz]
    lax.fori_loop(0, n_kb, max_block, jnp.zeros((t, 1), F32))
    for h in range(AT_HEADS):
        m_ref[h] = jnp.max(mrun_ref[h], axis=1, keepdims=True)

    acc_ref[...] = jnp.zeros(acc_ref.shape, F32)

    def attend_block(kb, carry):
        ks = pl.ds(pl.multiple_of(kb * kb_sz, kb_sz), kb_sz)
        mask_bias = bias_ref[kb]
        for g in range(AT_KV_HEADS):
            kg = kaug_ref[ks, g * LANES:(g + 1) * LANES]
            for r in range(r_per_g):
                h = g * r_per_g + r
                p_ref[h] = jnp.exp2(logits(h, kg, mask_bias) - m_ref[h]).astype(BF16)
        for g in range(AT_KV_HEADS):
            vg = vaug_ref[ks, g * LANES:(g + 1) * LANES]
            for r in range(r_per_g):
                h = g * r_per_g + r
                acc_ref[h] = acc_ref[h] + jnp.dot(p_ref[h], vg, preferred_element_type=F32)
        return carry
    lax.fori_loop(0, n_kb, attend_block, 0)

    for h in range(AT_HEADS):
        hs = slice(h * AT_HEAD_DIM, (h + 1) * AT_HEAD_DIM)
        a = acc_ref[h]
        o = a[:, 0:AT_HEAD_DIM] / a[:, AT_HEAD_DIM:AT_HEAD_DIM + 1]
        act_ref[:, hs] = (o * z_ref[:, hs]).astype(act_ref.dtype)


def _dsa(qaug, qi, wi, z, kaug, vaug, ki_all, batch, n_qb, t, s_len, topk):
    n = qaug.shape[0]
    qrow = lambda c: pl.BlockSpec((t, c), lambda b, j: (b * n_qb + j, 0))
    krow = lambda c: pl.BlockSpec((s_len, c), lambda b, j: (b, 0))
    n_kb = s_len // KEY_BLOCK
    kern = functools.partial(_dsa_kernel, t=t, topk=topk)
    return pl.pallas_call(
        kern, grid=(batch, n_qb),
        in_specs=[qrow(AT_HEADS * LANES), qrow(IDX_HEADS * IDX_DIM), qrow(LANES), qrow(D_MODEL),
                  krow(AT_KV_HEADS * LANES), krow(AT_KV_HEADS * LANES),
                  pl.BlockSpec((None, n_kb, IDX_DIM, KEY_BLOCK), lambda b, j: (b, 0, 0, 0))],
        out_specs=qrow(D_MODEL), out_shape=jax.ShapeDtypeStruct((n, D_MODEL), BF16),
        scratch_shapes=[pltpu.VMEM((n_kb, t, KEY_BLOCK), jnp.int32), pltpu.VMEM((n_kb, t, KEY_BLOCK), jnp.int16),
                        pltpu.VMEM((n_kb, t, KEY_BLOCK), F32),
                        pltpu.VMEM((AT_HEADS, t, LANES), F32), pltpu.VMEM((AT_HEADS, t, 1), F32),
                        pltpu.VMEM((AT_HEADS, t, LANES), F32), pltpu.VMEM((AT_HEADS, t, KEY_BLOCK), BF16)],
        compiler_params=_params(2), name="dsa")(qaug, qi, wi, z, kaug, vaug, ki_all)


def _gm_kernel(*refs, tm, chunk, emit_v):
    if emit_v:
        (x_ref, win_ref, bin_ref, g_ref, b_ref, wmix_ref, bmix_ref, act_ref, v_ref, ubuf, vbuf, zbuf) = refs
    else:
        (x_ref, win_ref, bin_ref, g_ref, b_ref, wmix_ref, bmix_ref, act_ref, ubuf, vbuf, zbuf) = refs
    xb = x_ref[...].astype(BF16)
    w = GM_WIDTH
    cols = 512
    for cc in range(w // cols):
        cs = slice(cc * cols, (cc + 1) * cols)
        proj = lambda off: (jnp.dot(xb, win_ref[:, off + cc * cols:off + (cc + 1) * cols],
                                    preferred_element_type=F32) + bin_ref[:, off + cc * cols:off + (cc + 1) * cols])
        ubuf[:, cs] = _gelu_tanh(proj(0))
        vbuf[:, cs] = _gelu_tanh(proj(w))
        zbuf[:, cs] = _silu(proj(2 * w))

    rows = 32

    def norm_rows(r, carry):
        rs = pl.ds(pl.multiple_of(r * rows, rows), rows)
        v = _layer_norm(vbuf[rs, :], g_ref[...], b_ref[...])
        vbuf[rs, :] = v
        if emit_v:
            v_ref[rs, :] = v
        return carry
    lax.fori_loop(0, tm // rows, norm_rows, 0)

    gw = w // GM_GROUPS
    for c in range(tm // chunk):
        rs = slice(c * chunk, (c + 1) * chunk)
        for g in range(GM_GROUPS):
            cs = slice(g * gw, (g + 1) * gw)
            s = jnp.dot(wmix_ref[g], vbuf[rs, cs].astype(BF16), preferred_element_type=F32) + bmix_ref[:, g:g + 1]
            act_ref[rs, cs] = (ubuf[rs, cs] * s * zbuf[rs, cs]).astype(act_ref.dtype)


def _gm(x, w, tm, chunk, emit_v):
    n = x.shape[0]
    row = lambda c: pl.BlockSpec((tm, c), lambda i: (i, 0))
    out_specs = [row(GM_WIDTH)]
    out_shape = [jax.ShapeDtypeStruct((n, GM_WIDTH), BF16)]
    if emit_v:
        out_specs.append(row(GM_WIDTH))
        out_shape.append(jax.ShapeDtypeStruct((n, GM_WIDTH), F32))
    kern = functools.partial(_gm_kernel, tm=tm, chunk=chunk, emit_v=emit_v)
    return pl.pallas_call(
        kern, grid=(n // tm,),
        in_specs=[row(D_MODEL)] + [_const_spec(a.shape) for a in w],
        out_specs=out_specs, out_shape=out_shape,
        scratch_shapes=[pltpu.VMEM((tm, GM_WIDTH), F32)] * 3,
        compiler_params=_params(1), name="gmlp")(x, *w)


def _pad_lanes(a, width=LANES):
    return jnp.pad(a, [(0, 0)] * (a.ndim - 1) + [(0, width - a.shape[-1])])


def kernel(x_prompt, x_sample, p_prompt, p_sample, state_cf_conv, state_mb_conv, state_mb_ssm,
           cache_k, cache_v, cache_kidx, page_table, post_ln_g, post_ln_b, ple_w, ple_gate_w,
           cf_w_in, cf_b_in, cf_w_dw, cf_b_dw, cf_ln_g, cf_ln_b, cf_w_out, cf_b_out,
           mb_w_in, mb_w_conv, mb_b_conv, mb_dt_bias, mb_a_log, mb_d_skip, mb_norm_g, mb_w_out,
           at_w_in, at_ki_ln_g, at_ki_ln_b, at_w_out,
           gm_w_in, gm_b_in, gm_ln_g, gm_ln_b, gm_w_s, gm_b_s, gm_w_out):
    bp, lp, d = x_prompt.shape
    bs, ls, _ = x_sample.shape
    row2 = lambda v: v.reshape(1, -1)
    bf = lambda v: v.astype(BF16)

    ple_wb, ple_gate_wb = bf(ple_w), bf(ple_gate_w)
    cf_w = (bf(cf_w_in), row2(cf_b_in), cf_w_dw, row2(cf_b_dw), row2(cf_ln_g), row2(cf_ln_b))
    mb_in_w = (bf(mb_w_in[:, :MB_INNER]), bf(mb_w_in[:, MB_INNER:MB_INNER + MB_CONV_DIM]),
               bf(_pad_lanes(mb_w_in[:, MB_INNER + MB_CONV_DIM:])), _pad_lanes(row2(mb_dt_bias)),
               mb_w_conv, row2(mb_b_conv))
    ssd_w = (_pad_lanes(row2(mb_a_log)), row2(jnp.repeat(mb_d_skip, MB_HEAD_DIM)), row2(mb_norm_g))
    sizes = [AT_HEADS * AT_HEAD_DIM, AT_KV_DIM, AT_KV_DIM, IDX_HEADS * IDX_DIM, IDX_DIM, IDX_HEADS,
             AT_HEADS * AT_HEAD_DIM]
    offs = np.concatenate([[0], np.cumsum(sizes)]).tolist()
    at_parts = [at_w_in[:, offs[i]:offs[i + 1]] for i in range(len(sizes))]
    at_parts[5] = _pad_lanes(at_parts[5])
    at_w = tuple(bf(a) for a in at_parts) + (row2(at_ki_ln_g), row2(at_ki_ln_b))
    wq_b, wk_b, wv_b, wqi_b, wki_b, wwi_b, wz_b = at_w[:7]
    at_w_prompt = (_slot_weights(wq_b, AT_HEADS), wk_b.T, wv_b.T, _slot_weights(wk_b, AT_KV_HEADS),
                   _slot_weights(wv_b, AT_KV_HEADS), wqi_b, wki_b.T, wwi_b, wz_b,
                   at_ki_ln_g.reshape(-1, 1), at_ki_ln_b.reshape(-1, 1), jnp.asarray(_alibi_query_columns()))
    gm_common = (bf(gm_w_in), row2(gm_b_in), row2(gm_ln_g), row2(gm_ln_b))
    zeros_d = jnp.zeros((1, d), F32)
    out_w = [(bf(cf_w_out), row2(cf_b_out)), (bf(mb_w_out), zeros_d), (bf(at_w_out), zeros_d),
             (bf(gm_w_out), zeros_d)]

    post_g, post_b = post_ln_g.reshape(DEPTH, 1, d), post_ln_b.reshape(DEPTH, 1, d)

    def tail(i, act, x, p, tm):
        wo, bo = out_w[i]
        return _tail(i, act, x, p, wo, bo, post_g, post_b, ple_gate_wb, ple_wb, tm)

    tm = 512
    n_p = bp * lp
    x = x_prompt.reshape(n_p, d)
    pp = p_prompt.reshape(DEPTH, n_p, D_PLE)

    act, cf_conv_p = _cf_prompt(x, cf_w, bp, lp, tm)
    x = tail(0, act, x, pp, tm)

    z, xbc, dt, mb_conv_p = _mb_in_prompt(x, mb_in_w, bp, lp, tm)
    act, mb_ssm_p = _ssd(xbc, dt, z, *ssd_w, None, bp, MB_CHUNK, MB_CHUNK, lp // MB_CHUNK, BF16)
    x = tail(1, act, x, pp, tm)

    kt, vt, kit, kib, qi, wi, zs, qaug, kaug, vaug = _at_in_prompt(x, at_w_prompt, bp, lp, tm)
    act = _dsa(qaug, qi, wi, zs, kaug, vaug, kib, bp, lp // DSA_ROWS, DSA_ROWS, lp, min(TOPK_MAX, lp // 4))
    x = tail(2, act, x, pp, tm)
    k_p = kt.reshape(bp, AT_KV_HEADS, AT_HEAD_DIM, lp).transpose(0, 3, 1, 2)
    v_p = vt.reshape(bp, AT_KV_HEADS, AT_HEAD_DIM, lp).transpose(0, 3, 1, 2)
    kidx_p = kit.transpose(0, 2, 1)

    gm_w_p = gm_common + (bf(jnp.tril(gm_w_s)), gm_b_s.T)
    (act,) = _gm(x, gm_w_p, tm, GM_CHUNK, False)
    y_prompt = tail(3, act, x, pp, tm).reshape(bp, lp, d)

    n_s = bs * ls
    bb = 32
    to_tm = lambda a: jnp.swapaxes(a, 0, 1)
    x_tm = to_tm(x_sample)
    x = x_tm.reshape(n_s, d)
    ps = jnp.swapaxes(p_sample, 1, 2).reshape(DEPTH, n_s, D_PLE)
    tm_s = n_s

    act, cf_st = _cf_sample(x_tm, to_tm(state_cf_conv), cf_w, bb)
    cf_conv_s = to_tm(cf_st)
    x = tail(0, act.reshape(n_s, d), x, ps, tm_s)

    z, xbc, dt, mb_st = _mb_in_sample(x.reshape(ls, bs, d), to_tm(state_mb_conv), mb_in_w, bb)
    mb_conv_s = to_tm(mb_st)
    qs = 8

    def to_bm(a):
        a = jnp.pad(to_tm(a), ((0, 0), (0, qs - ls), (0, 0)))
        return a.reshape(bs * qs, a.shape[-1])

    def from_bm(a):
        a = a.reshape(bs, qs, a.shape[-1])[:, :ls]
        return to_tm(a).reshape(n_s, a.shape[-1])

    act, mb_ssm_s = _ssd(to_bm(xbc), to_bm(dt), to_bm(z), *ssd_w, state_mb_ssm, bs, qs, ls, 1, F32)
    x = tail(1, from_bm(act), x, ps, tm_s)

    q, k, v, qi, ki, wi, zs = _at_in(x, at_w, tm_s)
    k_s = to_tm(k.reshape(ls, bs, AT_KV_HEADS, AT_HEAD_DIM))
    v_s = to_tm(v.reshape(ls, bs, AT_KV_HEADS, AT_HEAD_DIM))
    kidx_s = to_tm(ki.reshape(ls, bs, IDX_DIM))
    n_pool, page = cache_k.shape[:2]
    past = page_table.shape[1] * page
    seq_major = lambda a: to_tm(a.reshape(ls, bs, a.shape[-1]))
    r_per_g = AT_HEADS // AT_KV_HEADS
    group_of_head = (np.arange(AT_HEADS)[:, None] // r_per_g == np.arange(AT_KV_HEADS)[None, :]).astype(np.float32)

    def block_diag(a):
        a = a.reshape(bs, ls, AT_HEADS, 1, AT_HEAD_DIM) * group_of_head[None, None, :, :, None]
        return a.reshape(bs, ls * AT_HEADS, AT_KV_DIM)

    pad_steps = lambda a: jnp.pad(a, ((0, 0), (0, 8 - ls), (0, 0)))
    qi_st = seq_major(qi).reshape(bs, ls * IDX_HEADS, IDX_DIM)
    wi_col = jnp.broadcast_to(seq_major(wi)[:, :, :IDX_HEADS].reshape(bs, ls * IDX_HEADS, 1),
                              (bs, ls * IDX_HEADS, LANES))
    pages_t = lambda c: jnp.moveaxis(c, 1, -1).reshape(n_pool, -1, page)
    bias = _dsa_select_sample(page_table, qi_st, wi_col, pad_steps(seq_major(ki)), pages_t(cache_kidx),
                              min(TOPK_MAX, (past + ls) // 4), 8)
    meta = np.zeros((ls * AT_HEADS, LANES), np.float32)
    meta[:, 0] = np.tile(np.asarray(ALIBI_SLOPES, np.float32), ls)
    meta[:, 1] = np.repeat(np.arange(ls, dtype=np.float32), AT_HEADS)
    o = _dsa_attend_sample(page_table, block_diag(seq_major(q)), block_diag(seq_major(zs)), bias,
                           pad_steps(seq_major(k)), pad_steps(seq_major(v)), jnp.asarray(meta),
                           pages_t(cache_k), pages_t(cache_v))
    act = o.reshape(bs, ls, AT_HEADS, AT_KV_HEADS, AT_HEAD_DIM).sum(axis=3).reshape(bs, ls, d)
    x = tail(2, to_tm(act).reshape(n_s, d), x, ps, tm_s)

    mix = jnp.stack([jnp.kron(jnp.tril(gm_w_s[g, :ls, :ls]), jnp.eye(bs, dtype=F32)) for g in range(GM_GROUPS)])
    gm_w_smp = gm_common + (bf(mix), jnp.repeat(gm_b_s[:, :ls].T, bs, axis=0))
    act, gm_v = _gm(x, gm_w_smp, tm_s, n_s, True)
    y_s = tail(3, act, x, ps, tm_s)
    y_sample = to_tm(y_s.reshape(ls, bs, d))
    gm_v_s = to_tm(gm_v.reshape(ls, bs, GM_WIDTH))

    return (y_prompt, y_sample, cf_conv_p, cf_conv_s, mb_conv_p, mb_conv_s, mb_ssm_p, mb_ssm_s,
            k_p, v_p, kidx_p, k_s, v_s, kidx_s, gm_v_s)
```

```python
import functools

import numpy as np
import jax
import jax.numpy as jnp
from jax import lax
from jax.experimental import pallas as pl
from jax.experimental.pallas import tpu as pltpu

F32 = jnp.float32
BF16 = jnp.bfloat16

D_MODEL = 1024
D_PLE = 256
DEPTH = 4
ALPHA_DN = (2 * DEPTH) ** 0.25
LN_EPS = 1e-5

CF_KERNEL = 31
CF_HIST = CF_KERNEL - 1
MB_INNER = 2048
MB_HEAD_DIM = 64
MB_HEADS = 32
MB_GROUPS = 8
MB_STATE = 128
MB_CONV = 4
MB_CONV_DIM = 4096
MB_CHUNK = 128
AT_HEADS = 16
AT_HEAD_DIM = 64
AT_KV_HEADS = 4
AT_KV_DIM = AT_KV_HEADS * AT_HEAD_DIM
IDX_HEADS = 8
IDX_DIM = 64
TOPK_MAX = 256
DSA_ROWS = 256
GM_WIDTH = 2048
GM_GROUPS = 4
GM_CHUNK = 128

LANES = 128
KEY_BLOCK = 512
NEG_BIG = -1e30
INT_MIN = -(2 ** 31)
VMEM_LIMIT = 56 * 1024 * 1024

ALIBI_SLOPES = [float(s) for s in
                (np.float32(2.0) ** (-8.0 * np.arange(1, AT_HEADS + 1, dtype=np.float32) / AT_HEADS))]


def _bdot(a, b):
    return jnp.dot(a.astype(BF16), b.astype(BF16), preferred_element_type=F32)


def _bdot_nt(a, b):
    return lax.dot_general(a.astype(BF16), b.astype(BF16), (((1,), (1,)), ((), ())),
                           preferred_element_type=F32)


def _bdot_tn(a, b):
    return lax.dot_general(a.astype(BF16), b.astype(BF16), (((0,), (0,)), ((), ())),
                           preferred_element_type=F32)


def _sigmoid(x):
    return 1.0 / (1.0 + jnp.exp(-x))


def _silu(x):
    return x * _sigmoid(x)


def _gelu_tanh(x):
    return x * (0.5 * (1.0 + jnp.tanh(np.sqrt(2.0 / np.pi).astype(np.float32) * (x + 0.044715 * (x * x * x)))))


def _softplus(x):
    return jnp.maximum(x, 0.0) + jnp.log1p(jnp.exp(-jnp.abs(x)))


def _layer_norm(x, g, b):
    mu = jnp.mean(x, axis=-1, keepdims=True)
    xc = x - mu
    var = jnp.mean(xc * xc, axis=-1, keepdims=True)
    return xc * lax.rsqrt(var + LN_EPS) * g + b


def _const_spec(shape):
    nd = len(shape)
    return pl.BlockSpec(shape, lambda *_: (0,) * nd)


def _params(n_axes):
    return pltpu.CompilerParams(dimension_semantics=("arbitrary",) * n_axes,
                                vmem_limit_bytes=VMEM_LIMIT)


def _tail_kernel(act_ref, x_ref, p_ref, wo_ref, bo_ref, g_ref, b_ref, wg_ref, wp_ref, o_ref):
    out = _bdot(act_ref[...], wo_ref[...]) + bo_ref[...]
    h = _layer_norm(ALPHA_DN * x_ref[...] + out, g_ref[...], b_ref[...])
    gate = _sigmoid(_bdot(h, wg_ref[...]))
    o_ref[...] = h + gate * _bdot(p_ref[...], wp_ref[...])


def _tail(layer, act, x, p, wo, bo, g, b, wg, wp, tm):
    n, k = act.shape
    row = lambda c: pl.BlockSpec((tm, c), lambda i: (i, 0))
    of_layer = lambda a: pl.BlockSpec((None,) + a.shape[1:], lambda i: (layer,) + (0,) * (a.ndim - 1))
    return pl.pallas_call(
        _tail_kernel, grid=(n // tm,),
        in_specs=[row(k), row(D_MODEL), pl.BlockSpec((None, tm, D_PLE), lambda i: (layer, i, 0)),
                  _const_spec(wo.shape), _const_spec(bo.shape),
                  of_layer(g), of_layer(b), of_layer(wg), of_layer(wp)],
        out_specs=row(D_MODEL), out_shape=jax.ShapeDtypeStruct((n, D_MODEL), F32),
        compiler_params=_params(1), name="layer_tail")(act, x, p, wo, bo, g, b, wg, wp)


def _cf_project(x2d, win_ref, bin_ref):
    xb = x2d.astype(BF16)
    d = D_MODEL
    a = jnp.dot(xb, win_ref[:, 0:d], preferred_element_type=F32) + bin_ref[:, 0:d]
    gl = jnp.dot(xb, win_ref[:, d:2 * d], preferred_element_type=F32) + bin_ref[:, d:2 * d]
    z = jnp.dot(xb, win_ref[:, 2 * d:3 * d], preferred_element_type=F32) + bin_ref[:, 2 * d:3 * d]
    return a * _sigmoid(gl), _silu(z)


def _cf_finish(cbuf, zbuf, g_ref, b_ref, store, n_rows, chunk):
    def body(r, carry):
        rs = pl.ds(pl.multiple_of(r * chunk, chunk), chunk)
        c = _silu(_layer_norm(cbuf[rs, :], g_ref[...], b_ref[...]))
        store(r, rs, (c * zbuf[rs, :]).astype(BF16))
        return carry
    lax.fori_loop(0, n_rows // chunk, body, 0, unroll=2)


def _cf_prompt_kernel(x_ref, win_ref, bin_ref, wdw_ref, bdw_ref, g_ref, b_ref, act_ref, st_ref,
                      ubuf, zbuf, cbuf, *, tm, tiles_per_seq):
    head = 32
    i = pl.program_id(0)
    first = (i % tiles_per_seq) == 0

    @pl.when(first)
    def _():
        ubuf[0:head, :] = jnp.zeros((head, D_MODEL), F32)

    @pl.when(jnp.logical_not(first))
    def _():
        ubuf[0:head, :] = ubuf[tm:tm + head, :]

    u, zs = _cf_project(x_ref[...], win_ref, bin_ref)
    ubuf[head:head + tm, :] = u
    zbuf[...] = zs
    off = head - CF_HIST
    rows, cols = 64, 256
    for rc in range(tm // rows):
        for cc in range(D_MODEL // cols):
            cs = slice(cc * cols, (cc + 1) * cols)
            acc = jnp.broadcast_to(bdw_ref[:, cs], (rows, cols))
            for b in range(8):
                span = rows + (8 if b else 0)
                part = None
                for a in range((off + CF_KERNEL + 7) // 8):
                    j = 8 * a + b - off
                    if 0 <= j < CF_KERNEL:
                        r0 = rc * rows + 8 * a
                        term = wdw_ref[j:j + 1, cs] * ubuf[r0:r0 + span, cs]
                        part = term if part is None else part + term
                acc = acc + part[b:b + rows, :]
            cbuf[rc * rows:(rc + 1) * rows, cs] = acc
    def store(r, rs, val):
        act_ref[rs, :] = val
    _cf_finish(cbuf, zbuf, g_ref, b_ref, store, tm, 64)

    @pl.when((i % tiles_per_seq) == tiles_per_seq - 1)
    def _():
        st_ref[0] = ubuf[head + tm - CF_HIST:head + tm, :]


def _cf_prompt(x, w, batch, seq, tm):
    n = x.shape[0]
    tps = seq // tm
    row = lambda c: pl.BlockSpec((tm, c), lambda i: (i, 0))
    kern = functools.partial(_cf_prompt_kernel, tm=tm, tiles_per_seq=tps)
    return pl.pallas_call(
        kern, grid=(n // tm,),
        in_specs=[row(D_MODEL)] + [_const_spec(a.shape) for a in w],
        out_specs=[row(D_MODEL), pl.BlockSpec((1, CF_HIST, D_MODEL), lambda i: (i // tps, 0, 0))],
        out_shape=[jax.ShapeDtypeStruct((n, D_MODEL), BF16),
                   jax.ShapeDtypeStruct((batch, CF_HIST, D_MODEL), F32)],
        scratch_shapes=[pltpu.VMEM((tm + 32, D_MODEL), F32), pltpu.VMEM((tm, D_MODEL), F32),
                        pltpu.VMEM((tm, D_MODEL), F32)],
        compiler_params=_params(1), name="cf_prompt")(x, *w)


def _cf_sample_kernel(x_ref, hist_ref, win_ref, bin_ref, wdw_ref, bdw_ref, g_ref, b_ref, act_ref, st_ref,
                      ubuf, zbuf, cbuf, *, steps, bb):
    n = steps * bb
    u, zs = _cf_project(x_ref[...].reshape(n, D_MODEL), win_ref, bin_ref)
    zbuf[...] = zs
    ubuf[0:CF_HIST] = hist_ref[...]
    for t in range(steps):
        ubuf[CF_HIST + t] = u[t * bb:(t + 1) * bb, :]
    cols = 512
    for t in range(steps):
        for cc in range(D_MODEL // cols):
            cs = slice(cc * cols, (cc + 1) * cols)
            acc = jnp.broadcast_to(bdw_ref[:, cs], (bb, cols))
            for j in range(CF_KERNEL):
                acc = acc + wdw_ref[j:j + 1, cs] * ubuf[t + j, :, cs]
            cbuf[t * bb:(t + 1) * bb, cs] = acc
    def store(r, rs, val):
        act_ref[r] = val
    _cf_finish(cbuf, zbuf, g_ref, b_ref, store, n, bb)
    st_ref[...] = ubuf[steps:steps + CF_HIST]


def _cf_sample(x_tm, hist_tm, w, bb):
    steps, batch, _ = x_tm.shape
    kern = functools.partial(_cf_sample_kernel, steps=steps, bb=bb)
    blk = lambda t, c: pl.BlockSpec((t, bb, c), lambda i: (0, i, 0))
    return pl.pallas_call(
        kern, grid=(batch // bb,),
        in_specs=[blk(steps, D_MODEL), blk(CF_HIST, D_MODEL)] + [_const_spec(a.shape) for a in w],
        out_specs=[blk(steps, D_MODEL), blk(CF_HIST, D_MODEL)],
        out_shape=[jax.ShapeDtypeStruct((steps, batch, D_MODEL), BF16),
                   jax.ShapeDtypeStruct((CF_HIST, batch, D_MODEL), F32)],
        scratch_shapes=[pltpu.VMEM((CF_HIST + steps, bb, D_MODEL), F32),
                        pltpu.VMEM((steps * bb, D_MODEL), F32), pltpu.VMEM((steps * bb, D_MODEL), F32)],
        compiler_params=_params(1), name="cf_sample")(x_tm, hist_tm, *w)


def _mb_project(x2d, wz_ref, wxbc_ref, wdt_ref, dtb_ref, store_z, store_dt, store_xbc):
    xb = x2d.astype(BF16)
    store_z(jnp.dot(xb, wz_ref[...], preferred_element_type=F32))
    store_dt(_softplus(jnp.dot(xb, wdt_ref[...], preferred_element_type=F32) + dtb_ref[...]))
    cols = 1024
    for cc in range(MB_CONV_DIM // cols):
        cs = slice(cc * cols, (cc + 1) * cols)
        store_xbc(cs, jnp.dot(xb, wxbc_ref[:, cs], preferred_element_type=F32))


def _mb_in_prompt_kernel(x_ref, wz_ref, wxbc_ref, wdt_ref, dtb_ref, cw_ref, cb_ref,
                         z_ref, xbc_ref, dt_ref, st_ref, xbuf, *, tm, tiles_per_seq):
    head = 8
    i = pl.program_id(0)
    first = (i % tiles_per_seq) == 0

    @pl.when(first)
    def _():
        xbuf[0:head, :] = jnp.zeros((head, MB_CONV_DIM), F32)

    @pl.when(jnp.logical_not(first))
    def _():
        xbuf[0:head, :] = xbuf[tm:tm + head, :]

    def store_xbc(cs, val):
        xbuf[head:head + tm, cs] = val
    def store_z(val):
        z_ref[...] = val

    def store_dt(val):
        dt_ref[...] = val
    _mb_project(x_ref[...], wz_ref, wxbc_ref, wdt_ref, dtb_ref, store_z, store_dt, store_xbc)
    off = head - (MB_CONV - 1)
    rows, cols = 32, 512
    for rc in range(tm // rows):
        for cc in range(MB_CONV_DIM // cols):
            cs = slice(cc * cols, (cc + 1) * cols)
            win = xbuf[rc * rows:rc * rows + rows + head, cs]
            acc = jnp.broadcast_to(cb_ref[:, cs], (rows, cols))
            for j in range(MB_CONV):
                acc = acc + cw_ref[j:j + 1, cs] * win[off + j:off + j + rows, :]
            xbc_ref[rc * rows:(rc + 1) * rows, cs] = _silu(acc)

    @pl.when((i % tiles_per_seq) == tiles_per_seq - 1)
    def _():
        st_ref[0] = xbuf[head + tm - (MB_CONV - 1):head + tm, :]


def _mb_in_prompt(x, w, batch, seq, tm):
    n = x.shape[0]
    tps = seq // tm
    row = lambda c: pl.BlockSpec((tm, c), lambda i: (i, 0))
    kern = functools.partial(_mb_in_prompt_kernel, tm=tm, tiles_per_seq=tps)
    return pl.pallas_call(
        kern, grid=(n // tm,),
        in_specs=[row(D_MODEL)] + [_const_spec(a.shape) for a in w],
        out_specs=[row(MB_INNER), row(MB_CONV_DIM), row(LANES),
                   pl.BlockSpec((1, MB_CONV - 1, MB_CONV_DIM), lambda i: (i // tps, 0, 0))],
        out_shape=[jax.ShapeDtypeStruct((n, MB_INNER), F32), jax.ShapeDtypeStruct((n, MB_CONV_DIM), F32),
                   jax.ShapeDtypeStruct((n, LANES), F32),
                   jax.ShapeDtypeStruct((batch, MB_CONV - 1, MB_CONV_DIM), F32)],
        scratch_shapes=[pltpu.VMEM((tm + 8, MB_CONV_DIM), F32)],
        compiler_params=_params(1), name="mb_in_prompt")(x, *w)


def _mb_in_sample_kernel(x_ref, hist_ref, wz_ref, wxbc_ref, wdt_ref, dtb_ref, cw_ref, cb_ref,
                         z_ref, xbc_ref, dt_ref, st_ref, xbuf, *, steps, bb):
    n = steps * bb
    hist = MB_CONV - 1
    xbuf[0:hist] = hist_ref[...]

    def store_xbc(cs, val):
        for t in range(steps):
            xbuf[hist + t, :, cs] = val[t * bb:(t + 1) * bb, :]
    def store_z(val):
        for t in range(steps):
            z_ref[t] = val[t * bb:(t + 1) * bb, :]

    def store_dt(val):
        for t in range(steps):
            dt_ref[t] = val[t * bb:(t + 1) * bb, :]
    _mb_project(x_ref[...].reshape(n, D_MODEL), wz_ref, wxbc_ref, wdt_ref, dtb_ref,
                store_z, store_dt, store_xbc)
    cols = 512
    for t in range(steps):
        for cc in range(MB_CONV_DIM // cols):
            cs = slice(cc * cols, (cc + 1) * cols)
            acc = jnp.broadcast_to(cb_ref[:, cs], (bb, cols))
            for j in range(MB_CONV):
                acc = acc + cw_ref[j:j + 1, cs] * xbuf[t + j, :, cs]
            xbc_ref[t, :, cs] = _silu(acc)
    st_ref[...] = xbuf[steps:steps + hist]


def _mb_in_sample(x_tm, hist_tm, w, bb):
    steps, batch, _ = x_tm.shape
    hist = MB_CONV - 1
    kern = functools.partial(_mb_in_sample_kernel, steps=steps, bb=bb)
    blk = lambda t, c: pl.BlockSpec((t, bb, c), lambda i: (0, i, 0))
    return pl.pallas_call(
        kern, grid=(batch // bb,),
        in_specs=[blk(steps, D_MODEL), blk(hist, MB_CONV_DIM)] + [_const_spec(a.shape) for a in w],
        out_specs=[blk(steps, MB_INNER), blk(steps, MB_CONV_DIM), blk(steps, LANES), blk(hist, MB_CONV_DIM)],
        out_shape=[jax.ShapeDtypeStruct((steps, batch, MB_INNER), F32),
                   jax.ShapeDtypeStruct((steps, batch, MB_CONV_DIM), F32),
                   jax.ShapeDtypeStruct((steps, batch, LANES), F32),
                   jax.ShapeDtypeStruct((hist, batch, MB_CONV_DIM), F32)],
        scratch_shapes=[pltpu.VMEM((hist + steps, bb, MB_CONV_DIM), F32)],
        compiler_params=_params(1), name="mb_in_sample")(x_tm, hist_tm, *w)


def _ssd_kernel(*refs, q, valid, n_chunks, has_s0):
    if has_s0:
        (xbc_x, xbc_b, xbc_c, dt_ref, z_ref, alog_ref, dskip_ref, ng_ref, s0_ref,
         act_ref, sfin_ref, s_ref, ybuf) = refs
    else:
        (xbc_x, xbc_b, xbc_c, dt_ref, z_ref, alog_ref, dskip_ref, ng_ref,
         act_ref, sfin_ref, s_ref, ybuf) = refs
    c = pl.program_id(1)
    r_per_g = MB_HEADS // MB_GROUPS
    gw = r_per_g * MB_HEAD_DIM

    @pl.when(c == 0)
    def _():
        if has_s0:
            for g in range(MB_GROUPS):
                s_ref[g] = s0_ref[0, g * r_per_g:(g + 1) * r_per_g].reshape(gw, MB_STATE).T
        else:
            s_ref[...] = jnp.zeros(s_ref.shape, F32)

    dt = dt_ref[...]
    if valid < q:
        dt = jnp.where(lax.broadcasted_iota(jnp.int32, (q, LANES), 0) < valid, dt, 0.0)
    a = dt * (-jnp.exp(alog_ref[...]))
    rid = lax.broadcasted_iota(jnp.int32, (q, q), 0)
    cid = lax.broadcasted_iota(jnp.int32, (q, q), 1)
    tri = rid >= cid
    lower = jnp.where(tri, 1.0, 0.0)
    upper = jnp.where(rid <= cid, 1.0, 0.0)
    eye = jnp.where(rid == cid, 1.0, 0.0)
    hi = lax.Precision.HIGHEST
    cum = jnp.dot(lower, a, precision=hi, preferred_element_type=F32)
    tn = (((0,), (0,)), ((), ()))
    cum_t = lax.dot_general(a, upper, tn, precision=hi, preferred_element_type=F32)
    dt_t = lax.dot_general(dt, eye, tn, precision=hi, preferred_element_type=F32)

    def hi_lo(v):
        v_hi = v.astype(BF16)
        return jnp.concatenate([v_hi, (v - v_hi.astype(F32)).astype(BF16)], axis=1)
    ecum_hl = hi_lo(jnp.exp(cum))
    wend_hl = hi_lo(jnp.exp(cum[q - 1:q, :] - cum) * dt)
    src_head = lax.broadcasted_iota(jnp.int32, (2 * LANES, gw), 0) & (LANES - 1)
    dst_head = lax.broadcasted_iota(jnp.int32, (2 * LANES, gw), 1) // MB_HEAD_DIM
    lane_head = lax.broadcasted_iota(jnp.int32, (1, gw), 1) // MB_HEAD_DIM
    for g in range(MB_GROUPS):
        gs = slice(g * MB_STATE, (g + 1) * MB_STATE)
        xs = slice(g * gw, (g + 1) * gw)
        cg = xbc_c[:, gs].astype(BF16)
        bg = xbc_b[:, gs].astype(BF16)
        cb = _bdot_nt(cg, bg)
        spread = jnp.where(src_head == g * r_per_g + dst_head, 1.0, 0.0).astype(BF16)
        ecum_g = jnp.dot(ecum_hl, spread, preferred_element_type=F32)
        wend_g = jnp.dot(wend_hl, spread, preferred_element_type=F32)
        xg = xbc_x[:, xs]
        st = s_ref[g]
        y = _bdot(cg, st) * ecum_g + dskip_ref[:, xs] * xg
        for r in range(r_per_g):
            h = g * r_per_g + r
            decay = jnp.exp(jnp.where(tri, cum[:, h:h + 1] - cum_t[h:h + 1, :], -jnp.inf))
            m = cb * decay * dt_t[h:h + 1, :]
            y = y + _bdot(m, jnp.where(lane_head == r, xg, 0.0))
        ybuf[:, xs] = y
        s_ref[g] = ecum_g[q - 1:q, :] * st + _bdot_tn(bg, xg * wend_g)
    y = ybuf[...] * _silu(z_ref[...])
    y = y * lax.rsqrt(jnp.mean(y * y, axis=-1, keepdims=True) + LN_EPS) * ng_ref[...]
    act_ref[...] = y.astype(act_ref.dtype)

    @pl.when(c == n_chunks - 1)
    def _():
        for g in range(MB_GROUPS):
            sfin_ref[0, g * r_per_g:(g + 1) * r_per_g] = s_ref[g].T.reshape(r_per_g, MB_HEAD_DIM, MB_STATE)


def _ssd(xbc, dt, z, alog, dskip, ng, s0, batch, q, valid, n_chunks, act_dtype):
    n = xbc.shape[0]
    has_s0 = s0 is not None
    rowblk = lambda cols, cb: pl.BlockSpec((q, cols), lambda b, c: (b * n_chunks + c, cb))
    st_spec = pl.BlockSpec((1, MB_HEADS, MB_HEAD_DIM, MB_STATE), lambda b, c: (b, 0, 0, 0))
    in_specs = [rowblk(MB_INNER, 0), rowblk(MB_GROUPS * MB_STATE, 2), rowblk(MB_GROUPS * MB_STATE, 3),
                rowblk(LANES, 0), rowblk(MB_INNER, 0),
                _const_spec(alog.shape), _const_spec(dskip.shape), _const_spec(ng.shape)]
    args = [xbc, xbc, xbc, dt, z, alog, dskip, ng]
    if has_s0:
        in_specs.append(st_spec)
        args.append(s0)
    kern = functools.partial(_ssd_kernel, q=q, valid=valid, n_chunks=n_chunks, has_s0=has_s0)
    return pl.pallas_call(
        kern, grid=(batch, n_chunks), in_specs=in_specs,
        out_specs=[rowblk(MB_INNER, 0), st_spec],
        out_shape=[jax.ShapeDtypeStruct((n, MB_INNER), act_dtype),
                   jax.ShapeDtypeStruct((batch, MB_HEADS, MB_HEAD_DIM, MB_STATE), F32)],
        scratch_shapes=[pltpu.VMEM((MB_GROUPS, MB_STATE, MB_INNER // MB_GROUPS), F32),
                        pltpu.VMEM((q, MB_INNER), F32)],
        compiler_params=_params(2), name="ssd")(*args)


def _at_in_kernel(x_ref, wq_ref, wk_ref, wv_ref, wqi_ref, wki_ref, wwi_ref, wz_ref, kg_ref, kb_ref,
                  q_ref, k_ref, v_ref, qi_ref, ki_ref, wi_ref, z_ref):
    xb = x_ref[...].astype(BF16)
    dot = lambda w: jnp.dot(xb, w[...], preferred_element_type=F32)
    q_ref[...] = dot(wq_ref) * (AT_HEAD_DIM ** -0.5)
    k_ref[...] = dot(wk_ref)
    v_ref[...] = dot(wv_ref)
    qi_ref[...] = dot(wqi_ref) * (IDX_DIM ** -0.5)
    ki_ref[...] = _layer_norm(dot(wki_ref), kg_ref[...], kb_ref[...])
    wi_ref[...] = dot(wwi_ref) * (IDX_HEADS ** -0.5)
    z_ref[...] = _silu(dot(wz_ref))


def _at_in(x, w, tm):
    n = x.shape[0]
    row = lambda c: pl.BlockSpec((tm, c), lambda i: (i, 0))
    widths = [D_MODEL, AT_KV_DIM, AT_KV_DIM, IDX_HEADS * IDX_DIM, IDX_DIM, LANES, D_MODEL]
    return pl.pallas_call(
        _at_in_kernel, grid=(n // tm,),
        in_specs=[row(D_MODEL)] + [_const_spec(a.shape) for a in w],
        out_specs=[row(c) for c in widths],
        out_shape=[jax.ShapeDtypeStruct((n, c), F32) for c in widths],
        compiler_params=_params(1), name="at_in")(x, *w)


ALIBI_COLS = 6


def _split3_bf16(x):
    hi = x.astype(jnp.bfloat16).astype(np.float32)
    mid = (x - hi).astype(jnp.bfloat16).astype(np.float32)
    lo = (x - hi - mid).astype(jnp.bfloat16).astype(np.float32)
    return hi, mid, lo


def _alibi_query_columns():
    s = np.asarray(ALIBI_SLOPES, np.float32) * np.float32(np.log2(np.e))
    parts = _split3_bf16(s)
    out = np.zeros((AT_HEADS, LANES), np.float32)
    for i, part in enumerate(parts):
        out[:, AT_HEAD_DIM + i] = 16.0 * part
        out[:, AT_HEAD_DIM + 3 + i] = part
    return out.reshape(1, AT_HEADS * LANES)


def _slot_weights(w, n_slots):
    d = w.shape[0]
    w = w.reshape(d, n_slots, AT_HEAD_DIM)
    return jnp.pad(w, ((0, 0), (0, 0), (0, LANES - AT_HEAD_DIM))).reshape(d, n_slots * LANES)


def _at_in_prompt_kernel(x_ref, wqs_ref, wkt_ref, wvt_ref, wks_ref, wvs_ref, wqi_ref, wkit_ref, wwi_ref, wz_ref,
                         kg_ref, kb_ref, qcol_ref,
                         kt_ref, vt_ref, kit_ref, kib_ref, qi_ref, wi_ref, z_ref, qaug_ref, kaug_ref, vaug_ref,
                         *, tm, tiles_per_seq):
    xb = x_ref[...].astype(BF16)
    dot = lambda w: jnp.dot(xb, w[...], preferred_element_type=F32)
    dot_t = lambda wt: lax.dot_general(wt[...], xb, (((1,), (1,)), ((), ())), preferred_element_type=F32)
    kt_ref[...] = dot_t(wkt_ref)
    vt_ref[...] = dot_t(wvt_ref)
    kit = dot_t(wkit_ref)
    mu = jnp.mean(kit, axis=0, keepdims=True)
    kc = kit - mu
    var = jnp.mean(kc * kc, axis=0, keepdims=True)
    kit = kc * lax.rsqrt(var + LN_EPS) * kg_ref[...] + kb_ref[...]
    kit_ref[...] = kit
    kib_ref[...] = kit
    qi_ref[...] = dot(wqi_ref) * (IDX_DIM ** -0.5)
    wi_ref[...] = dot(wwi_ref) * (IDX_HEADS ** -0.5)
    z_ref[...] = _silu(dot(wz_ref))
    qaug_ref[...] = (dot(wqs_ref) * (AT_HEAD_DIM ** -0.5 * float(np.log2(np.e))) + qcol_ref[...]).astype(BF16)
    pos = (pl.program_id(0) % tiles_per_seq) * tm + lax.broadcasted_iota(jnp.int32, (tm, 1), 0)
    a = (pos >> 4).astype(F32)
    c = (pos & 15).astype(F32)
    col = lax.broadcasted_iota(jnp.int32, (1, AT_KV_HEADS * LANES), 1) & (LANES - 1)
    in_a = (col >= AT_HEAD_DIM) & (col < AT_HEAD_DIM + 3)
    in_c = (col >= AT_HEAD_DIM + 3) & (col < AT_HEAD_DIM + ALIBI_COLS)
    kaug_ref[...] = (dot(wks_ref) + jnp.where(in_a, a, jnp.where(in_c, c, 0.0))).astype(BF16)
    vaug_ref[...] = (dot(wvs_ref) + jnp.where(col == AT_HEAD_DIM, 1.0, 0.0)).astype(BF16)


def _at_in_prompt(x, w, batch, seq, tm):
    assert tm == KEY_BLOCK
    n = x.shape[0]
    tps = seq // tm
    row = lambda c: pl.BlockSpec((tm, c), lambda i: (i, 0))
    feat = lambda c: pl.BlockSpec((None, c, tm), lambda i: (i // tps, 0, i % tps))
    rows_out = [(IDX_HEADS * IDX_DIM, F32), (LANES, F32), (D_MODEL, F32), (AT_HEADS * LANES, BF16),
                (AT_KV_HEADS * LANES, BF16), (AT_KV_HEADS * LANES, BF16)]
    kern = functools.partial(_at_in_prompt_kernel, tm=tm, tiles_per_seq=tps)
    return pl.pallas_call(
        kern, grid=(n // tm,),
        in_specs=[row(D_MODEL)] + [_const_spec(a.shape) for a in w],
        out_specs=[feat(AT_KV_DIM), feat(AT_KV_DIM), feat(IDX_DIM),
                   pl.BlockSpec((None, None, IDX_DIM, tm), lambda i: (i // tps, i % tps, 0, 0))]
        + [row(c) for c, _ in rows_out],
        out_shape=[jax.ShapeDtypeStruct((batch, AT_KV_DIM, seq), F32), jax.ShapeDtypeStruct((batch, AT_KV_DIM, seq), F32),
                   jax.ShapeDtypeStruct((batch, IDX_DIM, seq), F32),
                   jax.ShapeDtypeStruct((batch, tps, IDX_DIM, tm), F32)]
        + [jax.ShapeDtypeStruct((n, c), dt) for c, dt in rows_out],
        compiler_params=_params(1), name="at_in_prompt")(x, *w)


def _page_dmas(pt_ref, seq0, n_seq, n_pages, src_hbm, dst_of, sem):
    out = []
    for g in range(n_seq):
        for p in range(n_pages):
            out.append(pltpu.make_async_copy(src_hbm.at[pt_ref[(seq0 + g) * n_pages + p]], dst_of(g, p), sem))
    return out


def _prefetch_step(dmas):
    i = pl.program_id(0)
    slot = i % 2

    @pl.when(i == 0)
    def _():
        for c in dmas(0, 0):
            c.start()

    @pl.when(i + 1 < pl.num_programs(0))
    def _():
        for c in dmas(i + 1, 1 - slot):
            c.start()
    for c in dmas(i, slot):
        c.wait()
    return slot


def _dsa_select_sample_kernel(pt_ref, qi_ref, wi_ref, kin_ref, kidx_hbm, bias_ref, kibuf, sem, idxbuf, keybuf,
                              *, g_seq, n_pages, page, steps, topk):
    past = n_pages * page
    s_pad = past + LANES
    n_tiles = s_pad // LANES
    rows = g_seq * 8
    slot = _prefetch_step(lambda step, sl: _page_dmas(
        pt_ref, step * g_seq, g_seq, n_pages, kidx_hbm,
        lambda g, p: kibuf.at[sl, g, :, pl.ds(p * page, page)], sem.at[sl]))

    idxbuf[...] = jnp.full((rows, s_pad), -jnp.inf, F32)
    tt = lax.broadcasted_iota(jnp.int32, (steps, 8), 0)
    jj = lax.broadcasted_iota(jnp.int32, (steps, 8), 1)
    for g in range(g_seq):
        qs = qi_ref[g].astype(BF16)
        w = wi_ref[g]
        sc = _bdot(qs, kibuf[slot, g])
        val = jnp.maximum(sc, 0.0) * jnp.concatenate([w] * (past // LANES), axis=1)
        idxbuf[g * 8:g * 8 + steps, 0:past] = jnp.sum(val.reshape(steps, IDX_HEADS, past), axis=1)
        scn = _bdot_nt(qs, kin_ref[g])
        valn = jnp.maximum(scn, 0.0) * w[:, 0:8]
        idn = jnp.sum(valn.reshape(steps, IDX_HEADS, 8), axis=1)
        idxbuf[g * 8:g * 8 + steps, past:past + 8] = jnp.where(jj <= tt, idn, -jnp.inf)

    bits = pltpu.bitcast(idxbuf[...], jnp.int32)
    keybuf[...] = jnp.where(bits < 0, bits ^ jnp.int32(0x7FFFFFFF), bits)

    def tile(c):
        return keybuf[:, c * LANES:(c + 1) * LANES]

    def bit_pass(b, carry):
        thr, cnt_gt = carry
        cand = thr + jnp.left_shift(jnp.int32(1), 31 - b)
        part = jnp.zeros((rows, LANES), F32)
        for c in range(n_tiles):
            part = part + jnp.where(tile(c) >= cand, 1.0, 0.0)
        total = jnp.sum(part, axis=1, keepdims=True)
        ok = total >= float(topk)
        return jnp.where(ok, cand, thr), jnp.where(ok, cnt_gt, total)
    thr, cnt_gt = lax.fori_loop(0, 32, bit_pass, (jnp.full((rows, 1), INT_MIN, jnp.int32),
                                                  jnp.zeros((rows, 1), F32)))
    need_eq = float(topk) - cnt_gt

    incl = jnp.where(lax.broadcasted_iota(jnp.int32, (LANES, LANES), 0)
                     <= lax.broadcasted_iota(jnp.int32, (LANES, LANES), 1), 1.0, 0.0).astype(BF16)
    eq_rows = jnp.concatenate([jnp.where(tile(c) == thr, 1.0, 0.0).astype(BF16) for c in range(n_tiles)], axis=0)
    pre = jnp.dot(eq_rows, incl, preferred_element_type=F32)
    run = jnp.zeros((rows, 1), F32)
    for c in range(n_tiles):
        key = tile(c)
        pc = pre[c * rows:(c + 1) * rows] + run
        sel = (key > thr) | ((key == thr) & (pc <= need_eq))
        idxbuf[:, c * LANES:(c + 1) * LANES] = jnp.where(sel, 0.0, NEG_BIG)
        run = pc[:, LANES - 1:LANES]
    for g in range(g_seq):
        bias_ref[g] = idxbuf[g * 8:(g + 1) * 8, :]


def _dsa_select_sample(page_table, qi_st, wi_col, ki_new, cache_ki, topk, g_seq):
    bs, n_pages = page_table.shape
    page = cache_ki.shape[2]
    past = n_pages * page
    s_pad = past + LANES
    rows_q = qi_st.shape[1]
    blk = lambda r, c: pl.BlockSpec((g_seq, r, c), lambda i, pt: (i, 0, 0))
    kern = functools.partial(_dsa_select_sample_kernel, g_seq=g_seq, n_pages=n_pages, page=page,
                             steps=rows_q // IDX_HEADS, topk=topk)
    return pl.pallas_call(
        kern,
        grid_spec=pltpu.PrefetchScalarGridSpec(
            num_scalar_prefetch=1, grid=(bs // g_seq,),
            in_specs=[blk(rows_q, IDX_DIM), blk(rows_q, LANES), blk(8, IDX_DIM), pl.BlockSpec(memory_space=pl.ANY)],
            out_specs=blk(8, s_pad),
            scratch_shapes=[pltpu.VMEM((2, g_seq, IDX_DIM, past), F32), pltpu.SemaphoreType.DMA((2,)),
                            pltpu.VMEM((g_seq * 8, s_pad), F32), pltpu.VMEM((g_seq * 8, s_pad), jnp.int32)]),
        out_shape=jax.ShapeDtypeStruct((bs, 8, s_pad), F32),
        compiler_params=_params(1), name="dsa_select_sample")(page_table.reshape(-1), qi_st, wi_col, ki_new, cache_ki)


def _dsa_attend_sample_kernel(pt_ref, qbd_ref, zbd_ref, bias_ref, knew_ref, vnew_ref, meta_ref, k_hbm, v_hbm,
                              out_ref, kbuf, vbuf, sem, *, n_pages, page, steps):
    past = n_pages * page

    def dmas(step, sl):
        return (_page_dmas(pt_ref, step, 1, n_pages, k_hbm, lambda g, p: kbuf.at[sl, :, pl.ds(p * page, page)],
                           sem.at[0, sl])
                + _page_dmas(pt_ref, step, 1, n_pages, v_hbm, lambda g, p: vbuf.at[sl, :, pl.ds(p * page, page)],
                             sem.at[1, sl]))
    slot = _prefetch_step(dmas)

    q = qbd_ref[0].astype(BF16)
    slope = meta_ref[:, 0:1]
    tq = meta_ref[:, 1:2]
    bias = jnp.concatenate([jnp.broadcast_to(bias_ref[0, t:t + 1, :], (AT_HEADS, past + LANES))
                            for t in range(steps)], axis=0)
    pos_old = lax.broadcasted_iota(jnp.int32, (1, past), 1).astype(F32)
    l_old = _bdot(q, kbuf[slot]) - slope * ((float(past) + tq) - pos_old) + bias[:, 0:past]
    pos_new = lax.broadcasted_iota(jnp.int32, (1, 8), 1).astype(F32)
    l_new = _bdot_nt(q, knew_ref[0]) - slope * (tq - pos_new) + bias[:, past:past + 8]
    m = jnp.maximum(jnp.max(l_old, axis=1, keepdims=True), jnp.max(l_new, axis=1, keepdims=True))
    p_old = jnp.exp(l_old - m)
    p_new = jnp.exp(l_new - m)
    den = jnp.sum(p_old, axis=1, keepdims=True) + jnp.sum(p_new, axis=1, keepdims=True)
    o = _bdot_nt(p_old, vbuf[slot]) + _bdot(p_new, vnew_ref[0])
    out_ref[0] = o / den * zbd_ref[0]


def _dsa_attend_sample(page_table, qbd, zbd, bias, k_new, v_new, meta, cache_k, cache_v):
    bs, n_pages = page_table.shape
    page = cache_k.shape[2]
    past = n_pages * page
    rows = qbd.shape[1]
    seq = lambda r, c: pl.BlockSpec((1, r, c), lambda i, pt: (i, 0, 0))
    hbm = pl.BlockSpec(memory_space=pl.ANY)
    kern = functools.partial(_dsa_attend_sample_kernel, n_pages=n_pages, page=page, steps=rows // AT_HEADS)
    return pl.pallas_call(
        kern,
        grid_spec=pltpu.PrefetchScalarGridSpec(
            num_scalar_prefetch=1, grid=(bs,),
            in_specs=[seq(rows, AT_KV_DIM), seq(rows, AT_KV_DIM), seq(8, past + LANES), seq(8, AT_KV_DIM),
                      seq(8, AT_KV_DIM), pl.BlockSpec(meta.shape, lambda i, pt: (0, 0)), hbm, hbm],
            out_specs=seq(rows, AT_KV_DIM),
            scratch_shapes=[pltpu.VMEM((2, AT_KV_DIM, past), F32), pltpu.VMEM((2, AT_KV_DIM, past), F32),
                            pltpu.SemaphoreType.DMA((2, 2))]),
        out_shape=jax.ShapeDtypeStruct((bs, rows, AT_KV_DIM), F32),
        compiler_params=_params(1), name="dsa_attend_sample")(
            page_table.reshape(-1), qbd, zbd, bias, k_new, v_new, meta, cache_k, cache_v)


def _dsa_kernel(qaug_ref, qi_ref, wi_ref, z_ref, kaug_ref, vaug_ref, ki_ref, act_ref,
                keys_ref, bias_ref, mrun_ref, m_ref, acc_ref, p_ref, *, t, topk):
    kb_sz = KEY_BLOCK
    pos0 = pl.program_id(1) * t
    n_kb = (pos0 + t - 1) // kb_sz + 1
    rowpos = pos0 + lax.broadcasted_iota(jnp.int32, (t, 1), 0)
    lane = lax.broadcasted_iota(jnp.int32, (1, kb_sz), 1)

    def score_block(kb, carry):
        kib = ki_ref[kb].astype(BF16)
        acc = jnp.zeros((t, kb_sz), F32)
        for i in range(IDX_HEADS):
            sc = _bdot(qi_ref[:, i * IDX_DIM:(i + 1) * IDX_DIM], kib)
            acc = acc + jnp.maximum(sc, 0.0) * wi_ref[:, i:i + 1]
        acc = jnp.where(kb * kb_sz + lane <= rowpos, acc, -jnp.inf)
        bits = pltpu.bitcast(acc, jnp.int32)
        keys_ref[kb] = jnp.where(bits < 0, bits ^ jnp.int32(0x7FFFFFFF), bits)
        return carry
    lax.fori_loop(0, n_kb, score_block, 0)

    def bit_pass(b, carry):
        thr, cnt_gt = carry
        cand = thr + jnp.left_shift(jnp.int32(1), 31 - b)

        def body(kb, cnt):
            hit = jnp.where(keys_ref[kb] >= cand, 1.0, 0.0)
            for s in range(kb_sz // LANES):
                cnt = cnt + hit[:, s * LANES:(s + 1) * LANES]
            return cnt
        cnt = lax.fori_loop(0, n_kb, body, jnp.zeros((t, LANES), F32))
        total = jnp.sum(cnt, axis=1, keepdims=True)
        ok = total >= float(topk)
        return jnp.where(ok, cand, thr), jnp.where(ok, cnt_gt, total)
    thr, cnt_gt = lax.fori_loop(0, 32, bit_pass, (jnp.full((t, 1), INT_MIN, jnp.int32), jnp.zeros((t, 1), F32)))
    need_eq = float(topk) - cnt_gt

    mrun_ref[...] = jnp.full(mrun_ref.shape, NEG_BIG, F32)
    incl = jnp.where(lax.broadcasted_iota(jnp.int32, (kb_sz, kb_sz), 0)
                     <= lax.broadcasted_iota(jnp.int32, (kb_sz, kb_sz), 1), 1.0, 0.0).astype(BF16)
    r_per_g = AT_HEADS // AT_KV_HEADS
    nt = (((1,), (1,)), ((), ()))

    def logits(h, kg, mask_bias):
        return lax.dot_general(qaug_ref[:, h * LANES:(h + 1) * LANES], kg, nt,
                               preferred_element_type=F32) + mask_bias

    def max_block(kb, run_eq):
        ks = pl.ds(pl.multiple_of(kb * kb_sz, kb_sz), kb_sz)
        key = keys_ref[kb]
        eq = key == thr
        pre = jnp.dot(jnp.where(eq, 1.0, 0.0).astype(BF16), incl, preferred_element_type=F32) + run_eq
        sel = ((key > thr) | (eq & (pre <= need_eq))) & (kb * kb_sz + lane <= rowpos)
        mask_bias = jnp.where(sel, 0.0, NEG_BIG)
        bias_ref[kb] = mask_bias
        for g in range(AT_KV_HEADS):
            kg = kaug_ref[ks, g * LANES:(g + 1) * LANES]
            for r in range(r_per_g):
                h = g * r_per_g + r
                s = logits(h, kg, mask_bias)
                tile_max = s[:, 0:LANES]
                for c in range(1, kb_sz // LANES):
                    tile_max = jnp.maximum(tile_max, s[:, c * LANES:(c + 1) * LANES])
                mrun_ref[h] = jnp.maximum(mrun_ref[h], tile_max)
        return pre[:, kb_sz - 1:kb_sz]
    lax.fori_loop(0, n_kb, max_block, jnp.zeros((t, 1), F32))
    for h in range(AT_HEADS):
        m_ref[h] = jnp.max(mrun_ref[h], axis=1, keepdims=True)

    acc_ref[...] = jnp.zeros(acc_ref.shape, F32)

    def attend_block(kb, carry):
        ks = pl.ds(pl.multiple_of(kb * kb_sz, kb_sz), kb_sz)
        mask_bias = bias_ref[kb]
        for g in range(AT_KV_HEADS):
            kg = kaug_ref[ks, g * LANES:(g + 1) * LANES]
            for r in range(r_per_g):
                h = g * r_per_g + r
                p_ref[h] = jnp.exp2(logits(h, kg, mask_bias) - m_ref[h]).astype(BF16)
        for g in range(AT_KV_HEADS):
            vg = vaug_ref[ks, g * LANES:(g + 1) * LANES]
            for r in range(r_per_g):
                h = g * r_per_g + r
                acc_ref[h] = acc_ref[h] + jnp.dot(p_ref[h], vg, preferred_element_type=F32)
        return carry
    lax.fori_loop(0, n_kb, attend_block, 0)

    for h in range(AT_HEADS):
        hs = slice(h * AT_HEAD_DIM, (h + 1) * AT_HEAD_DIM)
        a = acc_ref[h]
        o = a[:, 0:AT_HEAD_DIM] / a[:, AT_HEAD_DIM:AT_HEAD_DIM + 1]
        act_ref[:, hs] = (o * z_ref[:, hs]).astype(act_ref.dtype)


def _dsa(qaug, qi, wi, z, kaug, vaug, ki_all, batch, n_qb, t, s_len, topk):
    n = qaug.shape[0]
    qrow = lambda c: pl.BlockSpec((t, c), lambda b, j: (b * n_qb + j, 0))
    krow = lambda c: pl.BlockSpec((s_len, c), lambda b, j: (b, 0))
    n_kb = s_len // KEY_BLOCK
    kern = functools.partial(_dsa_kernel, t=t, topk=topk)
    return pl.pallas_call(
        kern, grid=(batch, n_qb),
        in_specs=[qrow(AT_HEADS * LANES), qrow(IDX_HEADS * IDX_DIM), qrow(LANES), qrow(D_MODEL),
                  krow(AT_KV_HEADS * LANES), krow(AT_KV_HEADS * LANES),
                  pl.BlockSpec((None, n_kb, IDX_DIM, KEY_BLOCK), lambda b, j: (b, 0, 0, 0))],
        out_specs=qrow(D_MODEL), out_shape=jax.ShapeDtypeStruct((n, D_MODEL), BF16),
        scratch_shapes=[pltpu.VMEM((n_kb, t, KEY_BLOCK), jnp.int32), pltpu.VMEM((n_kb, t, KEY_BLOCK), F32),
                        pltpu.VMEM((AT_HEADS, t, LANES), F32), pltpu.VMEM((AT_HEADS, t, 1), F32),
                        pltpu.VMEM((AT_HEADS, t, LANES), F32), pltpu.VMEM((AT_HEADS, t, KEY_BLOCK), BF16)],
        compiler_params=_params(2), name="dsa")(qaug, qi, wi, z, kaug, vaug, ki_all)


def _gm_kernel(*refs, tm, chunk, emit_v):
    if emit_v:
        (x_ref, win_ref, bin_ref, g_ref, b_ref, wmix_ref, bmix_ref, act_ref, v_ref, ubuf, vbuf, zbuf) = refs
    else:
        (x_ref, win_ref, bin_ref, g_ref, b_ref, wmix_ref, bmix_ref, act_ref, ubuf, vbuf, zbuf) = refs
    xb = x_ref[...].astype(BF16)
    w = GM_WIDTH
    cols = 512
    for cc in range(w // cols):
        cs = slice(cc * cols, (cc + 1) * cols)
        proj = lambda off: (jnp.dot(xb, win_ref[:, off + cc * cols:off + (cc + 1) * cols],
                                    preferred_element_type=F32) + bin_ref[:, off + cc * cols:off + (cc + 1) * cols])
        ubuf[:, cs] = _gelu_tanh(proj(0))
        vbuf[:, cs] = _gelu_tanh(proj(w))
        zbuf[:, cs] = _silu(proj(2 * w))

    rows = 32

    def norm_rows(r, carry):
        rs = pl.ds(pl.multiple_of(r * rows, rows), rows)
        v = _layer_norm(vbuf[rs, :], g_ref[...], b_ref[...])
        vbuf[rs, :] = v
        if emit_v:
            v_ref[rs, :] = v
        return carry
    lax.fori_loop(0, tm // rows, norm_rows, 0)

    gw = w // GM_GROUPS
    for c in range(tm // chunk):
        rs = slice(c * chunk, (c + 1) * chunk)
        for g in range(GM_GROUPS):
            cs = slice(g * gw, (g + 1) * gw)
            s = jnp.dot(wmix_ref[g], vbuf[rs, cs].astype(BF16), preferred_element_type=F32) + bmix_ref[:, g:g + 1]
            act_ref[rs, cs] = (ubuf[rs, cs] * s * zbuf[rs, cs]).astype(act_ref.dtype)


def _gm(x, w, tm, chunk, emit_v):
    n = x.shape[0]
    row = lambda c: pl.BlockSpec((tm, c), lambda i: (i, 0))
    out_specs = [row(GM_WIDTH)]
    out_shape = [jax.ShapeDtypeStruct((n, GM_WIDTH), BF16)]
    if emit_v:
        out_specs.append(row(GM_WIDTH))
        out_shape.append(jax.ShapeDtypeStruct((n, GM_WIDTH), F32))
    kern = functools.partial(_gm_kernel, tm=tm, chunk=chunk, emit_v=emit_v)
    return pl.pallas_call(
        kern, grid=(n // tm,),
        in_specs=[row(D_MODEL)] + [_const_spec(a.shape) for a in w],
        out_specs=out_specs, out_shape=out_shape,
        scratch_shapes=[pltpu.VMEM((tm, GM_WIDTH), F32)] * 3,
        compiler_params=_params(1), name="gmlp")(x, *w)


def _pad_lanes(a, width=LANES):
    return jnp.pad(a, [(0, 0)] * (a.ndim - 1) + [(0, width - a.shape[-1])])


def kernel(x_prompt, x_sample, p_prompt, p_sample, state_cf_conv, state_mb_conv, state_mb_ssm,
           cache_k, cache_v, cache_kidx, page_table, post_ln_g, post_ln_b, ple_w, ple_gate_w,
           cf_w_in, cf_b_in, cf_w_dw, cf_b_dw, cf_ln_g, cf_ln_b, cf_w_out, cf_b_out,
           mb_w_in, mb_w_conv, mb_b_conv, mb_dt_bias, mb_a_log, mb_d_skip, mb_norm_g, mb_w_out,
           at_w_in, at_ki_ln_g, at_ki_ln_b, at_w_out,
           gm_w_in, gm_b_in, gm_ln_g, gm_ln_b, gm_w_s, gm_b_s, gm_w_out):
    bp, lp, d = x_prompt.shape
    bs, ls, _ = x_sample.shape
    row2 = lambda v: v.reshape(1, -1)
    bf = lambda v: v.astype(BF16)

    ple_wb, ple_gate_wb = bf(ple_w), bf(ple_gate_w)
    cf_w = (bf(cf_w_in), row2(cf_b_in), cf_w_dw, row2(cf_b_dw), row2(cf_ln_g), row2(cf_ln_b))
    mb_in_w = (bf(mb_w_in[:, :MB_INNER]), bf(mb_w_in[:, MB_INNER:MB_INNER + MB_CONV_DIM]),
               bf(_pad_lanes(mb_w_in[:, MB_INNER + MB_CONV_DIM:])), _pad_lanes(row2(mb_dt_bias)),
               mb_w_conv, row2(mb_b_conv))
    ssd_w = (_pad_lanes(row2(mb_a_log)), row2(jnp.repeat(mb_d_skip, MB_HEAD_DIM)), row2(mb_norm_g))
    sizes = [AT_HEADS * AT_HEAD_DIM, AT_KV_DIM, AT_KV_DIM, IDX_HEADS * IDX_DIM, IDX_DIM, IDX_HEADS,
             AT_HEADS * AT_HEAD_DIM]
    offs = np.concatenate([[0], np.cumsum(sizes)]).tolist()
    at_parts = [at_w_in[:, offs[i]:offs[i + 1]] for i in range(len(sizes))]
    at_parts[5] = _pad_lanes(at_parts[5])
    at_w = tuple(bf(a) for a in at_parts) + (row2(at_ki_ln_g), row2(at_ki_ln_b))
    wq_b, wk_b, wv_b, wqi_b, wki_b, wwi_b, wz_b = at_w[:7]
    at_w_prompt = (_slot_weights(wq_b, AT_HEADS), wk_b.T, wv_b.T, _slot_weights(wk_b, AT_KV_HEADS),
                   _slot_weights(wv_b, AT_KV_HEADS), wqi_b, wki_b.T, wwi_b, wz_b,
                   at_ki_ln_g.reshape(-1, 1), at_ki_ln_b.reshape(-1, 1), jnp.asarray(_alibi_query_columns()))
    gm_common = (bf(gm_w_in), row2(gm_b_in), row2(gm_ln_g), row2(gm_ln_b))
    zeros_d = jnp.zeros((1, d), F32)
    out_w = [(bf(cf_w_out), row2(cf_b_out)), (bf(mb_w_out), zeros_d), (bf(at_w_out), zeros_d),
             (bf(gm_w_out), zeros_d)]

    post_g, post_b = post_ln_g.reshape(DEPTH, 1, d), post_ln_b.reshape(DEPTH, 1, d)

    def tail(i, act, x, p, tm):
        wo, bo = out_w[i]
        return _tail(i, act, x, p, wo, bo, post_g, post_b, ple_gate_wb, ple_wb, tm)

    tm = 512
    n_p = bp * lp
    x = x_prompt.reshape(n_p, d)
    pp = p_prompt.reshape(DEPTH, n_p, D_PLE)

    act, cf_conv_p = _cf_prompt(x, cf_w, bp, lp, tm)
    x = tail(0, act, x, pp, tm)

    z, xbc, dt, mb_conv_p = _mb_in_prompt(x, mb_in_w, bp, lp, tm)
    act, mb_ssm_p = _ssd(xbc, dt, z, *ssd_w, None, bp, MB_CHUNK, MB_CHUNK, lp // MB_CHUNK, BF16)
    x = tail(1, act, x, pp, tm)

    kt, vt, kit, kib, qi, wi, zs, qaug, kaug, vaug = _at_in_prompt(x, at_w_prompt, bp, lp, tm)
    act = _dsa(qaug, qi, wi, zs, kaug, vaug, kib, bp, lp // DSA_ROWS, DSA_ROWS, lp, min(TOPK_MAX, lp // 4))
    x = tail(2, act, x, pp, tm)
    k_p = kt.reshape(bp, AT_KV_HEADS, AT_HEAD_DIM, lp).transpose(0, 3, 1, 2)
    v_p = vt.reshape(bp, AT_KV_HEADS, AT_HEAD_DIM, lp).transpose(0, 3, 1, 2)
    kidx_p = kit.transpose(0, 2, 1)

    gm_w_p = gm_common + (bf(jnp.tril(gm_w_s)), gm_b_s.T)
    (act,) = _gm(x, gm_w_p, tm, GM_CHUNK, False)
    y_prompt = tail(3, act, x, pp, tm).reshape(bp, lp, d)

    n_s = bs * ls
    bb = 32
    to_tm = lambda a: jnp.swapaxes(a, 0, 1)
    x_tm = to_tm(x_sample)
    x = x_tm.reshape(n_s, d)
    ps = jnp.swapaxes(p_sample, 1, 2).reshape(DEPTH, n_s, D_PLE)
    tm_s = n_s

    act, cf_st = _cf_sample(x_tm, to_tm(state_cf_conv), cf_w, bb)
    cf_conv_s = to_tm(cf_st)
    x = tail(0, act.reshape(n_s, d), x, ps, tm_s)

    z, xbc, dt, mb_st = _mb_in_sample(x.reshape(ls, bs, d), to_tm(state_mb_conv), mb_in_w, bb)
    mb_conv_s = to_tm(mb_st)
    qs = 8

    def to_bm(a):
        a = jnp.pad(to_tm(a), ((0, 0), (0, qs - ls), (0, 0)))
        return a.reshape(bs * qs, a.shape[-1])

    def from_bm(a):
        a = a.reshape(bs, qs, a.shape[-1])[:, :ls]
        return to_tm(a).reshape(n_s, a.shape[-1])

    act, mb_ssm_s = _ssd(to_bm(xbc), to_bm(dt), to_bm(z), *ssd_w, state_mb_ssm, bs, qs, ls, 1, F32)
    x = tail(1, from_bm(act), x, ps, tm_s)

    q, k, v, qi, ki, wi, zs = _at_in(x, at_w, tm_s)
    k_s = to_tm(k.reshape(ls, bs, AT_KV_HEADS, AT_HEAD_DIM))
    v_s = to_tm(v.reshape(ls, bs, AT_KV_HEADS, AT_HEAD_DIM))
    kidx_s = to_tm(ki.reshape(ls, bs, IDX_DIM))
    n_pool, page = cache_k.shape[:2]
    past = page_table.shape[1] * page
    seq_major = lambda a: to_tm(a.reshape(ls, bs, a.shape[-1]))
    r_per_g = AT_HEADS // AT_KV_HEADS
    group_of_head = (np.arange(AT_HEADS)[:, None] // r_per_g == np.arange(AT_KV_HEADS)[None, :]).astype(np.float32)

    def block_diag(a):
        a = a.reshape(bs, ls, AT_HEADS, 1, AT_HEAD_DIM) * group_of_head[None, None, :, :, None]
        return a.reshape(bs, ls * AT_HEADS, AT_KV_DIM)

    pad_steps = lambda a: jnp.pad(a, ((0, 0), (0, 8 - ls), (0, 0)))
    qi_st = seq_major(qi).reshape(bs, ls * IDX_HEADS, IDX_DIM)
    wi_col = jnp.broadcast_to(seq_major(wi)[:, :, :IDX_HEADS].reshape(bs, ls * IDX_HEADS, 1),
                              (bs, ls * IDX_HEADS, LANES))
    pages_t = lambda c: jnp.moveaxis(c, 1, -1).reshape(n_pool, -1, page)
    bias = _dsa_select_sample(page_table, qi_st, wi_col, pad_steps(seq_major(ki)), pages_t(cache_kidx),
                              min(TOPK_MAX, (past + ls) // 4), 8)
    meta = np.zeros((ls * AT_HEADS, LANES), np.float32)
    meta[:, 0] = np.tile(np.asarray(ALIBI_SLOPES, np.float32), ls)
    meta[:, 1] = np.repeat(np.arange(ls, dtype=np.float32), AT_HEADS)
    o = _dsa_attend_sample(page_table, block_diag(seq_major(q)), block_diag(seq_major(zs)), bias,
                           pad_steps(seq_major(k)), pad_steps(seq_major(v)), jnp.asarray(meta),
                           pages_t(cache_k), pages_t(cache_v))
    act = o.reshape(bs, ls, AT_HEADS, AT_KV_HEADS, AT_HEAD_DIM).sum(axis=3).reshape(bs, ls, d)
    x = tail(2, to_tm(act).reshape(n_s, d), x, ps, tm_s)

    mix = jnp.stack([jnp.kron(jnp.tril(gm_w_s[g, :ls, :ls]), jnp.eye(bs, dtype=F32)) for g in range(GM_GROUPS)])
    gm_w_smp = gm_common + (bf(mix), jnp.repeat(gm_b_s[:, :ls].T, bs, axis=0))
    act, gm_v = _gm(x, gm_w_smp, tm_s, n_s, True)
    y_s = tail(3, act, x, ps, tm_s)
    y_sample = to_tm(y_s.reshape(ls, bs, d))
    gm_v_s = to_tm(gm_v.reshape(ls, bs, GM_WIDTH))

    return (y_prompt, y_sample, cf_conv_p, cf_conv_s, mb_conv_p, mb_conv_s, mb_ssm_p, mb_ssm_s,
            k_p, v_p, kidx_p, k_s, v_s, kidx_s, gm_v_s)
```

```python
import functools

import numpy as np
import jax
import jax.numpy as jnp
from jax import lax
from jax.experimental import pallas as pl
from jax.experimental.pallas import tpu as pltpu

F32 = jnp.float32
BF16 = jnp.bfloat16

D_MODEL = 1024
D_PLE = 256
DEPTH = 4
ALPHA_DN = (2 * DEPTH) ** 0.25
LN_EPS = 1e-5

CF_KERNEL = 31
CF_HIST = CF_KERNEL - 1
MB_INNER = 2048
MB_HEAD_DIM = 64
MB_HEADS = 32
MB_GROUPS = 8
MB_STATE = 128
MB_CONV = 4
MB_CONV_DIM = 4096
MB_CHUNK = 128
AT_HEADS = 16
AT_HEAD_DIM = 64
AT_KV_HEADS = 4
AT_KV_DIM = AT_KV_HEADS * AT_HEAD_DIM
IDX_HEADS = 8
IDX_DIM = 64
TOPK_MAX = 256
DSA_ROWS = 256
GM_WIDTH = 2048
GM_GROUPS = 4
GM_CHUNK = 128

LANES = 128
KEY_BLOCK = 512
NEG_BIG = -1e30
INT_MIN = -(2 ** 31)
VMEM_LIMIT = 56 * 1024 * 1024

ALIBI_SLOPES = [float(s) for s in
                (np.float32(2.0) ** (-8.0 * np.arange(1, AT_HEADS + 1, dtype=np.float32) / AT_HEADS))]


def _bdot(a, b):
    return jnp.dot(a.astype(BF16), b.astype(BF16), preferred_element_type=F32)


def _bdot_nt(a, b):
    return lax.dot_general(a.astype(BF16), b.astype(BF16), (((1,), (1,)), ((), ())),
                           preferred_element_type=F32)


def _bdot_tn(a, b):
    return lax.dot_general(a.astype(BF16), b.astype(BF16), (((0,), (0,)), ((), ())),
                           preferred_element_type=F32)


def _sigmoid(x):
    return 1.0 / (1.0 + jnp.exp(-x))


def _silu(x):
    return x * _sigmoid(x)


def _gelu_tanh(x):
    return x * (0.5 * (1.0 + jnp.tanh(np.sqrt(2.0 / np.pi).astype(np.float32) * (x + 0.044715 * (x * x * x)))))


def _softplus(x):
    return jnp.maximum(x, 0.0) + jnp.log1p(jnp.exp(-jnp.abs(x)))


def _layer_norm(x, g, b):
    mu = jnp.mean(x, axis=-1, keepdims=True)
    xc = x - mu
    var = jnp.mean(xc * xc, axis=-1, keepdims=True)
    return xc * lax.rsqrt(var + LN_EPS) * g + b


def _const_spec(shape):
    nd = len(shape)
    return pl.BlockSpec(shape, lambda *_: (0,) * nd)


def _params(n_axes):
    return pltpu.CompilerParams(dimension_semantics=("arbitrary",) * n_axes,
                                vmem_limit_bytes=VMEM_LIMIT)


def _tail_kernel(act_ref, x_ref, p_ref, wo_ref, bo_ref, g_ref, b_ref, wg_ref, wp_ref, o_ref):
    out = _bdot(act_ref[...], wo_ref[...]) + bo_ref[...]
    h = _layer_norm(ALPHA_DN * x_ref[...] + out, g_ref[...], b_ref[...])
    gate = _sigmoid(_bdot(h, wg_ref[...]))
    o_ref[...] = h + gate * _bdot(p_ref[...], wp_ref[...])


def _tail(layer, act, x, p, wo, bo, g, b, wg, wp, tm):
    n, k = act.shape
    row = lambda c: pl.BlockSpec((tm, c), lambda i: (i, 0))
    of_layer = lambda a: pl.BlockSpec((None,) + a.shape[1:], lambda i: (layer,) + (0,) * (a.ndim - 1))
    return pl.pallas_call(
        _tail_kernel, grid=(n // tm,),
        in_specs=[row(k), row(D_MODEL), pl.BlockSpec((None, tm, D_PLE), lambda i: (layer, i, 0)),
                  _const_spec(wo.shape), _const_spec(bo.shape),
                  of_layer(g), of_layer(b), of_layer(wg), of_layer(wp)],
        out_specs=row(D_MODEL), out_shape=jax.ShapeDtypeStruct((n, D_MODEL), F32),
        compiler_params=_params(1), name="layer_tail")(act, x, p, wo, bo, g, b, wg, wp)


def _cf_project(x2d, win_ref, bin_ref):
    xb = x2d.astype(BF16)
    d = D_MODEL
    a = jnp.dot(xb, win_ref[:, 0:d], preferred_element_type=F32) + bin_ref[:, 0:d]
    gl = jnp.dot(xb, win_ref[:, d:2 * d], preferred_element_type=F32) + bin_ref[:, d:2 * d]
    z = jnp.dot(xb, win_ref[:, 2 * d:3 * d], preferred_element_type=F32) + bin_ref[:, 2 * d:3 * d]
    return a * _sigmoid(gl), _silu(z)


def _cf_finish(cbuf, zbuf, g_ref, b_ref, store, n_rows, chunk):
    def body(r, carry):
        rs = pl.ds(pl.multiple_of(r * chunk, chunk), chunk)
        c = _silu(_layer_norm(cbuf[rs, :], g_ref[...], b_ref[...]))
        store(r, rs, (c * zbuf[rs, :]).astype(BF16))
        return carry
    lax.fori_loop(0, n_rows // chunk, body, 0, unroll=2)


def _cf_prompt_kernel(x_ref, win_ref, bin_ref, wdw_ref, bdw_ref, g_ref, b_ref, act_ref, st_ref,
                      ubuf, zbuf, cbuf, *, tm, tiles_per_seq):
    head = 32
    i = pl.program_id(0)
    first = (i % tiles_per_seq) == 0

    @pl.when(first)
    def _():
        ubuf[0:head, :] = jnp.zeros((head, D_MODEL), F32)

    @pl.when(jnp.logical_not(first))
    def _():
        ubuf[0:head, :] = ubuf[tm:tm + head, :]

    u, zs = _cf_project(x_ref[...], win_ref, bin_ref)
    ubuf[head:head + tm, :] = u
    zbuf[...] = zs
    off = head - CF_HIST
    rows, cols = 128, 128
    for rc in range(tm // rows):
        for cc in range(D_MODEL // cols):
            cs = slice(cc * cols, (cc + 1) * cols)
            acc = jnp.broadcast_to(bdw_ref[:, cs], (rows, cols))
            for b in range(8):
                span = rows + (8 if b else 0)
                part = None
                for a in range((off + CF_KERNEL + 7) // 8):
                    j = 8 * a + b - off
                    if 0 <= j < CF_KERNEL:
                        r0 = rc * rows + 8 * a
                        term = wdw_ref[j:j + 1, cs] * ubuf[r0:r0 + span, cs]
                        part = term if part is None else part + term
                acc = acc + part[b:b + rows, :]
            cbuf[rc * rows:(rc + 1) * rows, cs] = acc
    def store(r, rs, val):
        act_ref[rs, :] = val
    _cf_finish(cbuf, zbuf, g_ref, b_ref, store, tm, 64)

    @pl.when((i % tiles_per_seq) == tiles_per_seq - 1)
    def _():
        st_ref[0] = ubuf[head + tm - CF_HIST:head + tm, :]


def _cf_prompt(x, w, batch, seq, tm):
    n = x.shape[0]
    tps = seq // tm
    row = lambda c: pl.BlockSpec((tm, c), lambda i: (i, 0))
    kern = functools.partial(_cf_prompt_kernel, tm=tm, tiles_per_seq=tps)
    return pl.pallas_call(
        kern, grid=(n // tm,),
        in_specs=[row(D_MODEL)] + [_const_spec(a.shape) for a in w],
        out_specs=[row(D_MODEL), pl.BlockSpec((1, CF_HIST, D_MODEL), lambda i: (i // tps, 0, 0))],
        out_shape=[jax.ShapeDtypeStruct((n, D_MODEL), BF16),
                   jax.ShapeDtypeStruct((batch, CF_HIST, D_MODEL), F32)],
        scratch_shapes=[pltpu.VMEM((tm + 32, D_MODEL), F32), pltpu.VMEM((tm, D_MODEL), F32),
                        pltpu.VMEM((tm, D_MODEL), F32)],
        compiler_params=_params(1), name="cf_prompt")(x, *w)


def _cf_sample_kernel(x_ref, hist_ref, win_ref, bin_ref, wdw_ref, bdw_ref, g_ref, b_ref, act_ref, st_ref,
                      ubuf, zbuf, cbuf, *, steps, bb):
    n = steps * bb
    u, zs = _cf_project(x_ref[...].reshape(n, D_MODEL), win_ref, bin_ref)
    zbuf[...] = zs
    ubuf[0:CF_HIST] = hist_ref[...]
    for t in range(steps):
        ubuf[CF_HIST + t] = u[t * bb:(t + 1) * bb, :]
    cols = 512
    for t in range(steps):
        for cc in range(D_MODEL // cols):
            cs = slice(cc * cols, (cc + 1) * cols)
            acc = jnp.broadcast_to(bdw_ref[:, cs], (bb, cols))
            for j in range(CF_KERNEL):
                acc = acc + wdw_ref[j:j + 1, cs] * ubuf[t + j, :, cs]
            cbuf[t * bb:(t + 1) * bb, cs] = acc
    def store(r, rs, val):
        act_ref[r] = val
    _cf_finish(cbuf, zbuf, g_ref, b_ref, store, n, bb)
    st_ref[...] = ubuf[steps:steps + CF_HIST]


def _cf_sample(x_tm, hist_tm, w, bb):
    steps, batch, _ = x_tm.shape
    kern = functools.partial(_cf_sample_kernel, steps=steps, bb=bb)
    blk = lambda t, c: pl.BlockSpec((t, bb, c), lambda i: (0, i, 0))
    return pl.pallas_call(
        kern, grid=(batch // bb,),
        in_specs=[blk(steps, D_MODEL), blk(CF_HIST, D_MODEL)] + [_const_spec(a.shape) for a in w],
        out_specs=[blk(steps, D_MODEL), blk(CF_HIST, D_MODEL)],
        out_shape=[jax.ShapeDtypeStruct((steps, batch, D_MODEL), BF16),
                   jax.ShapeDtypeStruct((CF_HIST, batch, D_MODEL), F32)],
        scratch_shapes=[pltpu.VMEM((CF_HIST + steps, bb, D_MODEL), F32),
                        pltpu.VMEM((steps * bb, D_MODEL), F32), pltpu.VMEM((steps * bb, D_MODEL), F32)],
        compiler_params=_params(1), name="cf_sample")(x_tm, hist_tm, *w)


def _mb_project(x2d, wz_ref, wxbc_ref, wdt_ref, dtb_ref, store_z, store_dt, store_xbc):
    xb = x2d.astype(BF16)
    store_z(jnp.dot(xb, wz_ref[...], preferred_element_type=F32))
    store_dt(_softplus(jnp.dot(xb, wdt_ref[...], preferred_element_type=F32) + dtb_ref[...]))
    cols = 1024
    for cc in range(MB_CONV_DIM // cols):
        cs = slice(cc * cols, (cc + 1) * cols)
        store_xbc(cs, jnp.dot(xb, wxbc_ref[:, cs], preferred_element_type=F32))


def _mb_in_prompt_kernel(x_ref, wz_ref, wxbc_ref, wdt_ref, dtb_ref, cw_ref, cb_ref,
                         z_ref, xbc_ref, dt_ref, st_ref, xbuf, *, tm, tiles_per_seq):
    head = 8
    i = pl.program_id(0)
    first = (i % tiles_per_seq) == 0

    @pl.when(first)
    def _():
        xbuf[0:head, :] = jnp.zeros((head, MB_CONV_DIM), F32)

    @pl.when(jnp.logical_not(first))
    def _():
        xbuf[0:head, :] = xbuf[tm:tm + head, :]

    def store_xbc(cs, val):
        xbuf[head:head + tm, cs] = val
    def store_z(val):
        z_ref[...] = val

    def store_dt(val):
        dt_ref[...] = val
    _mb_project(x_ref[...], wz_ref, wxbc_ref, wdt_ref, dtb_ref, store_z, store_dt, store_xbc)
    off = head - (MB_CONV - 1)
    rows, cols = 128, 128
    for rc in range(tm // rows):
        for cc in range(MB_CONV_DIM // cols):
            cs = slice(cc * cols, (cc + 1) * cols)
            win = xbuf[rc * rows:rc * rows + rows + head, cs]
            acc = jnp.broadcast_to(cb_ref[:, cs], (rows, cols))
            for j in range(MB_CONV):
                acc = acc + cw_ref[j:j + 1, cs] * win[off + j:off + j + rows, :]
            xbc_ref[rc * rows:(rc + 1) * rows, cs] = _silu(acc)

    @pl.when((i % tiles_per_seq) == tiles_per_seq - 1)
    def _():
        st_ref[0] = xbuf[head + tm - (MB_CONV - 1):head + tm, :]


def _mb_in_prompt(x, w, batch, seq, tm):
    n = x.shape[0]
    tps = seq // tm
    row = lambda c: pl.BlockSpec((tm, c), lambda i: (i, 0))
    kern = functools.partial(_mb_in_prompt_kernel, tm=tm, tiles_per_seq=tps)
    return pl.pallas_call(
        kern, grid=(n // tm,),
        in_specs=[row(D_MODEL)] + [_const_spec(a.shape) for a in w],
        out_specs=[row(MB_INNER), row(MB_CONV_DIM), row(LANES),
                   pl.BlockSpec((1, MB_CONV - 1, MB_CONV_DIM), lambda i: (i // tps, 0, 0))],
        out_shape=[jax.ShapeDtypeStruct((n, MB_INNER), F32), jax.ShapeDtypeStruct((n, MB_CONV_DIM), F32),
                   jax.ShapeDtypeStruct((n, LANES), F32),
                   jax.ShapeDtypeStruct((batch, MB_CONV - 1, MB_CONV_DIM), F32)],
        scratch_shapes=[pltpu.VMEM((tm + 8, MB_CONV_DIM), F32)],
        compiler_params=_params(1), name="mb_in_prompt")(x, *w)


def _mb_in_sample_kernel(x_ref, hist_ref, wz_ref, wxbc_ref, wdt_ref, dtb_ref, cw_ref, cb_ref,
                         z_ref, xbc_ref, dt_ref, st_ref, xbuf, *, steps, bb):
    n = steps * bb
    hist = MB_CONV - 1
    xbuf[0:hist] = hist_ref[...]

    def store_xbc(cs, val):
        for t in range(steps):
            xbuf[hist + t, :, cs] = val[t * bb:(t + 1) * bb, :]
    def store_z(val):
        for t in range(steps):
            z_ref[t] = val[t * bb:(t + 1) * bb, :]

    def store_dt(val):
        for t in range(steps):
            dt_ref[t] = val[t * bb:(t + 1) * bb, :]
    _mb_project(x_ref[...].reshape(n, D_MODEL), wz_ref, wxbc_ref, wdt_ref, dtb_ref,
                store_z, store_dt, store_xbc)
    cols = 512
    for t in range(steps):
        for cc in range(MB_CONV_DIM // cols):
            cs = slice(cc * cols, (cc + 1) * cols)
            acc = jnp.broadcast_to(cb_ref[:, cs], (bb, cols))
            for j in range(MB_CONV):
                acc = acc + cw_ref[j:j + 1, cs] * xbuf[t + j, :, cs]
            xbc_ref[t, :, cs] = _silu(acc)
    st_ref[...] = xbuf[steps:steps + hist]


def _mb_in_sample(x_tm, hist_tm, w, bb):
    steps, batch, _ = x_tm.shape
    hist = MB_CONV - 1
    kern = functools.partial(_mb_in_sample_kernel, steps=steps, bb=bb)
    blk = lambda t, c: pl.BlockSpec((t, bb, c), lambda i: (0, i, 0))
    return pl.pallas_call(
        kern, grid=(batch // bb,),
        in_specs=[blk(steps, D_MODEL), blk(hist, MB_CONV_DIM)] + [_const_spec(a.shape) for a in w],
        out_specs=[blk(steps, MB_INNER), blk(steps, MB_CONV_DIM), blk(steps, LANES), blk(hist, MB_CONV_DIM)],
        out_shape=[jax.ShapeDtypeStruct((steps, batch, MB_INNER), F32),
                   jax.ShapeDtypeStruct((steps, batch, MB_CONV_DIM), F32),
                   jax.ShapeDtypeStruct((steps, batch, LANES), F32),
                   jax.ShapeDtypeStruct((hist, batch, MB_CONV_DIM), F32)],
        scratch_shapes=[pltpu.VMEM((hist + steps, bb, MB_CONV_DIM), F32)],
        compiler_params=_params(1), name="mb_in_sample")(x_tm, hist_tm, *w)


def _ssd_kernel(*refs, q, valid, n_chunks, has_s0):
    if has_s0:
        (xbc_x, xbc_b, xbc_c, dt_ref, z_ref, alog_ref, dskip_ref, ng_ref, s0_ref,
         act_ref, sfin_ref, s_ref, ybuf) = refs
    else:
        (xbc_x, xbc_b, xbc_c, dt_ref, z_ref, alog_ref, dskip_ref, ng_ref,
         act_ref, sfin_ref, s_ref, ybuf) = refs
    c = pl.program_id(1)
    r_per_g = MB_HEADS // MB_GROUPS
    gw = r_per_g * MB_HEAD_DIM

    @pl.when(c == 0)
    def _():
        if has_s0:
            for g in range(MB_GROUPS):
                s_ref[g] = s0_ref[0, g * r_per_g:(g + 1) * r_per_g].reshape(gw, MB_STATE).T
        else:
            s_ref[...] = jnp.zeros(s_ref.shape, F32)

    dt = dt_ref[...]
    if valid < q:
        dt = jnp.where(lax.broadcasted_iota(jnp.int32, (q, LANES), 0) < valid, dt, 0.0)
    a = dt * (-jnp.exp(alog_ref[...]))
    rid = lax.broadcasted_iota(jnp.int32, (q, q), 0)
    cid = lax.broadcasted_iota(jnp.int32, (q, q), 1)
    tri = rid >= cid
    lower = jnp.where(tri, 1.0, 0.0)
    upper = jnp.where(rid <= cid, 1.0, 0.0)
    eye = jnp.where(rid == cid, 1.0, 0.0)
    hi = lax.Precision.HIGHEST
    cum = jnp.dot(lower, a, precision=hi, preferred_element_type=F32)
    tn = (((0,), (0,)), ((), ()))
    cum_t = lax.dot_general(a, upper, tn, precision=hi, preferred_element_type=F32)
    dt_t = lax.dot_general(dt, eye, tn, precision=hi, preferred_element_type=F32)

    def hi_lo(v):
        v_hi = v.astype(BF16)
        return jnp.concatenate([v_hi, (v - v_hi.astype(F32)).astype(BF16)], axis=1)
    ecum_hl = hi_lo(jnp.exp(cum))
    wend_hl = hi_lo(jnp.exp(cum[q - 1:q, :] - cum) * dt)
    src_head = lax.broadcasted_iota(jnp.int32, (2 * LANES, gw), 0) & (LANES - 1)
    dst_head = lax.broadcasted_iota(jnp.int32, (2 * LANES, gw), 1) // MB_HEAD_DIM
    lane_head = lax.broadcasted_iota(jnp.int32, (1, gw), 1) // MB_HEAD_DIM
    for g in range(MB_GROUPS):
        gs = slice(g * MB_STATE, (g + 1) * MB_STATE)
        xs = slice(g * gw, (g + 1) * gw)
        cg = xbc_c[:, gs].astype(BF16)
        bg = xbc_b[:, gs].astype(BF16)
        cb = _bdot_nt(cg, bg)
        spread = jnp.where(src_head == g * r_per_g + dst_head, 1.0, 0.0).astype(BF16)
        ecum_g = jnp.dot(ecum_hl, spread, preferred_element_type=F32)
        wend_g = jnp.dot(wend_hl, spread, preferred_element_type=F32)
        xg = xbc_x[:, xs]
        st = s_ref[g]
        y = _bdot(cg, st) * ecum_g + dskip_ref[:, xs] * xg
        for r in range(r_per_g):
            h = g * r_per_g + r
            decay = jnp.exp(jnp.where(tri, cum[:, h:h + 1] - cum_t[h:h + 1, :], -jnp.inf))
            m = cb * decay * dt_t[h:h + 1, :]
            y = y + _bdot(m, jnp.where(lane_head == r, xg, 0.0))
        ybuf[:, xs] = y
        s_ref[g] = ecum_g[q - 1:q, :] * st + _bdot_tn(bg, xg * wend_g)
    y = ybuf[...] * _silu(z_ref[...])
    y = y * lax.rsqrt(jnp.mean(y * y, axis=-1, keepdims=True) + LN_EPS) * ng_ref[...]
    act_ref[...] = y.astype(act_ref.dtype)

    @pl.when(c == n_chunks - 1)
    def _():
        for g in range(MB_GROUPS):
            sfin_ref[0, g * r_per_g:(g + 1) * r_per_g] = s_ref[g].T.reshape(r_per_g, MB_HEAD_DIM, MB_STATE)


def _ssd(xbc, dt, z, alog, dskip, ng, s0, batch, q, valid, n_chunks, act_dtype):
    n = xbc.shape[0]
    has_s0 = s0 is not None
    rowblk = lambda cols, cb: pl.BlockSpec((q, cols), lambda b, c: (b * n_chunks + c, cb))
    st_spec = pl.BlockSpec((1, MB_HEADS, MB_HEAD_DIM, MB_STATE), lambda b, c: (b, 0, 0, 0))
    in_specs = [rowblk(MB_INNER, 0), rowblk(MB_GROUPS * MB_STATE, 2), rowblk(MB_GROUPS * MB_STATE, 3),
                rowblk(LANES, 0), rowblk(MB_INNER, 0),
                _const_spec(alog.shape), _const_spec(dskip.shape), _const_spec(ng.shape)]
    args = [xbc, xbc, xbc, dt, z, alog, dskip, ng]
    if has_s0:
        in_specs.append(st_spec)
        args.append(s0)
    kern = functools.partial(_ssd_kernel, q=q, valid=valid, n_chunks=n_chunks, has_s0=has_s0)
    return pl.pallas_call(
        kern, grid=(batch, n_chunks), in_specs=in_specs,
        out_specs=[rowblk(MB_INNER, 0), st_spec],
        out_shape=[jax.ShapeDtypeStruct((n, MB_INNER), act_dtype),
                   jax.ShapeDtypeStruct((batch, MB_HEADS, MB_HEAD_DIM, MB_STATE), F32)],
        scratch_shapes=[pltpu.VMEM((MB_GROUPS, MB_STATE, MB_INNER // MB_GROUPS), F32),
                        pltpu.VMEM((q, MB_INNER), F32)],
        compiler_params=_params(2), name="ssd")(*args)


def _at_in_kernel(x_ref, wq_ref, wk_ref, wv_ref, wqi_ref, wki_ref, wwi_ref, wz_ref, kg_ref, kb_ref,
                  q_ref, k_ref, v_ref, qi_ref, ki_ref, wi_ref, z_ref):
    xb = x_ref[...].astype(BF16)
    dot = lambda w: jnp.dot(xb, w[...], preferred_element_type=F32)
    q_ref[...] = dot(wq_ref) * (AT_HEAD_DIM ** -0.5)
    k_ref[...] = dot(wk_ref)
    v_ref[...] = dot(wv_ref)
    qi_ref[...] = dot(wqi_ref) * (IDX_DIM ** -0.5)
    ki_ref[...] = _layer_norm(dot(wki_ref), kg_ref[...], kb_ref[...])
    wi_ref[...] = dot(wwi_ref) * (IDX_HEADS ** -0.5)
    z_ref[...] = _silu(dot(wz_ref))


def _at_in(x, w, tm):
    n = x.shape[0]
    row = lambda c: pl.BlockSpec((tm, c), lambda i: (i, 0))
    widths = [D_MODEL, AT_KV_DIM, AT_KV_DIM, IDX_HEADS * IDX_DIM, IDX_DIM, LANES, D_MODEL]
    return pl.pallas_call(
        _at_in_kernel, grid=(n // tm,),
        in_specs=[row(D_MODEL)] + [_const_spec(a.shape) for a in w],
        out_specs=[row(c) for c in widths],
        out_shape=[jax.ShapeDtypeStruct((n, c), F32) for c in widths],
        compiler_params=_params(1), name="at_in")(x, *w)


ALIBI_COLS = 6


def _split3_bf16(x):
    hi = x.astype(jnp.bfloat16).astype(np.float32)
    mid = (x - hi).astype(jnp.bfloat16).astype(np.float32)
    lo = (x - hi - mid).astype(jnp.bfloat16).astype(np.float32)
    return hi, mid, lo


def _alibi_query_columns():
    s = np.asarray(ALIBI_SLOPES, np.float32) * np.float32(np.log2(np.e))
    parts = _split3_bf16(s)
    out = np.zeros((AT_HEADS, LANES), np.float32)
    for i, part in enumerate(parts):
        out[:, AT_HEAD_DIM + i] = 16.0 * part
        out[:, AT_HEAD_DIM + 3 + i] = part
    return out.reshape(1, AT_HEADS * LANES)


def _slot_weights(w, n_slots):
    d = w.shape[0]
    w = w.reshape(d, n_slots, AT_HEAD_DIM)
    return jnp.pad(w, ((0, 0), (0, 0), (0, LANES - AT_HEAD_DIM))).reshape(d, n_slots * LANES)


def _at_in_prompt_kernel(x_ref, wqs_ref, wkt_ref, wvt_ref, wks_ref, wvs_ref, wqi_ref, wkit_ref, wwi_ref, wz_ref,
                         kg_ref, kb_ref, qcol_ref,
                         kt_ref, vt_ref, kit_ref, kib_ref, qi_ref, wi_ref, z_ref, qaug_ref, kaug_ref, vaug_ref,
                         *, tm, tiles_per_seq):
    xb = x_ref[...].astype(BF16)
    dot = lambda w: jnp.dot(xb, w[...], preferred_element_type=F32)
    dot_t = lambda wt: lax.dot_general(wt[...], xb, (((1,), (1,)), ((), ())), preferred_element_type=F32)
    kt_ref[...] = dot_t(wkt_ref)
    vt_ref[...] = dot_t(wvt_ref)
    kit = dot_t(wkit_ref)
    mu = jnp.mean(kit, axis=0, keepdims=True)
    kc = kit - mu
    var = jnp.mean(kc * kc, axis=0, keepdims=True)
    kit = kc * lax.rsqrt(var + LN_EPS) * kg_ref[...] + kb_ref[...]
    kit_ref[...] = kit
    kib_ref[...] = kit
    qi_ref[...] = dot(wqi_ref) * (IDX_DIM ** -0.5)
    wi_ref[...] = dot(wwi_ref) * (IDX_HEADS ** -0.5)
    z_ref[...] = _silu(dot(wz_ref))
    qaug_ref[...] = (dot(wqs_ref) * (AT_HEAD_DIM ** -0.5 * float(np.log2(np.e))) + qcol_ref[...]).astype(BF16)
    pos = (pl.program_id(0) % tiles_per_seq) * tm + lax.broadcasted_iota(jnp.int32, (tm, 1), 0)
    a = (pos >> 4).astype(F32)
    c = (pos & 15).astype(F32)
    col = lax.broadcasted_iota(jnp.int32, (1, AT_KV_HEADS * LANES), 1) & (LANES - 1)
    in_a = (col >= AT_HEAD_DIM) & (col < AT_HEAD_DIM + 3)
    in_c = (col >= AT_HEAD_DIM + 3) & (col < AT_HEAD_DIM + ALIBI_COLS)
    kaug_ref[...] = (dot(wks_ref) + jnp.where(in_a, a, jnp.where(in_c, c, 0.0))).astype(BF16)
    vaug_ref[...] = (dot(wvs_ref) + jnp.where(col == AT_HEAD_DIM, 1.0, 0.0)).astype(BF16)


def _at_in_prompt(x, w, batch, seq, tm):
    assert tm == KEY_BLOCK
    n = x.shape[0]
    tps = seq // tm
    row = lambda c: pl.BlockSpec((tm, c), lambda i: (i, 0))
    feat = lambda c: pl.BlockSpec((None, c, tm), lambda i: (i // tps, 0, i % tps))
    rows_out = [(IDX_HEADS * IDX_DIM, F32), (LANES, F32), (D_MODEL, F32), (AT_HEADS * LANES, BF16),
                (AT_KV_HEADS * LANES, BF16), (AT_KV_HEADS * LANES, BF16)]
    kern = functools.partial(_at_in_prompt_kernel, tm=tm, tiles_per_seq=tps)
    return pl.pallas_call(
        kern, grid=(n // tm,),
        in_specs=[row(D_MODEL)] + [_const_spec(a.shape) for a in w],
        out_specs=[feat(AT_KV_DIM), feat(AT_KV_DIM), feat(IDX_DIM),
                   pl.BlockSpec((None, None, IDX_DIM, tm), lambda i: (i // tps, i % tps, 0, 0))]
        + [row(c) for c, _ in rows_out],
        out_shape=[jax.ShapeDtypeStruct((batch, AT_KV_DIM, seq), F32), jax.ShapeDtypeStruct((batch, AT_KV_DIM, seq), F32),
                   jax.ShapeDtypeStruct((batch, IDX_DIM, seq), F32),
                   jax.ShapeDtypeStruct((batch, tps, IDX_DIM, tm), F32)]
        + [jax.ShapeDtypeStruct((n, c), dt) for c, dt in rows_out],
        compiler_params=_params(1), name="at_in_prompt")(x, *w)


def _page_dmas(pt_ref, seq0, n_seq, n_pages, src_hbm, dst_of, sem):
    out = []
    for g in range(n_seq):
        for p in range(n_pages):
            out.append(pltpu.make_async_copy(src_hbm.at[pt_ref[(seq0 + g) * n_pages + p]], dst_of(g, p), sem))
    return out


def _prefetch_step(dmas):
    i = pl.program_id(0)
    slot = i % 2

    @pl.when(i == 0)
    def _():
        for c in dmas(0, 0):
            c.start()

    @pl.when(i + 1 < pl.num_programs(0))
    def _():
        for c in dmas(i + 1, 1 - slot):
            c.start()
    for c in dmas(i, slot):
        c.wait()
    return slot


def _dsa_select_sample_kernel(pt_ref, qi_ref, wi_ref, kin_ref, kidx_hbm, bias_ref, kibuf, sem, idxbuf, keybuf,
                              *, g_seq, n_pages, page, steps, topk):
    past = n_pages * page
    s_pad = past + LANES
    n_tiles = s_pad // LANES
    rows = g_seq * 8
    slot = _prefetch_step(lambda step, sl: _page_dmas(
        pt_ref, step * g_seq, g_seq, n_pages, kidx_hbm,
        lambda g, p: kibuf.at[sl, g, :, pl.ds(p * page, page)], sem.at[sl]))

    idxbuf[...] = jnp.full((rows, s_pad), -jnp.inf, F32)
    tt = lax.broadcasted_iota(jnp.int32, (steps, 8), 0)
    jj = lax.broadcasted_iota(jnp.int32, (steps, 8), 1)
    for g in range(g_seq):
        qs = qi_ref[g].astype(BF16)
        w = wi_ref[g]
        sc = _bdot(qs, kibuf[slot, g])
        val = jnp.maximum(sc, 0.0) * jnp.concatenate([w] * (past // LANES), axis=1)
        idxbuf[g * 8:g * 8 + steps, 0:past] = jnp.sum(val.reshape(steps, IDX_HEADS, past), axis=1)
        scn = _bdot_nt(qs, kin_ref[g])
        valn = jnp.maximum(scn, 0.0) * w[:, 0:8]
        idn = jnp.sum(valn.reshape(steps, IDX_HEADS, 8), axis=1)
        idxbuf[g * 8:g * 8 + steps, past:past + 8] = jnp.where(jj <= tt, idn, -jnp.inf)

    bits = pltpu.bitcast(idxbuf[...], jnp.int32)
    keybuf[...] = jnp.where(bits < 0, bits ^ jnp.int32(0x7FFFFFFF), bits)

    def tile(c):
        return keybuf[:, c * LANES:(c + 1) * LANES]

    def bit_pass(b, carry):
        thr, cnt_gt = carry
        cand = thr + jnp.left_shift(jnp.int32(1), 31 - b)
        part = jnp.zeros((rows, LANES), F32)
        for c in range(n_tiles):
            part = part + jnp.where(tile(c) >= cand, 1.0, 0.0)
        total = jnp.sum(part, axis=1, keepdims=True)
        ok = total >= float(topk)
        return jnp.where(ok, cand, thr), jnp.where(ok, cnt_gt, total)
    thr, cnt_gt = lax.fori_loop(0, 32, bit_pass, (jnp.full((rows, 1), INT_MIN, jnp.int32),
                                                  jnp.zeros((rows, 1), F32)))
    need_eq = float(topk) - cnt_gt

    incl = jnp.where(lax.broadcasted_iota(jnp.int32, (LANES, LANES), 0)
                     <= lax.broadcasted_iota(jnp.int32, (LANES, LANES), 1), 1.0, 0.0).astype(BF16)
    eq_rows = jnp.concatenate([jnp.where(tile(c) == thr, 1.0, 0.0).astype(BF16) for c in range(n_tiles)], axis=0)
    pre = jnp.dot(eq_rows, incl, preferred_element_type=F32)
    run = jnp.zeros((rows, 1), F32)
    for c in range(n_tiles):
        key = tile(c)
        pc = pre[c * rows:(c + 1) * rows] + run
        sel = (key > thr) | ((key == thr) & (pc <= need_eq))
        idxbuf[:, c * LANES:(c + 1) * LANES] = jnp.where(sel, 0.0, NEG_BIG)
        run = pc[:, LANES - 1:LANES]
    for g in range(g_seq):
        bias_ref[g] = idxbuf[g * 8:(g + 1) * 8, :]


def _dsa_select_sample(page_table, qi_st, wi_col, ki_new, cache_ki, topk, g_seq):
    bs, n_pages = page_table.shape
    page = cache_ki.shape[2]
    past = n_pages * page
    s_pad = past + LANES
    rows_q = qi_st.shape[1]
    blk = lambda r, c: pl.BlockSpec((g_seq, r, c), lambda i, pt: (i, 0, 0))
    kern = functools.partial(_dsa_select_sample_kernel, g_seq=g_seq, n_pages=n_pages, page=page,
                             steps=rows_q // IDX_HEADS, topk=topk)
    return pl.pallas_call(
        kern,
        grid_spec=pltpu.PrefetchScalarGridSpec(
            num_scalar_prefetch=1, grid=(bs // g_seq,),
            in_specs=[blk(rows_q, IDX_DIM), blk(rows_q, LANES), blk(8, IDX_DIM), pl.BlockSpec(memory_space=pl.ANY)],
            out_specs=blk(8, s_pad),
            scratch_shapes=[pltpu.VMEM((2, g_seq, IDX_DIM, past), F32), pltpu.SemaphoreType.DMA((2,)),
                            pltpu.VMEM((g_seq * 8, s_pad), F32), pltpu.VMEM((g_seq * 8, s_pad), jnp.int32)]),
        out_shape=jax.ShapeDtypeStruct((bs, 8, s_pad), F32),
        compiler_params=_params(1), name="dsa_select_sample")(page_table.reshape(-1), qi_st, wi_col, ki_new, cache_ki)


def _dsa_attend_sample_kernel(pt_ref, qbd_ref, zbd_ref, bias_ref, knew_ref, vnew_ref, meta_ref, k_hbm, v_hbm,
                              out_ref, kbuf, vbuf, sem, *, n_pages, page, steps):
    past = n_pages * page

    def dmas(step, sl):
        return (_page_dmas(pt_ref, step, 1, n_pages, k_hbm, lambda g, p: kbuf.at[sl, :, pl.ds(p * page, page)],
                           sem.at[0, sl])
                + _page_dmas(pt_ref, step, 1, n_pages, v_hbm, lambda g, p: vbuf.at[sl, :, pl.ds(p * page, page)],
                             sem.at[1, sl]))
    slot = _prefetch_step(dmas)

    q = qbd_ref[0].astype(BF16)
    slope = meta_ref[:, 0:1]
    tq = meta_ref[:, 1:2]
    bias = jnp.concatenate([jnp.broadcast_to(bias_ref[0, t:t + 1, :], (AT_HEADS, past + LANES))
                            for t in range(steps)], axis=0)
    pos_old = lax.broadcasted_iota(jnp.int32, (1, past), 1).astype(F32)
    l_old = _bdot(q, kbuf[slot]) - slope * ((float(past) + tq) - pos_old) + bias[:, 0:past]
    pos_new = lax.broadcasted_iota(jnp.int32, (1, 8), 1).astype(F32)
    l_new = _bdot_nt(q, knew_ref[0]) - slope * (tq - pos_new) + bias[:, past:past + 8]
    m = jnp.maximum(jnp.max(l_old, axis=1, keepdims=True), jnp.max(l_new, axis=1, keepdims=True))
    p_old = jnp.exp(l_old - m)
    p_new = jnp.exp(l_new - m)
    den = jnp.sum(p_old, axis=1, keepdims=True) + jnp.sum(p_new, axis=1, keepdims=True)
    o = _bdot_nt(p_old, vbuf[slot]) + _bdot(p_new, vnew_ref[0])
    out_ref[0] = o / den * zbd_ref[0]


def _dsa_attend_sample(page_table, qbd, zbd, bias, k_new, v_new, meta, cache_k, cache_v):
    bs, n_pages = page_table.shape
    page = cache_k.shape[2]
    past = n_pages * page
    rows = qbd.shape[1]
    seq = lambda r, c: pl.BlockSpec((1, r, c), lambda i, pt: (i, 0, 0))
    hbm = pl.BlockSpec(memory_space=pl.ANY)
    kern = functools.partial(_dsa_attend_sample_kernel, n_pages=n_pages, page=page, steps=rows // AT_HEADS)
    return pl.pallas_call(
        kern,
        grid_spec=pltpu.PrefetchScalarGridSpec(
            num_scalar_prefetch=1, grid=(bs,),
            in_specs=[seq(rows, AT_KV_DIM), seq(rows, AT_KV_DIM), seq(8, past + LANES), seq(8, AT_KV_DIM),
                      seq(8, AT_KV_DIM), pl.BlockSpec(meta.shape, lambda i, pt: (0, 0)), hbm, hbm],
            out_specs=seq(rows, AT_KV_DIM),
            scratch_shapes=[pltpu.VMEM((2, AT_KV_DIM, past), F32), pltpu.VMEM((2, AT_KV_DIM, past), F32),
                            pltpu.SemaphoreType.DMA((2, 2))]),
        out_shape=jax.ShapeDtypeStruct((bs, rows, AT_KV_DIM), F32),
        compiler_params=_params(1), name="dsa_attend_sample")(
            page_table.reshape(-1), qbd, zbd, bias, k_new, v_new, meta, cache_k, cache_v)


def _dsa_kernel(qaug_ref, qi_ref, wi_ref, z_ref, kaug_ref, vaug_ref, ki_ref, act_ref,
                keys_ref, bias_ref, mrun_ref, m_ref, acc_ref, p_ref, *, t, topk):
    kb_sz = KEY_BLOCK
    pos0 = pl.program_id(1) * t
    n_kb = (pos0 + t - 1) // kb_sz + 1
    rowpos = pos0 + lax.broadcasted_iota(jnp.int32, (t, 1), 0)
    lane = lax.broadcasted_iota(jnp.int32, (1, kb_sz), 1)

    def score_block(kb, carry):
        kib = ki_ref[kb].astype(BF16)
        acc = jnp.zeros((t, kb_sz), F32)
        for i in range(IDX_HEADS):
            sc = _bdot(qi_ref[:, i * IDX_DIM:(i + 1) * IDX_DIM], kib)
            acc = acc + jnp.maximum(sc, 0.0) * wi_ref[:, i:i + 1]
        acc = jnp.where(kb * kb_sz + lane <= rowpos, acc, -jnp.inf)
        bits = pltpu.bitcast(acc, jnp.int32)
        keys_ref[kb] = jnp.where(bits < 0, bits ^ jnp.int32(0x7FFFFFFF), bits)
        return carry
    lax.fori_loop(0, n_kb, score_block, 0)

    def bit_pass(b, carry):
        thr, cnt_gt = carry
        cand = thr + jnp.left_shift(jnp.int32(1), 31 - b)

        def body(kb, cnt):
            hit = jnp.where(keys_ref[kb] >= cand, 1.0, 0.0)
            for s in range(kb_sz // LANES):
                cnt = cnt + hit[:, s * LANES:(s + 1) * LANES]
            return cnt
        cnt = lax.fori_loop(0, n_kb, body, jnp.zeros((t, LANES), F32))
        total = jnp.sum(cnt, axis=1, keepdims=True)
        ok = total >= float(topk)
        return jnp.where(ok, cand, thr), jnp.where(ok, cnt_gt, total)
    thr, cnt_gt = lax.fori_loop(0, 32, bit_pass, (jnp.full((t, 1), INT_MIN, jnp.int32), jnp.zeros((t, 1), F32)))
    need_eq = float(topk) - cnt_gt

    mrun_ref[...] = jnp.full(mrun_ref.shape, NEG_BIG, F32)
    incl = jnp.where(lax.broadcasted_iota(jnp.int32, (kb_sz, kb_sz), 0)
                     <= lax.broadcasted_iota(jnp.int32, (kb_sz, kb_sz), 1), 1.0, 0.0).astype(BF16)
    r_per_g = AT_HEADS // AT_KV_HEADS
    nt = (((1,), (1,)), ((), ()))

    def logits(h, kg, mask_bias):
        return lax.dot_general(qaug_ref[:, h * LANES:(h + 1) * LANES], kg, nt,
                               preferred_element_type=F32) + mask_bias

    def max_block(kb, run_eq):
        ks = pl.ds(pl.multiple_of(kb * kb_sz, kb_sz), kb_sz)
        key = keys_ref[kb]
        eq = key == thr
        pre = jnp.dot(jnp.where(eq, 1.0, 0.0).astype(BF16), incl, preferred_element_type=F32) + run_eq
        sel = ((key > thr) | (eq & (pre <= need_eq))) & (kb * kb_sz + lane <= rowpos)
        mask_bias = jnp.where(sel, 0.0, NEG_BIG)
        bias_ref[kb] = mask_bias
        for g in range(AT_KV_HEADS):
            kg = kaug_ref[ks, g * LANES:(g + 1) * LANES]
            for r in range(r_per_g):
                h = g * r_per_g + r
                s = logits(h, kg, mask_bias)
                tile_max = s[:, 0:LANES]
                for c in range(1, kb_sz // LANES):
                    tile_max = jnp.maximum(tile_max, s[:, c * LANES:(c + 1) * LANES])
                mrun_ref[h] = jnp.maximum(mrun_ref[h], tile_max)
        return pre[:, kb_sz - 1:kb_sz]
    lax.fori_loop(0, n_kb, max_block, jnp.zeros((t, 1), F32))
    for h in range(AT_HEADS):
        m_ref[h] = jnp.max(mrun_ref[h], axis=1, keepdims=True)

    acc_ref[...] = jnp.zeros(acc_ref.shape, F32)

    def attend_block(kb, carry):
        ks = pl.ds(pl.multiple_of(kb * kb_sz, kb_sz), kb_sz)
        mask_bias = bias_ref[kb]
        for g in range(AT_KV_HEADS):
            kg = kaug_ref[ks, g * LANES:(g + 1) * LANES]
            for r in range(r_per_g):
                h = g * r_per_g + r
                p_ref[h] = jnp.exp2(logits(h, kg, mask_bias) - m_ref[h]).astype(BF16)
        for g in range(AT_KV_HEADS):
            vg = vaug_ref[ks, g * LANES:(g + 1) * LANES]
            for r in range(r_per_g):
                h = g * r_per_g + r
                acc_ref[h] = acc_ref[h] + jnp.dot(p_ref[h], vg, preferred_element_type=F32)
        return carry
    lax.fori_loop(0, n_kb, attend_block, 0)

    for h in range(AT_HEADS):
        hs = slice(h * AT_HEAD_DIM, (h + 1) * AT_HEAD_DIM)
        a = acc_ref[h]
        o = a[:, 0:AT_HEAD_DIM] / a[:, AT_HEAD_DIM:AT_HEAD_DIM + 1]
        act_ref[:, hs] = (o * z_ref[:, hs]).astype(act_ref.dtype)


def _dsa(qaug, qi, wi, z, kaug, vaug, ki_all, batch, n_qb, t, s_len, topk):
    n = qaug.shape[0]
    qrow = lambda c: pl.BlockSpec((t, c), lambda b, j: (b * n_qb + j, 0))
    krow = lambda c: pl.BlockSpec((s_len, c), lambda b, j: (b, 0))
    n_kb = s_len // KEY_BLOCK
    kern = functools.partial(_dsa_kernel, t=t, topk=topk)
    return pl.pallas_call(
        kern, grid=(batch, n_qb),
        in_specs=[qrow(AT_HEADS * LANES), qrow(IDX_HEADS * IDX_DIM), qrow(LANES), qrow(D_MODEL),
                  krow(AT_KV_HEADS * LANES), krow(AT_KV_HEADS * LANES),
                  pl.BlockSpec((None, n_kb, IDX_DIM, KEY_BLOCK), lambda b, j: (b, 0, 0, 0))],
        out_specs=qrow(D_MODEL), out_shape=jax.ShapeDtypeStruct((n, D_MODEL), BF16),
        scratch_shapes=[pltpu.VMEM((n_kb, t, KEY_BLOCK), jnp.int32), pltpu.VMEM((n_kb, t, KEY_BLOCK), F32),
                        pltpu.VMEM((AT_HEADS, t, LANES), F32), pltpu.VMEM((AT_HEADS, t, 1), F32),
                        pltpu.VMEM((AT_HEADS, t, LANES), F32), pltpu.VMEM((AT_HEADS, t, KEY_BLOCK), BF16)],
        compiler_params=_params(2), name="dsa")(qaug, qi, wi, z, kaug, vaug, ki_all)


def _gm_kernel(*refs, tm, chunk, emit_v):
    if emit_v:
        (x_ref, win_ref, bin_ref, g_ref, b_ref, wmix_ref, bmix_ref, act_ref, v_ref, ubuf, vbuf, zbuf) = refs
    else:
        (x_ref, win_ref, bin_ref, g_ref, b_ref, wmix_ref, bmix_ref, act_ref, ubuf, vbuf, zbuf) = refs
    xb = x_ref[...].astype(BF16)
    w = GM_WIDTH
    cols = 512
    for cc in range(w // cols):
        cs = slice(cc * cols, (cc + 1) * cols)
        proj = lambda off: (jnp.dot(xb, win_ref[:, off + cc * cols:off + (cc + 1) * cols],
                                    preferred_element_type=F32) + bin_ref[:, off + cc * cols:off + (cc + 1) * cols])
        ubuf[:, cs] = _gelu_tanh(proj(0))
        vbuf[:, cs] = _gelu_tanh(proj(w))
        zbuf[:, cs] = _silu(proj(2 * w))

    rows = 32

    def norm_rows(r, carry):
        rs = pl.ds(pl.multiple_of(r * rows, rows), rows)
        v = _layer_norm(vbuf[rs, :], g_ref[...], b_ref[...])
        vbuf[rs, :] = v
        if emit_v:
            v_ref[rs, :] = v
        return carry
    lax.fori_loop(0, tm // rows, norm_rows, 0)

    gw = w // GM_GROUPS
    for c in range(tm // chunk):
        rs = slice(c * chunk, (c + 1) * chunk)
        for g in range(GM_GROUPS):
            cs = slice(g * gw, (g + 1) * gw)
            s = jnp.dot(wmix_ref[g], vbuf[rs, cs].astype(BF16), preferred_element_type=F32) + bmix_ref[:, g:g + 1]
            act_ref[rs, cs] = (ubuf[rs, cs] * s * zbuf[rs, cs]).astype(act_ref.dtype)


def _gm(x, w, tm, chunk, emit_v):
    n = x.shape[0]
    row = lambda c: pl.BlockSpec((tm, c), lambda i: (i, 0))
    out_specs = [row(GM_WIDTH)]
    out_shape = [jax.ShapeDtypeStruct((n, GM_WIDTH), BF16)]
    if emit_v:
        out_specs.append(row(GM_WIDTH))
        out_shape.append(jax.ShapeDtypeStruct((n, GM_WIDTH), F32))
    kern = functools.partial(_gm_kernel, tm=tm, chunk=chunk, emit_v=emit_v)
    return pl.pallas_call(
        kern, grid=(n // tm,),
        in_specs=[row(D_MODEL)] + [_const_spec(a.shape) for a in w],
        out_specs=out_specs, out_shape=out_shape,
        scratch_shapes=[pltpu.VMEM((tm, GM_WIDTH), F32)] * 3,
        compiler_params=_params(1), name="gmlp")(x, *w)


def _pad_lanes(a, width=LANES):
    return jnp.pad(a, [(0, 0)] * (a.ndim - 1) + [(0, width - a.shape[-1])])


def kernel(x_prompt, x_sample, p_prompt, p_sample, state_cf_conv, state_mb_conv, state_mb_ssm,
           cache_k, cache_v, cache_kidx, page_table, post_ln_g, post_ln_b, ple_w, ple_gate_w,
           cf_w_in, cf_b_in, cf_w_dw, cf_b_dw, cf_ln_g, cf_ln_b, cf_w_out, cf_b_out,
           mb_w_in, mb_w_conv, mb_b_conv, mb_dt_bias, mb_a_log, mb_d_skip, mb_norm_g, mb_w_out,
           at_w_in, at_ki_ln_g, at_ki_ln_b, at_w_out,
           gm_w_in, gm_b_in, gm_ln_g, gm_ln_b, gm_w_s, gm_b_s, gm_w_out):
    bp, lp, d = x_prompt.shape
    bs, ls, _ = x_sample.shape
    row2 = lambda v: v.reshape(1, -1)
    bf = lambda v: v.astype(BF16)

    ple_wb, ple_gate_wb = bf(ple_w), bf(ple_gate_w)
    cf_w = (bf(cf_w_in), row2(cf_b_in), cf_w_dw, row2(cf_b_dw), row2(cf_ln_g), row2(cf_ln_b))
    mb_in_w = (bf(mb_w_in[:, :MB_INNER]), bf(mb_w_in[:, MB_INNER:MB_INNER + MB_CONV_DIM]),
               bf(_pad_lanes(mb_w_in[:, MB_INNER + MB_CONV_DIM:])), _pad_lanes(row2(mb_dt_bias)),
               mb_w_conv, row2(mb_b_conv))
    ssd_w = (_pad_lanes(row2(mb_a_log)), row2(jnp.repeat(mb_d_skip, MB_HEAD_DIM)), row2(mb_norm_g))
    sizes = [AT_HEADS * AT_HEAD_DIM, AT_KV_DIM, AT_KV_DIM, IDX_HEADS * IDX_DIM, IDX_DIM, IDX_HEADS,
             AT_HEADS * AT_HEAD_DIM]
    offs = np.concatenate([[0], np.cumsum(sizes)]).tolist()
    at_parts = [at_w_in[:, offs[i]:offs[i + 1]] for i in range(len(sizes))]
    at_parts[5] = _pad_lanes(at_parts[5])
    at_w = tuple(bf(a) for a in at_parts) + (row2(at_ki_ln_g), row2(at_ki_ln_b))
    wq_b, wk_b, wv_b, wqi_b, wki_b, wwi_b, wz_b = at_w[:7]
    at_w_prompt = (_slot_weights(wq_b, AT_HEADS), wk_b.T, wv_b.T, _slot_weights(wk_b, AT_KV_HEADS),
                   _slot_weights(wv_b, AT_KV_HEADS), wqi_b, wki_b.T, wwi_b, wz_b,
                   at_ki_ln_g.reshape(-1, 1), at_ki_ln_b.reshape(-1, 1), jnp.asarray(_alibi_query_columns()))
    gm_common = (bf(gm_w_in), row2(gm_b_in), row2(gm_ln_g), row2(gm_ln_b))
    zeros_d = jnp.zeros((1, d), F32)
    out_w = [(bf(cf_w_out), row2(cf_b_out)), (bf(mb_w_out), zeros_d), (bf(at_w_out), zeros_d),
             (bf(gm_w_out), zeros_d)]

    post_g, post_b = post_ln_g.reshape(DEPTH, 1, d), post_ln_b.reshape(DEPTH, 1, d)

    def tail(i, act, x, p, tm):
        wo, bo = out_w[i]
        return _tail(i, act, x, p, wo, bo, post_g, post_b, ple_gate_wb, ple_wb, tm)

    tm = 512
    n_p = bp * lp
    x = x_prompt.reshape(n_p, d)
    pp = p_prompt.reshape(DEPTH, n_p, D_PLE)

    act, cf_conv_p = _cf_prompt(x, cf_w, bp, lp, tm)
    x = tail(0, act, x, pp, tm)

    z, xbc, dt, mb_conv_p = _mb_in_prompt(x, mb_in_w, bp, lp, tm)
    act, mb_ssm_p = _ssd(xbc, dt, z, *ssd_w, None, bp, MB_CHUNK, MB_CHUNK, lp // MB_CHUNK, BF16)
    x = tail(1, act, x, pp, tm)

    kt, vt, kit, kib, qi, wi, zs, qaug, kaug, vaug = _at_in_prompt(x, at_w_prompt, bp, lp, tm)
    act = _dsa(qaug, qi, wi, zs, kaug, vaug, kib, bp, lp // DSA_ROWS, DSA_ROWS, lp, min(TOPK_MAX, lp // 4))
    x = tail(2, act, x, pp, tm)
    k_p = kt.reshape(bp, AT_KV_HEADS, AT_HEAD_DIM, lp).transpose(0, 3, 1, 2)
    v_p = vt.reshape(bp, AT_KV_HEADS, AT_HEAD_DIM, lp).transpose(0, 3, 1, 2)
    kidx_p = kit.transpose(0, 2, 1)

    gm_w_p = gm_common + (bf(jnp.tril(gm_w_s)), gm_b_s.T)
    (act,) = _gm(x, gm_w_p, tm, GM_CHUNK, False)
    y_prompt = tail(3, act, x, pp, tm).reshape(bp, lp, d)

    n_s = bs * ls
    bb = 32
    to_tm = lambda a: jnp.swapaxes(a, 0, 1)
    x_tm = to_tm(x_sample)
    x = x_tm.reshape(n_s, d)
    ps = jnp.swapaxes(p_sample, 1, 2).reshape(DEPTH, n_s, D_PLE)
    tm_s = n_s

    act, cf_st = _cf_sample(x_tm, to_tm(state_cf_conv), cf_w, bb)
    cf_conv_s = to_tm(cf_st)
    x = tail(0, act.reshape(n_s, d), x, ps, tm_s)

    z, xbc, dt, mb_st = _mb_in_sample(x.reshape(ls, bs, d), to_tm(state_mb_conv), mb_in_w, bb)
    mb_conv_s = to_tm(mb_st)
    qs = 8

    def to_bm(a):
        a = jnp.pad(to_tm(a), ((0, 0), (0, qs - ls), (0, 0)))
        return a.reshape(bs * qs, a.shape[-1])

    def from_bm(a):
        a = a.reshape(bs, qs, a.shape[-1])[:, :ls]
        return to_tm(a).reshape(n_s, a.shape[-1])

    act, mb_ssm_s = _ssd(to_bm(xbc), to_bm(dt), to_bm(z), *ssd_w, state_mb_ssm, bs, qs, ls, 1, F32)
    x = tail(1, from_bm(act), x, ps, tm_s)

    q, k, v, qi, ki, wi, zs = _at_in(x, at_w, tm_s)
    k_s = to_tm(k.reshape(ls, bs, AT_KV_HEADS, AT_HEAD_DIM))
    v_s = to_tm(v.reshape(ls, bs, AT_KV_HEADS, AT_HEAD_DIM))
    kidx_s = to_tm(ki.reshape(ls, bs, IDX_DIM))
    n_pool, page = cache_k.shape[:2]
    past = page_table.shape[1] * page
    seq_major = lambda a: to_tm(a.reshape(ls, bs, a.shape[-1]))
    r_per_g = AT_HEADS // AT_KV_HEADS
    group_of_head = (np.arange(AT_HEADS)[:, None] // r_per_g == np.arange(AT_KV_HEADS)[None, :]).astype(np.float32)

    def block_diag(a):
        a = a.reshape(bs, ls, AT_HEADS, 1, AT_HEAD_DIM) * group_of_head[None, None, :, :, None]
        return a.reshape(bs, ls * AT_HEADS, AT_KV_DIM)

    pad_steps = lambda a: jnp.pad(a, ((0, 0), (0, 8 - ls), (0, 0)))
    qi_st = seq_major(qi).reshape(bs, ls * IDX_HEADS, IDX_DIM)
    wi_col = jnp.broadcast_to(seq_major(wi)[:, :, :IDX_HEADS].reshape(bs, ls * IDX_HEADS, 1),
                              (bs, ls * IDX_HEADS, LANES))
    pages_t = lambda c: jnp.moveaxis(c, 1, -1).reshape(n_pool, -1, page)
    bias = _dsa_select_sample(page_table, qi_st, wi_col, pad_steps(seq_major(ki)), pages_t(cache_kidx),
                              min(TOPK_MAX, (past + ls) // 4), 8)
    meta = np.zeros((ls * AT_HEADS, LANES), np.float32)
    meta[:, 0] = np.tile(np.asarray(ALIBI_SLOPES, np.float32), ls)
    meta[:, 1] = np.repeat(np.arange(ls, dtype=np.float32), AT_HEADS)
    o = _dsa_attend_sample(page_table, block_diag(seq_major(q)), block_diag(seq_major(zs)), bias,
                           pad_steps(seq_major(k)), pad_steps(seq_major(v)), jnp.asarray(meta),
                           pages_t(cache_k), pages_t(cache_v))
    act = o.reshape(bs, ls, AT_HEADS, AT_KV_HEADS, AT_HEAD_DIM).sum(axis=3).reshape(bs, ls, d)
    x = tail(2, to_tm(act).reshape(n_s, d), x, ps, tm_s)

    mix = jnp.stack([jnp.kron(jnp.tril(gm_w_s[g, :ls, :ls]), jnp.eye(bs, dtype=F32)) for g in range(GM_GROUPS)])
    gm_w_smp = gm_common + (bf(mix), jnp.repeat(gm_b_s[:, :ls].T, bs, axis=0))
    act, gm_v = _gm(x, gm_w_smp, tm_s, n_s, True)
    y_s = tail(3, act, x, ps, tm_s)
    y_sample = to_tm(y_s.reshape(ls, bs, d))
    gm_v_s = to_tm(gm_v.reshape(ls, bs, GM_WIDTH))

    return (y_prompt, y_sample, cf_conv_p, cf_conv_s, mb_conv_p, mb_conv_s, mb_ssm_p, mb_ssm_s,
            k_p, v_p, kidx_p, k_s, v_s, kidx_s, gm_v_s)
```
